```python
import math
import jax
import jax.numpy as jnp
from jax import lax
import numpy as np

D_MODEL = 1024
BATCH = 32
SEQ = 256
DEPTH = 4
DEC_BATCH = 4
DEC_SEQ = 1024
PAST_LEN = 256

GRID_W = 64
SSD_HEADS = 8
SSD_HEADDIM = 64
SSD_DIM = SSD_HEADS * SSD_HEADDIM
SSD_GROUPS = 2
SSD_STATE = 64
SSD_CONV = 5
SSD_CHUNK = 128
SSD_CONV_DIM = SSD_DIM + 2 * SSD_GROUPS * SSD_STATE
MLA_HEADS = 4
Q_LORA = 256
KV_LORA = 128
QK_NOPE = 64
ROPE_DIM = 32
V_DIM = 64
MLA_DIM = MLA_HEADS * V_DIM
MLA_SCALE = (QK_NOPE + ROPE_DIM) ** -0.5
ROPE_BASE = 10000.0
Q_BLOCK = 128
FNET_GROUPS = 4
FNET_GW = 64
FNET_DIM = FNET_GROUPS * FNET_GW
MIX_DIM = SSD_DIM + MLA_DIM + FNET_DIM
IN_DIM = SSD_DIM + SSD_CONV_DIM + 2 * SSD_HEADS + Q_LORA + KV_LORA + ROPE_DIM + FNET_DIM
N_EXPERTS = 16
EXPERT_FF = 1024
CAPACITY_FACTOR = 2
DN_ALPHA = (2 * DEPTH) ** 0.25
DN_BETA = (8 * DEPTH) ** -0.25
EPS = 1e-5

kernel_name = "hybrid_ssd_mla_fnet_ec_dit_step"


def ln_plain(x):
    xf = x.astype(jnp.float32)
    mu = jnp.mean(xf, -1, keepdims=True)
    var = jnp.mean(jnp.square(xf - mu), -1, keepdims=True)
    return ((xf - mu) * lax.rsqrt(var + EPS)).astype(x.dtype)


def layer_norm(x, g, b):
    return ln_plain(x) * g + b


def rms_norm(x, g):
    xf = x.astype(jnp.float32)
    y = xf * lax.rsqrt(jnp.mean(jnp.square(xf), -1, keepdims=True) + EPS)
    return (y * g.astype(jnp.float32)).astype(x.dtype)


def segsum(x):
    t = x.shape[-1]
    xc = jnp.cumsum(x, axis=-1)
    d = xc[..., :, None] - xc[..., None, :]
    return jnp.where(jnp.tril(jnp.ones((t, t), bool)), d, -jnp.inf)


def ssd_scan(xdt, a, bm, cm, h0):
    b, s, h, p = xdt.shape
    n = bm.shape[-1]
    nc = s // SSD_CHUNK
    xc = xdt.reshape(b, nc, SSD_CHUNK, h, p)
    bc = bm.reshape(b, nc, SSD_CHUNK, h, n)
    cc = cm.reshape(b, nc, SSD_CHUNK, h, n)
    ac = jnp.transpose(a.reshape(b, nc, SSD_CHUNK, h), (0, 3, 1, 2))
    a_cum = jnp.cumsum(ac, axis=-1)
    scores = jnp.einsum("bclhn,bcshn->bhcls", cc, bc) * jnp.exp(segsum(ac))
    y_diag = jnp.einsum("bhcls,bcshp->bclhp", scores, xc)
    decay_to_end = jnp.exp(a_cum[..., -1:] - a_cum)
    chunk_states = jnp.einsum("bclhn,bhcl,bclhp->bchpn", bc, decay_to_end, xc)
    chunk_states = jnp.concatenate([h0[:, None], chunk_states], axis=1)
    decay_chunks = jnp.exp(segsum(jnp.pad(a_cum[..., -1], ((0, 0), (0, 0), (1, 0)))))
    states = jnp.einsum("bhzc,bchpn->bzhpn", decay_chunks, chunk_states)
    y_off = jnp.einsum("bclhn,bchpn,bhcl->bclhp", cc, states[:, :-1], jnp.exp(a_cum))
    return (y_diag + y_off).reshape(b, s, h, p), states[:, -1]


def depthwise_conv(x, w, bias):
    pad = (SSD_CONV - 1) // 2
    out = lax.conv_general_dilated(x, w[:, None, :], window_strides=(1,), padding=[(pad, pad)],
                                   dimension_numbers=("NWC", "WIO", "NWC"),
                                   feature_group_count=x.shape[-1])
    return out + bias


def ssd_mixer(z, xbc, dt_raw, lp, h0):
    b, s, _ = xbc.shape
    xbc = jax.nn.silu(depthwise_conv(xbc, lp["conv_w"], lp["conv_b"])).astype(jnp.float32)
    xs, bm, cm = jnp.split(xbc, [SSD_DIM, SSD_DIM + SSD_GROUPS * SSD_STATE], axis=-1)
    xs = xs.reshape(b, s, SSD_HEADS, SSD_HEADDIM)
    rep = SSD_HEADS // SSD_GROUPS
    bm = jnp.repeat(bm.reshape(b, s, SSD_GROUPS, SSD_STATE), rep, axis=2)
    cm = jnp.repeat(cm.reshape(b, s, SSD_GROUPS, SSD_STATE), rep, axis=2)
    dt = jax.nn.softplus(dt_raw.astype(jnp.float32).reshape(b, s, 2, SSD_HEADS)
                         + lp["dt_bias"].astype(jnp.float32))
    a_neg = -jnp.exp(lp["a_log"].astype(jnp.float32))
    h0 = h0.astype(jnp.float32)
    y_f, h_f = ssd_scan(xs * dt[:, :, 0, :, None], dt[:, :, 0] * a_neg[0], bm, cm, h0[:, 0])
    flip = lambda t: jnp.flip(t, axis=1)
    y_b, h_b = ssd_scan(flip(xs * dt[:, :, 1, :, None]), flip(dt[:, :, 1] * a_neg[1]),
                        flip(bm), flip(cm), h0[:, 1])
    y = y_f + flip(y_b) + xs * lp["d_skip"].astype(jnp.float32)[:, None]
    y = y.reshape(b, s, SSD_DIM) * jax.nn.silu(z.astype(jnp.float32))
    y = rms_norm(y, lp["ssd_norm_g"])
    return y.astype(z.dtype), jnp.stack([h_f, h_b], axis=1)


def axial_rope_tables(rows):
    r, col = jnp.meshgrid(jnp.arange(rows, dtype=jnp.float32),
                          jnp.arange(GRID_W, dtype=jnp.float32), indexing="ij")
    pairs = ROPE_DIM // 4
    inv = ROPE_BASE ** (-jnp.arange(pairs, dtype=jnp.float32) / pairs)
    ang = jnp.concatenate([r.reshape(-1, 1) * inv, col.reshape(-1, 1) * inv], axis=-1)
    return jnp.cos(ang), jnp.sin(ang)


def apply_rope(x, cos, sin):
    xp = x.reshape(x.shape[:-1] + (ROPE_DIM // 2, 2))
    x0, x1 = xp[..., 0], xp[..., 1]
    cos = cos.astype(x.dtype)
    sin = sin.astype(x.dtype)
    return jnp.stack([x0 * cos - x1 * sin, x0 * sin + x1 * cos], axis=-1).reshape(x.shape)


def block_attention(q, k, v):
    b, s, h, dk = q.shape
    nb = s // Q_BLOCK
    qb = jnp.moveaxis(q.reshape(b, nb, Q_BLOCK, h, dk), 1, 0)

    def attend_block(qi):
        sc = jnp.einsum("bqhd,bkhd->bhqk", qi, k).astype(jnp.float32) * MLA_SCALE
        p = jax.nn.softmax(sc, axis=-1).astype(v.dtype)
        return jnp.einsum("bhqk,bkhe->bqhe", p, v)

    o = lax.map(attend_block, qb)
    return jnp.moveaxis(o, 0, 1).reshape(b, s, h, v.shape[-1])


def mla_mixer(c_q, c_kv, k_rope, lp, kv_ctx, rope):
    b, s, _ = c_q.shape
    q = (rms_norm(c_q, lp["q_norm_g"]) @ lp["w_uq"]).reshape(b, s, MLA_HEADS, QK_NOPE + ROPE_DIM)
    q_nope, q_rope = q[..., :QK_NOPE], q[..., QK_NOPE:]
    ckv = rms_norm(c_kv, lp["kv_norm_g"])
    if kv_ctx is None:
        ckv_all, kr_all = ckv, k_rope
    else:
        cos, sin = rope
        q_rope = apply_rope(q_rope, cos[:, None], sin[:, None])
        ckv_all = jnp.concatenate([kv_ctx[0].astype(ckv.dtype), ckv], axis=1)
        kr_all = jnp.concatenate([kv_ctx[1].astype(k_rope.dtype), apply_rope(k_rope, cos, sin)], axis=1)
    kv = (ckv_all @ lp["w_ukv"]).reshape(b, -1, MLA_HEADS, QK_NOPE + V_DIM)
    n_keys = kv.shape[1]
    k = jnp.concatenate([kv[..., :QK_NOPE],
                         jnp.broadcast_to(kr_all[:, :, None], (b, n_keys, MLA_HEADS, ROPE_DIM))], axis=-1)
    o = block_attention(jnp.concatenate([q_nope, q_rope], axis=-1), k, kv[..., QK_NOPE:])
    return o.reshape(b, s, MLA_DIM), ckv, k_rope


def fourier_mixer(f_in):
    b, s, _ = f_in.shape
    g = f_in.astype(jnp.float32).reshape(b, s, FNET_GROUPS, FNET_GW)
    out = jnp.fft.fft2(g, axes=(1, 3), norm="ortho").real
    return out.reshape(b, s, FNET_DIM).astype(f_in.dtype)


def expert_choice_ffn(u, lp):
    b, s, d = u.shape
    tok = u.reshape(b * s, d)
    cap = CAPACITY_FACTOR * b * s // N_EXPERTS
    aff = jax.nn.softmax((tok @ lp["w_router"]).astype(jnp.float32), axis=-1)
    gate, idx = lax.top_k(aff.T, cap)
    xe = tok[idx]
    h = jax.nn.silu(jnp.einsum("ecd,edf->ecf", xe, lp["w_gate"])) * jnp.einsum("ecd,edf->ecf", xe, lp["w_up"])
    ye = jnp.einsum("ecf,efd->ecd", h, lp["w_down"]) * gate[..., None].astype(u.dtype)
    out = jnp.zeros_like(tok).at[idx.reshape(-1)].add(ye.reshape(-1, d))
    return out.reshape(b, s, d)


def trunk_layer(x, cond, lp, cache, rope):
    mod = jax.nn.silu(cond) @ lp["w_mod"] + lp["b_mod"]
    sh1, sc1, g1, sh2, sc2, g2 = jnp.split(mod[:, None, :], 6, axis=-1)
    u = ln_plain(x) * (1 + sc1) + sh1
    splits = np.cumsum([SSD_DIM, SSD_CONV_DIM, 2 * SSD_HEADS, Q_LORA, KV_LORA, ROPE_DIM]).tolist()
    z, xbc, dt_raw, c_q, c_kv, k_rope, f_in = jnp.split(u @ lp["w_in"], splits, axis=-1)
    if cache is None:
        h0 = jnp.zeros((x.shape[0], 2, SSD_HEADS, SSD_HEADDIM, SSD_STATE), jnp.float32)
        kv_ctx = None
    else:
        kv_ctx = (cache[0], cache[1])
        h0 = cache[2]
    y_ssd, h_final = ssd_mixer(z, xbc, dt_raw, lp, h0)
    y_mla, ckv, kr = mla_mixer(c_q, c_kv, k_rope, lp, kv_ctx, rope)
    y_fft = fourier_mixer(f_in)
    mix = jnp.concatenate([y_ssd, y_mla, y_fft], axis=-1) @ lp["w_out"]
    x = layer_norm(DN_ALPHA * x + g1 * mix, lp["ln1_g"], lp["ln1_b"])
    u2 = ln_plain(x) * (1 + sc2) + sh2
    x = layer_norm(DN_ALPHA * x + g2 * expert_choice_ffn(u2, lp), lp["ln2_g"], lp["ln2_b"])
    return x, (ckv, kr, h_final.astype(x.dtype))


def setup_inputs(seed: int = 0) -> dict:
    key = jax.random.key(seed)
    k = jax.random.split(key, 32)
    f32 = jnp.float32
    L = DEPTH

    def nrm(i, shape, scale):
        return jax.random.normal(k[i], shape, f32) * scale

    dt0 = jnp.exp(jax.random.uniform(k[10], (L, 2, SSD_HEADS), f32, math.log(1e-3), math.log(1e-1)))
    return {
        "x_prompt": nrm(0, (BATCH, SEQ, D_MODEL), 1.0),
        "x_sample": nrm(1, (DEC_BATCH, DEC_SEQ, D_MODEL), 1.0),
        "cache_ckv": nrm(2, (DEC_BATCH, L, PAST_LEN, KV_LORA), 1.0),
        "cache_krope": nrm(3, (DEC_BATCH, L, PAST_LEN, ROPE_DIM), 1.0),
        "state_ssm": nrm(4, (DEC_BATCH, L, 2, SSD_HEADS, SSD_HEADDIM, SSD_STATE), 0.5),
        "c": nrm(5, (DEC_BATCH, D_MODEL), 1.0),
        "c_ctx": nrm(6, (D_MODEL,), 1.0),
        "w_in": nrm(7, (L, D_MODEL, IN_DIM), D_MODEL ** -0.5),
        "conv_w": nrm(8, (L, SSD_CONV, SSD_CONV_DIM), SSD_CONV ** -0.5),
        "conv_b": nrm(9, (L, SSD_CONV_DIM), 0.02),
        "dt_bias": dt0 + jnp.log(-jnp.expm1(-dt0)),
        "a_log": jnp.log(jax.random.uniform(k[11], (L, 2, SSD_HEADS), f32, 1.0, 16.0)),
        "d_skip": 1.0 + nrm(12, (L, SSD_HEADS), 0.1),
        "ssd_norm_g": 1.0 + nrm(13, (L, SSD_DIM), 0.02),
        "q_norm_g": 1.0 + nrm(14, (L, Q_LORA), 0.02),
        "w_uq": nrm(15, (L, Q_LORA, MLA_HEADS * (QK_NOPE + ROPE_DIM)), Q_LORA ** -0.5),
        "kv_norm_g": 1.0 + nrm(16, (L, KV_LORA), 0.02),
        "w_ukv": nrm(17, (L, KV_LORA, MLA_HEADS * (QK_NOPE + V_DIM)), KV_LORA ** -0.5),
        "w_out": nrm(18, (L, MIX_DIM, D_MODEL), MIX_DIM ** -0.5 * DN_BETA),
        "w_mod": nrm(19, (L, D_MODEL, 6 * D_MODEL), D_MODEL ** -0.5),
        "b_mod": nrm(20, (L, 6 * D_MODEL), 0.02),
        "ln1_g": 1.0 + nrm(21, (L, D_MODEL), 0.02),
        "ln1_b": nrm(22, (L, D_MODEL), 0.02),
        "ln2_g": 1.0 + nrm(23, (L, D_MODEL), 0.02),
        "ln2_b": nrm(24, (L, D_MODEL), 0.02),
        "w_router": nrm(25, (L, D_MODEL, N_EXPERTS), D_MODEL ** -0.5),
        "w_gate": nrm(26, (L, N_EXPERTS, D_MODEL, EXPERT_FF), D_MODEL ** -0.5),
        "w_up": nrm(27, (L, N_EXPERTS, D_MODEL, EXPERT_FF), D_MODEL ** -0.5),
        "w_down": nrm(28, (L, N_EXPERTS, EXPERT_FF, D_MODEL), EXPERT_FF ** -0.5 * DN_BETA),
    }


def reference(x_prompt, x_sample, cache_ckv, cache_krope, state_ssm, c, c_ctx,
              w_in, conv_w, conv_b, dt_bias, a_log, d_skip, ssd_norm_g,
              q_norm_g, w_uq, kv_norm_g, w_ukv, w_out, w_mod, b_mod,
              ln1_g, ln1_b, ln2_g, ln2_b, w_router, w_gate, w_up, w_down):
    rows = x_sample.shape[1] // GRID_W
    rope = axial_rope_tables(rows)
    ctx_cond = c_ctx[None, :]
    y_p, y_s = x_prompt, x_sample
    ckvs, krs, ssms = [], [], []
    for l in range(DEPTH):
        lp = {
            "w_in": w_in[l], "conv_w": conv_w[l], "conv_b": conv_b[l], "dt_bias": dt_bias[l],
            "a_log": a_log[l], "d_skip": d_skip[l], "ssd_norm_g": ssd_norm_g[l],
            "q_norm_g": q_norm_g[l], "w_uq": w_uq[l], "kv_norm_g": kv_norm_g[l], "w_ukv": w_ukv[l],
            "w_out": w_out[l], "w_mod": w_mod[l], "b_mod": b_mod[l],
            "ln1_g": ln1_g[l], "ln1_b": ln1_b[l], "ln2_g": ln2_g[l], "ln2_b": ln2_b[l],
            "w_router": w_router[l], "w_gate": w_gate[l], "w_up": w_up[l], "w_down": w_down[l],
        }
        y_p, (ckv, kr, ssm) = trunk_layer(y_p, ctx_cond, lp, None, None)
        ckvs.append(ckv)
        krs.append(kr)
        ssms.append(ssm)
        y_s, _ = trunk_layer(y_s, c, lp, (cache_ckv[:, l], cache_krope[:, l], state_ssm[:, l]), rope)
    new_ckv = jnp.stack(ckvs, axis=1)
    new_krope = jnp.stack(krs, axis=1)
    new_ssm = jnp.stack(ssms, axis=1)
    return (y_p, y_s, new_ckv, new_krope, new_ssm)
```

```python
import functools

import jax
import jax.numpy as jnp
import numpy as np
from jax import lax
from jax.experimental import pallas as pl
from jax.experimental.pallas import tpu as pltpu

F32 = jnp.float32
BF16 = jnp.bfloat16
I32 = jnp.int32
U32 = jnp.uint32

D_MODEL = 1024
DEPTH = 4
GRID_W = 64
SSD_HEADS = 8
SSD_HEADDIM = 64
SSD_DIM = SSD_HEADS * SSD_HEADDIM
SSD_STATE = 64
SSD_CONV = 5
SSD_CHUNK = 128
SSD_CONV_DIM = SSD_DIM + 4 * SSD_STATE
MLA_HEADS = 4
Q_LORA = 256
KV_LORA = 128
QK_NOPE = 64
ROPE_DIM = 32
V_DIM = 64
MLA_DIM = MLA_HEADS * V_DIM
MLA_SCALE = (QK_NOPE + ROPE_DIM) ** -0.5
ROPE_BASE = 10000.0
Q_TILE = 256
FNET_GW = 64
FNET_DIM = 256
N_EXPERTS = 16
EXPERT_FF = 1024
CAPACITY_FACTOR = 2
DN_ALPHA = (2 * DEPTH) ** 0.25
EPS = 1e-5

LANES = 128
SUBLANES = 8
ROW_SLAB = D_MODEL // LANES
PACK_SLAB = ROW_SLAB // 2
VMEM_LIMIT = 56 * 1024 * 1024

_IN_Z = 0
_IN_XBC = 512
_IN_CQ = 1280
_IN_CKV = 1536
_IN_FIN = 1664
_IN_SMALL = 1920
_IN_KR4 = 2048
IN_PAD = 2304
_IN_WIDTHS = (512, 768, 256, 128, 256, 128, 256)


def _cparams(sem, vmem=None):
    return pltpu.CompilerParams(dimension_semantics=sem, vmem_limit_bytes=vmem)


def _ln(x):
    mu = jnp.mean(x, axis=-1, keepdims=True)
    xc = x - mu
    var = jnp.mean(xc * xc, axis=-1, keepdims=True)
    return xc * lax.rsqrt(var + EPS)


def _rms(x):
    return x * lax.rsqrt(jnp.mean(x * x, axis=-1, keepdims=True) + EPS)


def _silu(x):
    return x * (1.0 / (1.0 + jnp.exp(-x)))


def _dot(a, b):
    return jnp.dot(a, b, preferred_element_type=F32)


def _dot_nt(a, b):
    return lax.dot_general(a, b, (((1,), (1,)), ((), ())), preferred_element_type=F32)


def _split3(x):
    hi = x.astype(BF16)
    r1 = x - hi.astype(F32)
    mid = r1.astype(BF16)
    lo = (r1 - mid.astype(F32)).astype(BF16)
    return hi, mid, lo


def _eye_bf16(n):
    r = lax.broadcasted_iota(I32, (n, n), 0)
    c = lax.broadcasted_iota(I32, (n, n), 1)
    return jnp.where(r == c, 1.0, 0.0).astype(BF16)


def _const_spec(shape):
    return pl.BlockSpec(shape, lambda *_: (0,) * len(shape))


_MOD_TN = 1536


def _mod_kernel(c_ref, w_ref, b_ref, o_ref):
    a = _silu(c_ref[...]).astype(BF16)
    o_ref[...] = _dot(a, w_ref[...].astype(BF16)) + b_ref[...]


def _modulation(cond, w_mod, b_mod):
    n = 6 * D_MODEL
    return pl.pallas_call(
        _mod_kernel,
        out_shape=jax.ShapeDtypeStruct((DEPTH, SUBLANES, n), F32),
        grid=(DEPTH, n // _MOD_TN),
        in_specs=[
            pl.BlockSpec((SUBLANES, D_MODEL), lambda l, j: (0, 0)),
            pl.BlockSpec((None, D_MODEL, _MOD_TN), lambda l, j: (l, 0, j)),
            pl.BlockSpec((None, 1, _MOD_TN), lambda l, j: (l, 0, j)),
        ],
        out_specs=pl.BlockSpec((None, SUBLANES, _MOD_TN), lambda l, j: (l, 0, j)),
        compiler_params=_cparams(("arbitrary", "arbitrary")),
        name="modulation",
    )(cond, w_mod, b_mod.reshape(DEPTH, 1, n))


_TM = 512


def _read_rows(slab_ref, tm):
    return jnp.concatenate(
        [slab_ref[pl.ds(j, tm, stride=ROW_SLAB), :] for j in range(ROW_SLAB)], axis=-1)


def _post_norm2(x_ref, moe_ref, pmod_ref, g_ref, b_ref):
    moe = _read_rows(moe_ref, _TM)
    return _ln(DN_ALPHA * x_ref[...] + pmod_ref[5:6, :] * moe) * g_ref[...] + b_ref[...]


def _in_proj_kernel(first, *refs):
    if first:
        x_ref, mod_ref, w_ref = refs[:3]
        outs = refs[3:]
        x = x_ref[...]
    else:
        x_ref, moe_ref, pmod_ref, g_ref, b_ref, mod_ref, w_ref, xo_ref = refs[:8]
        outs = refs[8:]
        x = _post_norm2(x_ref, moe_ref, pmod_ref, g_ref, b_ref)
        xo_ref[...] = x
    u = _ln(x) * (1.0 + mod_ref[1:2, :]) + mod_ref[0:1, :]
    p = _dot(u.astype(BF16), w_ref[...])
    off = 0
    for o_ref, w in zip(outs, _IN_WIDTHS):
        o_ref[...] = p[:, off:off + w]
        off += w


def _mod_spec(mod, t, seq):
    per = seq // _TM if mod.shape[0] > 1 else t // _TM
    return pl.BlockSpec((None, 6, D_MODEL), lambda i: (i // per, 0, 0))


def _row_spec(w):
    return pl.BlockSpec((_TM, w), lambda i: (i, 0))


_SLAB_SPEC = pl.BlockSpec((_TM * ROW_SLAB, LANES), lambda i: (i, 0))


def _in_proj(x, prev, mod, w_in_p, seq):
    t = x.shape[0]
    mod_spec = _mod_spec(mod, t, seq)
    outs = [jax.ShapeDtypeStruct((t, w), F32) for w in _IN_WIDTHS]
    out_specs = [_row_spec(w) for w in _IN_WIDTHS]
    w_spec = _const_spec((D_MODEL, IN_PAD))
    if prev is None:
        ins = [x, mod, w_in_p]
        in_specs = [_row_spec(D_MODEL), mod_spec, w_spec]
    else:
        moe_slab, pmod, g, b = prev
        ins = [x, moe_slab, pmod, g, b, mod, w_in_p]
        in_specs = [_row_spec(D_MODEL), _SLAB_SPEC, mod_spec,
                    _const_spec((1, D_MODEL)), _const_spec((1, D_MODEL)), mod_spec, w_spec]
        outs = [jax.ShapeDtypeStruct((t, D_MODEL), F32)] + outs
        out_specs = [_row_spec(D_MODEL)] + out_specs
    res = pl.pallas_call(
        functools.partial(_in_proj_kernel, prev is None),
        out_shape=outs, grid=(t // _TM,), in_specs=in_specs, out_specs=out_specs,
        compiler_params=_cparams(("arbitrary",), VMEM_LIMIT),
        name="in_proj",
    )(*ins)
    if prev is None:
        return (x,) + tuple(res)
    return tuple(res)


def _final_norm_kernel(x_ref, moe_ref, pmod_ref, g_ref, b_ref, o_ref):
    o_ref[...] = _post_norm2(x_ref, moe_ref, pmod_ref, g_ref, b_ref)


def _final_norm(x, moe_slab, pmod, g, b, seq):
    t = x.shape[0]
    return pl.pallas_call(
        _final_norm_kernel,
        out_shape=jax.ShapeDtypeStruct((t, D_MODEL), F32), grid=(t // _TM,),
        in_specs=[_row_spec(D_MODEL), _SLAB_SPEC, _mod_spec(pmod, t, seq),
                  _const_spec((1, D_MODEL)), _const_spec((1, D_MODEL))],
        out_specs=_row_spec(D_MODEL),
        compiler_params=_cparams(("arbitrary",)),
        name="final_norm",
    )(x, moe_slab, pmod, g, b)


def _ssd_kernel(has_h0, write_h, seq, *refs):
    refs = list(refs)
    z_ref, xbc_ref, sm_ref, cw_ref, cb_ref, dtb_ref, alog_ref, dsk_ref, ng_ref = refs[:9]
    pos = 9
    if has_h0:
        h0_ref = refs[pos]
        pos += 1
    y_ref = refs[pos]
    pos += 1
    if write_h:
        hf_ref = refs[pos]
        pos += 1
    pad_ref, xs_ref, bc_ref, ad_ref, st_ref = refs[pos:]
    ck = SSD_CHUNK
    nc = seq // ck
    halo = SUBLANES
    nh2 = 2 * SSD_HEADS

    pad_ref[0:halo, :] = jnp.zeros((halo, SSD_CONV_DIM), F32)
    pad_ref[halo + seq:2 * halo + seq, :] = jnp.zeros((halo, SSD_CONV_DIM), F32)
    pad_ref[halo:halo + seq, :] = xbc_ref[...]
    lane = lax.broadcasted_iota(I32, (ck, LANES), 1)
    a_neg = -jnp.exp(alog_ref[...])
    for c in range(nc):
        rows = slice(c * ck, (c + 1) * ck)
        acc = jnp.zeros((ck, SSD_CONV_DIM), F32) + cb_ref[...]
        for k in range(SSD_CONV):
            s0 = halo + c * ck + k - (SSD_CONV - 1) // 2
            acc = acc + pad_ref[s0:s0 + ck, :] * cw_ref[k:k + 1, :]
        act = _silu(acc)
        xs = act[:, :SSD_DIM]
        xs_ref[rows, :] = xs
        bc_ref[rows, :] = act[:, SSD_DIM:]
        y_ref[rows, :] = xs * dsk_ref[...]
        xr = sm_ref[rows, :] + dtb_ref[...]
        dt = jnp.maximum(xr, 0.0) + jnp.log(1.0 + jnp.exp(-jnp.abs(xr)))
        ad_ref[rows, :] = jnp.where(
            lane < nh2, dt * a_neg, jnp.where(lane < 2 * nh2, pltpu.roll(dt, nh2, axis=1), 0.0))

    if has_h0:
        eye_n = _eye_bf16(SSD_STATE)
        for d in range(2):
            parts = _split3(h0_ref[d].reshape(SSD_DIM, SSD_STATE))
            st_ref[d] = _dot_nt(eye_n, parts[0]) + _dot_nt(eye_n, parts[1]) + _dot_nt(eye_n, parts[2])
    else:
        st_ref[...] = jnp.zeros(st_ref.shape, F32)

    ri = lax.broadcasted_iota(I32, (ck, ck), 0)
    ci = lax.broadcasted_iota(I32, (ck, ck), 1)
    tril = jnp.where(ci <= ri, 1.0, 0.0).astype(BF16)
    triu = jnp.where(ri <= ci, 1.0, 0.0).astype(BF16)
    lo_half = ci < SSD_STATE
    lo64 = lax.broadcasted_iota(I32, (SSD_STATE, LANES), 1) < SSD_STATE
    zeros_half = jnp.zeros((SSD_STATE, LANES), BF16)

    def chunk_dir(d, cs):
        ad = ad_ref[pl.ds(cs, ck), :]
        adt = ad.T
        ad3 = _split3(ad)
        acol = _dot(tril, ad3[0]) + _dot(tril, ad3[1]) + _dot(tril, ad3[2])
        at16 = adt[0:nh2, :]
        dt16 = adt[nh2:2 * nh2, :]
        at3 = _split3(at16)
        arow = _dot(at3[0], triu) + _dot(at3[1], triu) + _dot(at3[2], triu)
        bc = bc_ref[pl.ds(cs, ck), :]
        bm = bc[:, :LANES]
        cm = bc[:, LANES:]
        bt = bm.T
        bmb = bm.astype(BF16)
        cg = [jnp.where(lo_half, cm, 0.0), jnp.where(lo_half, 0.0, cm)]
        gm = [_dot_nt(cg[g].astype(BF16), bmb) for g in range(2)]
        alast16 = jnp.broadcast_to(arow[:, ck - 1:ck], (nh2, ck))
        if d == 1:
            acol = acol - ad
            arow = arow - at16
        for q in range(SSD_HEADS // 2):
            g = q // 2
            lhs_rows = []
            decl = []
            for hh in range(2):
                hi = d * SSD_HEADS + 2 * q + hh
                acol_b = jnp.broadcast_to(acol[:, hi:hi + 1], (ck, ck))
                arow_h = arow[hi:hi + 1, :]
                alast = alast16[hi:hi + 1, :]
                dtrow = dt16[hi:hi + 1, :]
                if d == 0:
                    lmat = jnp.where(ri >= ci, jnp.exp(acol_b - arow_h), 0.0)
                    wrow = jnp.exp(alast - arow_h)
                    ecol = jnp.exp(acol_b)
                else:
                    lmat = jnp.where(ci >= ri, jnp.exp(arow_h - acol_b), 0.0)
                    wrow = jnp.exp(arow_h)
                    ecol = jnp.exp(alast - acol_b)
                mm = (gm[g] * lmat * dtrow).astype(BF16)
                cs_h = (cg[g] * ecol).astype(BF16)
                bw = (bt[g * SSD_STATE:(g + 1) * SSD_STATE, :] * (wrow * dtrow)).astype(BF16)
                lhs_rows.append(jnp.concatenate([mm, cs_h], axis=1))
                lhs_rows.append(jnp.concatenate([bw, zeros_half], axis=1))
                decl.append(jnp.exp(alast))
            lhs = jnp.concatenate(lhs_rows, axis=0)
            cols = slice(q * LANES, (q + 1) * LANES)
            st_pair = st_ref[d, :, cols]
            stb = st_pair.astype(BF16)
            w_st = jnp.concatenate([stb, zeros_half] if g == 0 else [zeros_half, stb], axis=0)
            xs_pair = xs_ref[pl.ds(cs, ck), cols].astype(BF16)
            out = _dot(lhs, jnp.concatenate([xs_pair, w_st], axis=0))
            r2 = ck + SSD_STATE
            y_pair = jnp.where(lo_half, out[0:ck], out[r2:r2 + ck])
            st_new = jnp.where(lo64, out[ck:r2], out[r2 + ck:2 * r2])
            dec = jnp.where(lo64, decl[0], decl[1])
            st_ref[d, :, cols] = st_pair * dec + st_new
            y_ref[pl.ds(cs, ck), cols] += y_pair

    def body(c, carry):
        chunk_dir(0, pl.multiple_of(c * ck, ck))
        chunk_dir(1, pl.multiple_of((nc - 1 - c) * ck, ck))
        return carry

    lax.fori_loop(0, nc, body, 0)

    for c in range(nc):
        rows = slice(c * ck, (c + 1) * ck)
        y = y_ref[rows, :] * _silu(z_ref[rows, :])
        y_ref[rows, :] = _rms(y) * ng_ref[...]

    if write_h:
        eye_d = _eye_bf16(SSD_DIM)
        for d in range(2):
            parts = _split3(st_ref[d])
            hp_n = _dot_nt(eye_d, parts[0]) + _dot_nt(eye_d, parts[1]) + _dot_nt(eye_d, parts[2])
            hf_ref[d] = hp_n.reshape(SSD_HEADS, SSD_HEADDIM, SSD_STATE)


def _ssd(z, xbc, small, lp, nb, seq, h0, layer):
    has_h0 = h0 is not None
    write_h = not has_h0
    seq_spec = lambda w: pl.BlockSpec((seq, w), lambda b: (b, 0))
    ins = [z, xbc, small, lp["conv_w"], lp["conv_b"], lp["dt_bias"], lp["a_log"], lp["d_skip"],
           lp["ssd_norm_g"]]
    in_specs = [seq_spec(SSD_DIM), seq_spec(SSD_CONV_DIM), seq_spec(LANES),
                _const_spec((SSD_CONV, SSD_CONV_DIM)), _const_spec((1, SSD_CONV_DIM)),
                _const_spec((1, LANES)), _const_spec((1, LANES)), _const_spec((1, SSD_DIM)),
                _const_spec((1, SSD_DIM))]
    hshape = (2, SSD_HEADS, SSD_HEADDIM, SSD_STATE)
    if has_h0:
        ins.append(h0)
        in_specs.append(pl.BlockSpec((None, None) + hshape, lambda b: (b, layer, 0, 0, 0, 0)))
    outs = [jax.ShapeDtypeStruct((nb * seq, SSD_DIM), F32)]
    out_specs = [seq_spec(SSD_DIM)]
    if write_h:
        outs.append(jax.ShapeDtypeStruct((nb,) + hshape, F32))
        out_specs.append(pl.BlockSpec((None,) + hshape, lambda b: (b, 0, 0, 0, 0)))
    res = pl.pallas_call(
        functools.partial(_ssd_kernel, has_h0, write_h, seq),
        out_shape=outs, grid=(nb,), in_specs=in_specs, out_specs=out_specs,
        scratch_shapes=[pltpu.VMEM((seq + 2 * SUBLANES, SSD_CONV_DIM), F32),
                        pltpu.VMEM((seq, SSD_DIM), F32),
                        pltpu.VMEM((seq, 2 * LANES), F32),
                        pltpu.VMEM((seq, LANES), F32),
                        pltpu.VMEM((2, SSD_STATE, SSD_DIM), F32)],
        compiler_params=_cparams(("arbitrary",), VMEM_LIMIT),
        name="ssd",
    )(*ins)
    return res if write_h else (res[0], None)


def _mla_kernel(has_ctx, seq, *refs):
    refs = list(refs)
    cq_ref, ckv_ref, kr_ref, qg_ref, wq_ref, kvg_ref, wkv_ref = refs[:7]
    pos = 7
    if has_ctx:
        cckv_ref, ckr_ref, cos_ref, sin_ref = refs[pos:pos + 4]
        pos += 4
    y_ref = refs[pos]
    pos += 1
    if not has_ctx:
        ckvo_ref, kro_ref = refs[pos:pos + 2]

    qn = (_rms(cq_ref[...]) * qg_ref[...]).astype(BF16)
    qall = _dot(qn, wq_ref[...])
    q_nope = qall[:, :MLA_DIM]
    q_rope = qall[:, MLA_DIM:MLA_DIM + LANES]
    ckv_n = _rms(ckv_ref[...]) * kvg_ref[...]
    kr4 = kr_ref[:, :LANES]
    if has_ctx:
        q_rope = q_rope * cos_ref[...] + qall[:, MLA_DIM + LANES:] * sin_ref[...]
        kr4 = kr4 * cos_ref[...] + kr_ref[:, LANES:] * sin_ref[...]
        ckv_all = jnp.concatenate([cckv_ref[...], ckv_n], axis=0)
        kr_all = jnp.concatenate([ckr_ref[...], kr4], axis=0)
    else:
        ckvo_ref[...] = ckv_n
        kro_ref[...] = kr4[:, :ROPE_DIM]
        ckv_all = ckv_n
        kr_all = kr4
    kv = _dot(ckv_all.astype(BF16), wkv_ref[...])
    krb = kr_all.astype(BF16)
    lane = lax.broadcasted_iota(I32, (Q_TILE, LANES), 1)
    lo_half = lane < QK_NOPE
    for qt in range(seq // Q_TILE):
        rows = slice(qt * Q_TILE, (qt + 1) * Q_TILE)
        qr = q_rope[rows] * MLA_SCALE
        for pair in range(MLA_HEADS // 2):
            cols = slice(pair * LANES, (pair + 1) * LANES)
            kcat = jnp.concatenate([kv[:, cols].astype(BF16), krb], axis=1)
            vb = kv[:, MLA_DIM + pair * LANES:MLA_DIM + (pair + 1) * LANES].astype(BF16)
            qn_pair = q_nope[rows, cols] * MLA_SCALE
            outs = []
            for hh in range(2):
                h = 2 * pair + hh
                qa = jnp.where(lo_half if hh == 0 else jnp.logical_not(lo_half), qn_pair, 0.0)
                qb = jnp.where((lane >= h * ROPE_DIM) & (lane < (h + 1) * ROPE_DIM), qr, 0.0)
                s = _dot_nt(jnp.concatenate([qa, qb], axis=1).astype(BF16), kcat)
                p = jnp.exp(s - jnp.max(s, axis=-1, keepdims=True))
                o = _dot(p.astype(BF16), vb)
                outs.append(o / jnp.sum(p, axis=-1, keepdims=True))
            y_ref[rows, cols] = jnp.where(lo_half, outs[0], outs[1])


def _mla(cq, ckv, kr, lp, nb, seq, ctx, layer):
    has_ctx = ctx is not None
    seq_spec = lambda w: pl.BlockSpec((seq, w), lambda b: (b, 0))
    ins = [cq, ckv, kr, lp["q_norm_g"], lp["wq"], lp["kv_norm_g"], lp["wkv"]]
    in_specs = [seq_spec(Q_LORA), seq_spec(KV_LORA), seq_spec(2 * LANES),
                _const_spec((1, Q_LORA)), _const_spec((Q_LORA, 2 * MLA_DIM)),
                _const_spec((1, KV_LORA)), _const_spec((KV_LORA, 2 * MLA_DIM))]
    outs = [jax.ShapeDtypeStruct((nb * seq, MLA_DIM), F32)]
    out_specs = [seq_spec(MLA_DIM)]
    if has_ctx:
        cache_ckv, cache_kr4, cos4, sin4 = ctx
        past = cache_ckv.shape[2]
        ins += [cache_ckv, cache_kr4, cos4, sin4]
        in_specs += [pl.BlockSpec((None, None, past, KV_LORA), lambda b: (b, layer, 0, 0)),
                     pl.BlockSpec((None, None, past, LANES), lambda b: (b, layer, 0, 0)),
                     _const_spec((seq, LANES)), _const_spec((seq, LANES))]
    else:
        outs += [jax.ShapeDtypeStruct((nb, seq, KV_LORA), F32),
                 jax.ShapeDtypeStruct((nb, seq, ROPE_DIM), F32)]
        out_specs += [pl.BlockSpec((None, seq, KV_LORA), lambda b: (b, 0, 0)),
                      pl.BlockSpec((None, seq, ROPE_DIM), lambda b: (b, 0, 0))]
    return pl.pallas_call(
        functools.partial(_mla_kernel, has_ctx, seq),
        out_shape=outs, grid=(nb,), in_specs=in_specs, out_specs=out_specs,
        compiler_params=_cparams(("arbitrary",), VMEM_LIMIT),
        name="mla",
    )(*ins)


def _fnet_kernel(g_ref, cbd_ref, sbd_ref, cs_ref, o_ref):
    gb = g_ref[...].astype(BF16)
    gc = _dot(gb, cbd_ref[...].astype(BF16))
    gs = _dot(gb, sbd_ref[...].astype(BF16))
    r = jnp.concatenate([gc, gs], axis=0).astype(BF16)
    o_ref[...] = _dot(cs_ref[...].astype(BF16), r)


def _dft_constants(seq):
    k = np.arange(FNET_GW)
    ang = 2.0 * np.pi * np.outer(k, k) / FNET_GW
    eye = np.eye(FNET_DIM // FNET_GW)
    cbd = np.kron(eye, np.cos(ang)) / np.sqrt(FNET_GW)
    sbd = np.kron(eye, np.sin(ang)) / np.sqrt(FNET_GW)
    s = np.arange(seq)
    angs = 2.0 * np.pi * np.outer(s, s) / seq
    cs = np.concatenate([np.cos(angs), -np.sin(angs)], axis=1) / np.sqrt(seq)
    return (jnp.asarray(cbd, F32), jnp.asarray(sbd, F32), jnp.asarray(cs, F32))


def _fnet(fin, nb, seq):
    cbd, sbd, cs = _dft_constants(seq)
    return pl.pallas_call(
        _fnet_kernel,
        out_shape=jax.ShapeDtypeStruct((nb * seq, FNET_DIM), F32), grid=(nb,),
        in_specs=[pl.BlockSpec((seq, FNET_DIM), lambda b: (b, 0)),
                  _const_spec((FNET_DIM, FNET_DIM)), _const_spec((FNET_DIM, FNET_DIM)),
                  _const_spec((seq, 2 * seq))],
        out_specs=pl.BlockSpec((seq, FNET_DIM), lambda b: (b, 0)),
        compiler_params=_cparams(("arbitrary",), VMEM_LIMIT),
        name="fnet",
    )(fin, cbd, sbd, cs)


def _out_proj_kernel(x_ref, ys_ref, ym_ref, yf_ref, mod_ref, w_ref, g_ref, b_ref, wr_ref,
                     x1_ref, u2p_ref, lg_ref):
    mix = (_dot(ys_ref[...].astype(BF16), w_ref[0:SSD_DIM, :])
           + _dot(ym_ref[...].astype(BF16), w_ref[SSD_DIM:SSD_DIM + MLA_DIM, :])
           + _dot(yf_ref[...].astype(BF16), w_ref[SSD_DIM + MLA_DIM:, :]))
    x1 = _ln(DN_ALPHA * x_ref[...] + mod_ref[2:3, :] * mix) * g_ref[...] + b_ref[...]
    x1_ref[...] = x1
    u2 = _ln(x1) * (1.0 + mod_ref[4:5, :]) + mod_ref[3:4, :]
    lg_ref[...] = _dot_nt(wr_ref[...], u2.astype(BF16))
    half = D_MODEL // 2
    words = pltpu.pack_elementwise([u2[:, :half], u2[:, half:]], packed_dtype=BF16)
    for j in range(PACK_SLAB):
        u2p_ref[pl.ds(j, _TM, stride=PACK_SLAB), :] = words[:, j * LANES:(j + 1) * LANES]


def _out_proj(x, yssd, ymla, yfft, mod, lp, seq):
    t = x.shape[0]
    return pl.pallas_call(
        _out_proj_kernel,
        out_shape=[jax.ShapeDtypeStruct((t, D_MODEL), F32),
                   jax.ShapeDtypeStruct((t * PACK_SLAB, LANES), U32),
                   jax.ShapeDtypeStruct((N_EXPERTS, t), F32)],
        grid=(t // _TM,),
        in_specs=[_row_spec(D_MODEL), _row_spec(SSD_DIM), _row_spec(MLA_DIM), _row_spec(FNET_DIM),
                  _mod_spec(mod, t, seq), _const_spec((D_MODEL, D_MODEL)),
                  _const_spec((1, D_MODEL)), _const_spec((1, D_MODEL)),
                  _const_spec((N_EXPERTS, D_MODEL))],
        out_specs=[_row_spec(D_MODEL),
                   pl.BlockSpec((_TM * PACK_SLAB, LANES), lambda i: (i, 0)),
                   pl.BlockSpec((N_EXPERTS, _TM), lambda i: (0, i))],
        compiler_params=_cparams(("arbitrary",), VMEM_LIMIT),
        name="out_proj",
    )(x, yssd, ymla, yfft, mod, lp["w_out"], lp["ln1_g"], lp["ln1_b"], lp["wrt"])


def _cumsum_lanes(x, lane):
    n = x.shape[1]
    s = 1
    while s < n:
        x = x + jnp.where(lane >= s, pltpu.roll(x, s, axis=1), 0.0)
        s *= 2
    return x


def _route_kernel(t, cap, lg_ref, idx_ref, aff_ref, p_ref):
    lg = lg_ref[...]
    e = jnp.exp(lg - jnp.max(lg, axis=0, keepdims=True))
    aff = e / jnp.sum(e, axis=0, keepdims=True)
    aff_ref[...] = aff
    capf = float(cap)

    def search(i, lo):
        cand = lo | jnp.left_shift(jnp.int32(1), 30 - i)
        cnt = jnp.sum(jnp.where(aff >= pltpu.bitcast(cand, F32), 1.0, 0.0), axis=1, keepdims=True)
        return jnp.where(cnt >= capf, cand, lo)

    thr_bits = lax.fori_loop(0, 31, search, jnp.zeros((N_EXPERTS, 1), I32))
    thr = pltpu.bitcast(thr_bits, F32)
    lane = lax.broadcasted_iota(I32, (N_EXPERTS, t), 1)
    gt = jnp.where(aff > thr, 1.0, 0.0)
    eq = jnp.where(aff == thr, 1.0, 0.0)
    need = capf - jnp.sum(gt, axis=1, keepdims=True)
    eq_rank = _cumsum_lanes(eq, lane) - eq
    sel = gt + eq * jnp.where(eq_rank < need, 1.0, 0.0)
    rank = _cumsum_lanes(sel, lane)
    for ex in range(N_EXPERTS):
        p_ref[ex] = rank[ex:ex + 1, :]

    ri = lax.broadcasted_iota(I32, (LANES, LANES), 0)
    ci = lax.broadcasted_iota(I32, (LANES, LANES), 1)
    diag = ri == ci
    group = 8

    def per_expert(ex, carry):
        for rb in range(cap // LANES):
            r_col = (lax.broadcasted_iota(I32, (LANES, 1), 0) + rb * LANES).astype(F32)

            def accumulate(kg, acc):
                base = pl.multiple_of(kg * (group * LANES), group * LANES)
                pe = p_ref[ex, :, pl.ds(base, group * LANES)]
                for k in range(group):
                    acc = acc + jnp.where(pe[:, k * LANES:(k + 1) * LANES] <= r_col, 1.0, 0.0)
                return acc

            acc = lax.fori_loop(0, t // (group * LANES), accumulate, jnp.zeros((LANES, LANES), F32))
            cnt = jnp.sum(acc, axis=1, keepdims=True)
            row = jnp.sum(jnp.where(diag, cnt, 0.0), axis=0, keepdims=True)
            idx_ref[ex, :, rb * LANES:(rb + 1) * LANES] = row.astype(I32)
        return carry

    lax.fori_loop(0, N_EXPERTS, per_expert, 0)


def _route(logits, cap):
    t = logits.shape[1]
    idx, aff = pl.pallas_call(
        functools.partial(_route_kernel, t, cap),
        out_shape=[jax.ShapeDtypeStruct((N_EXPERTS, 1, cap), I32),
                   jax.ShapeDtypeStruct((N_EXPERTS, t), F32)],
        scratch_shapes=[pltpu.VMEM((N_EXPERTS, 1, t), F32)],
        name="route",
    )(logits)
    return idx.reshape(N_EXPERTS, cap), aff


_UP_TF = 512
_UNROLL = 16


def _moe_up_kernel(caps, ua_ref, ub_ref, ia_ref, ib_ref, wg_ref, wu_ref, h_ref, tile_ref, xe_ref):
    ex = pl.program_id(0)
    m = sum(caps)

    @pl.when(pl.program_id(1) == 0)
    def _gather():
        row0 = 0
        for src_ref, idx_ref, cap in ((ua_ref, ia_ref, caps[0]), (ub_ref, ib_ref, caps[1])):
            def body(i, carry, src_ref=src_ref, idx_ref=idx_ref, cap=cap, row0=row0):
                for u in range(_UNROLL):
                    r = i * _UNROLL + u
                    tok = idx_ref[ex * cap + r]
                    src = pl.multiple_of(tok * PACK_SLAB, PACK_SLAB)
                    dst = pl.multiple_of((row0 + r) * PACK_SLAB, PACK_SLAB)
                    tile_ref[pl.ds(dst, PACK_SLAB), :] = src_ref[pl.ds(src, PACK_SLAB), :]
                return carry

            lax.fori_loop(0, cap // _UNROLL, body, 0)
            row0 += cap
        half = D_MODEL // 2
        for j in range(PACK_SLAB):
            w = tile_ref[pl.ds(j, m, stride=PACK_SLAB), :]
            for part in range(2):
                v = pltpu.unpack_elementwise(w, index=part, packed_dtype=BF16, unpacked_dtype=F32)
                lo = part * half + j * LANES
                xe_ref[:, lo:lo + LANES] = v.astype(BF16)

    x = xe_ref[...]
    g = _dot(x, wg_ref[...].astype(BF16))
    u = _dot(x, wu_ref[...].astype(BF16))
    h_ref[...] = (_silu(g) * u).astype(BF16)


def _moe_up(u2p_a, u2p_b, idx_a, idx_b, w_gate, w_up, layer):
    caps = (idx_a.shape[1], idx_b.shape[1])
    m = sum(caps)
    smem = pl.BlockSpec(memory_space=pltpu.SMEM)
    w_spec = pl.BlockSpec((None, None, D_MODEL, _UP_TF), lambda e, f: (layer, e, 0, f))
    return pl.pallas_call(
        functools.partial(_moe_up_kernel, caps),
        out_shape=jax.ShapeDtypeStruct((N_EXPERTS, m, EXPERT_FF), BF16),
        grid=(N_EXPERTS, EXPERT_FF // _UP_TF),
        in_specs=[pl.BlockSpec(u2p_a.shape, lambda e, f: (0, 0), pipeline_mode=pl.Buffered(1)),
                  pl.BlockSpec(u2p_b.shape, lambda e, f: (0, 0), pipeline_mode=pl.Buffered(1)),
                  smem, smem, w_spec, w_spec],
        out_specs=pl.BlockSpec((None, m, _UP_TF), lambda e, f: (e, 0, f)),
        scratch_shapes=[pltpu.VMEM((m * PACK_SLAB, LANES), U32), pltpu.VMEM((m, D_MODEL), BF16)],
        compiler_params=_cparams(("arbitrary", "arbitrary"), VMEM_LIMIT),
        name="moe_up",
    )(u2p_a, u2p_b, idx_a.reshape(-1), idx_b.reshape(-1), w_gate, w_up)


_DOWN_TN = 512
_RMW = 16


def _moe_down_kernel(cap, h_ref, wd_ref, idx_ref, aff_ref, out_ref, acc_ref, slab_ref):
    ex = pl.program_id(0)
    nn = pl.program_id(1)
    n_half = _DOWN_TN // LANES

    @pl.when((ex == 0) & (nn == 0))
    def _zero():
        acc_ref[...] = jnp.zeros(acc_ref.shape, F32)

    ye = _dot(h_ref[...], wd_ref[...].astype(BF16))
    for n in range(D_MODEL // _DOWN_TN):
        @pl.when(nn == n)
        def _to_slab(n=n):
            for j in range(n_half):
                slab_ref[pl.ds(n * n_half + j, cap, stride=ROW_SLAB), :] = ye[:, j * LANES:(j + 1) * LANES]

    @pl.when(nn == pl.num_programs(1) - 1)
    def _scatter():
        def body(i, carry):
            dsts, vals = [], []
            for u in range(_RMW):
                r = i * _RMW + u
                tok = idx_ref[ex * cap + r]
                dst = pl.multiple_of(tok * ROW_SLAB, ROW_SLAB)
                src = pl.multiple_of(r * ROW_SLAB, ROW_SLAB)
                vals.append(acc_ref[pl.ds(dst, ROW_SLAB), :]
                            + aff_ref[tok] * slab_ref[pl.ds(src, ROW_SLAB), :])
                dsts.append(dst)
            for dst, val in zip(dsts, vals):
                acc_ref[pl.ds(dst, ROW_SLAB), :] = val
            return carry

        lax.fori_loop(0, cap // _RMW, body, 0)

    @pl.when((ex == N_EXPERTS - 1) & (nn == pl.num_programs(1) - 1))
    def _write():
        pltpu.sync_copy(acc_ref, out_ref)


def _moe_down(h, w_down, idx, aff, row_block, layer):
    cap = idx.shape[1]
    t = aff.shape[1]
    return pl.pallas_call(
        functools.partial(_moe_down_kernel, cap),
        out_shape=jax.ShapeDtypeStruct((t * ROW_SLAB, LANES), F32),
        grid=(N_EXPERTS, D_MODEL // _DOWN_TN),
        in_specs=[pl.BlockSpec((None, cap, EXPERT_FF), lambda e, n: (e, row_block, 0)),
                  pl.BlockSpec((None, None, EXPERT_FF, _DOWN_TN), lambda e, n: (layer, e, 0, n)),
                  pl.BlockSpec(memory_space=pltpu.SMEM),
                  pl.BlockSpec((None, None, t), lambda e, n: (e, 0, 0), memory_space=pltpu.SMEM)],
        out_specs=pl.BlockSpec(memory_space=pl.ANY),
        scratch_shapes=[pltpu.VMEM((t * ROW_SLAB, LANES), F32),
                        pltpu.VMEM((cap * ROW_SLAB, LANES), F32)],
        compiler_params=_cparams(("arbitrary", "arbitrary"), VMEM_LIMIT),
        name="moe_down",
    )(h, w_down, idx.reshape(-1), aff[:, None, :])


def _in_proj_columns():
    perm = np.zeros(IN_PAD, np.int32)
    valid = np.zeros(IN_PAD, bool)
    o_z, o_xbc, o_dt, o_cq, o_ckv, o_kr, o_fin = 0, 512, 1280, 1296, 1552, 1680, 1712

    def put(dst, src, n):
        perm[dst:dst + n] = np.arange(src, src + n)
        valid[dst:dst + n] = True

    put(_IN_Z, o_z, 512)
    put(_IN_XBC, o_xbc, 768)
    put(_IN_CQ, o_cq, 256)
    put(_IN_CKV, o_ckv, 128)
    put(_IN_FIN, o_fin, 256)
    put(_IN_SMALL, o_dt, 2 * SSD_HEADS)
    r = np.arange(LANES) % ROPE_DIM
    perm[_IN_KR4:_IN_KR4 + LANES] = o_kr + r
    perm[_IN_KR4 + LANES:IN_PAD] = o_kr + (r ^ 1)
    valid[_IN_KR4:IN_PAD] = True
    return perm, valid


def _uq_columns():
    hd = QK_NOPE + ROPE_DIM
    nope = np.concatenate([h * hd + np.arange(QK_NOPE) for h in range(MLA_HEADS)])
    rope = np.concatenate([h * hd + QK_NOPE + np.arange(ROPE_DIM) for h in range(MLA_HEADS)])
    swap = np.concatenate([h * hd + QK_NOPE + (np.arange(ROPE_DIM) ^ 1) for h in range(MLA_HEADS)])
    return np.concatenate([nope, rope, swap])


def _ukv_columns():
    hd = QK_NOPE + V_DIM
    k = np.concatenate([h * hd + np.arange(QK_NOPE) for h in range(MLA_HEADS)])
    v = np.concatenate([h * hd + QK_NOPE + np.arange(V_DIM) for h in range(MLA_HEADS)])
    return np.concatenate([k, v])


def _rope_tables(seq):
    rows = seq // GRID_W
    r, col = jnp.meshgrid(jnp.arange(rows, dtype=F32), jnp.arange(GRID_W, dtype=F32), indexing="ij")
    pairs = ROPE_DIM // 4
    inv = ROPE_BASE ** (-jnp.arange(pairs, dtype=F32) / pairs)
    ang = jnp.concatenate([r.reshape(-1, 1) * inv, col.reshape(-1, 1) * inv], axis=-1)
    cos = jnp.repeat(jnp.cos(ang), 2, axis=1)
    sin = jnp.repeat(jnp.sin(ang), 2, axis=1) * jnp.tile(jnp.array([-1.0, 1.0], F32), ROPE_DIM // 2)
    return jnp.tile(cos, (1, MLA_HEADS)), jnp.tile(sin, (1, MLA_HEADS))


def _pad_lanes(v):
    return jnp.pad(v, ((0, 0), (0, LANES - v.shape[1])))[:, None, :]


def kernel(x_prompt, x_sample, cache_ckv, cache_krope, state_ssm, c, c_ctx,
           w_in, conv_w, conv_b, dt_bias, a_log, d_skip, ssd_norm_g,
           q_norm_g, w_uq, kv_norm_g, w_ukv, w_out, w_mod, b_mod,
           ln1_g, ln1_b, ln2_g, ln2_b, w_router, w_gate, w_up, w_down):
    nb_c, seq_c, _ = x_prompt.shape
    nb_l, seq_l, _ = x_sample.shape
    t_c, t_l = nb_c * seq_c, nb_l * seq_l
    caps = (CAPACITY_FACTOR * t_c // N_EXPERTS, CAPACITY_FACTOR * t_l // N_EXPERTS)

    perm, valid = _in_proj_columns()
    w_in_p = jnp.where(valid, jnp.take(w_in, perm, axis=2), 0.0).astype(BF16)
    wq = jnp.take(w_uq, _uq_columns(), axis=2).astype(BF16)
    wkv = jnp.take(w_ukv, _ukv_columns(), axis=2).astype(BF16)
    w_out_b = w_out.astype(BF16)
    wrt = jnp.swapaxes(w_router, 1, 2).astype(BF16)
    dtb = _pad_lanes(dt_bias.reshape(DEPTH, -1))
    alog = _pad_lanes(a_log.reshape(DEPTH, -1))
    dsk = jnp.repeat(d_skip, SSD_HEADDIM, axis=1)[:, None, :]
    cache_kr4 = jnp.tile(cache_krope, (1, 1, 1, MLA_HEADS))
    cos4, sin4 = _rope_tables(seq_l)

    cond = jnp.zeros((SUBLANES, D_MODEL), F32).at[0].set(c_ctx).at[1:1 + nb_l].set(c)
    mod_all = _modulation(cond, w_mod, b_mod).reshape(DEPTH, SUBLANES, 6, D_MODEL)

    streams = [
        dict(x=x_prompt.reshape(t_c, D_MODEL), nb=nb_c, seq=seq_c, cap=caps[0]),
        dict(x=x_sample.reshape(t_l, D_MODEL), nb=nb_l, seq=seq_l, cap=caps[1]),
    ]
    prev = [None, None]
    ckvs, krs, ssms = [], [], []
    for l in range(DEPTH):
        lp = dict(conv_w=conv_w[l], conv_b=conv_b[l][None], dt_bias=dtb[l], a_log=alog[l],
                  d_skip=dsk[l], ssd_norm_g=ssd_norm_g[l][None],
                  q_norm_g=q_norm_g[l][None], wq=wq[l], kv_norm_g=kv_norm_g[l][None], wkv=wkv[l],
                  w_out=w_out_b[l], ln1_g=ln1_g[l][None], ln1_b=ln1_b[l][None], wrt=wrt[l])
        mods = [mod_all[l, 0:1], mod_all[l, 1:1 + nb_l]]
        routed = []
        for si, st in enumerate(streams):
            nb, seq = st["nb"], st["seq"]
            x, z, xbc, cq, ckv, fin, small, kr = _in_proj(st["x"], prev[si], mods[si], w_in_p[l], seq)
            if si == 0:
                yssd, ssm = _ssd(z, xbc, small, lp, nb, seq, None, l)
                ymla, ckv_n, kr_o = _mla(cq, ckv, kr, lp, nb, seq, None, l)
                ckvs.append(ckv_n)
                krs.append(kr_o)
                ssms.append(ssm)
            else:
                yssd, _ = _ssd(z, xbc, small, lp, nb, seq, state_ssm, l)
                (ymla,) = _mla(cq, ckv, kr, lp, nb, seq, (cache_ckv, cache_kr4, cos4, sin4), l)
            yfft = _fnet(fin, nb, seq)
            x1, u2p, logits = _out_proj(x, yssd, ymla, yfft, mods[si], lp, seq)
            idx, aff = _route(logits, st["cap"])
            st["x"] = x1
            routed.append((u2p, idx, aff))
        h = _moe_up(routed[0][0], routed[1][0], routed[0][1], routed[1][1], w_gate, w_up, l)
        for si in range(2):
            moe = _moe_down(h, w_down, routed[si][1], routed[si][2], 0 if si == 0 else 2, l)
            prev[si] = (moe, mods[si], ln2_g[l][None], ln2_b[l][None])
    y_p = _final_norm(streams[0]["x"], *prev[0], seq_c).reshape(x_prompt.shape)
    y_s = _final_norm(streams[1]["x"], *prev[1], seq_l).reshape(x_sample.shape)
    return (y_p, y_s, jnp.stack(ckvs, axis=1), jnp.stack(krs, axis=1), jnp.stack(ssms, axis=1))
```

```python
import functools

import jax
import jax.numpy as jnp
import numpy as np
from jax import lax
from jax.experimental import pallas as pl
from jax.experimental.pallas import tpu as pltpu

F32 = jnp.float32
BF16 = jnp.bfloat16
I32 = jnp.int32
U32 = jnp.uint32

D_MODEL = 1024
DEPTH = 4
GRID_W = 64
SSD_HEADS = 8
SSD_HEADDIM = 64
SSD_DIM = SSD_HEADS * SSD_HEADDIM
SSD_STATE = 64
SSD_CONV = 5
SSD_CHUNK = 128
SSD_CONV_DIM = SSD_DIM + 4 * SSD_STATE
MLA_HEADS = 4
Q_LORA = 256
KV_LORA = 128
QK_NOPE = 64
ROPE_DIM = 32
V_DIM = 64
MLA_DIM = MLA_HEADS * V_DIM
MLA_SCALE = (QK_NOPE + ROPE_DIM) ** -0.5
ROPE_BASE = 10000.0
Q_TILE = 256
FNET_GW = 64
FNET_DIM = 256
N_EXPERTS = 16
EXPERT_FF = 1024
CAPACITY_FACTOR = 2
DN_ALPHA = (2 * DEPTH) ** 0.25
EPS = 1e-5

LANES = 128
SUBLANES = 8
ROW_SLAB = D_MODEL // LANES
PACK_SLAB = ROW_SLAB // 2
VMEM_LIMIT = 56 * 1024 * 1024

_IN_Z = 0
_IN_XBC = 512
_IN_CQ = 1280
_IN_CKV = 1536
_IN_FIN = 1664
_IN_SMALL = 1920
_IN_KR4 = 2048
IN_PAD = 2304
_IN_WIDTHS = (512, 768, 256, 128, 256, 128, 256)


def _cparams(sem, vmem=None):
    return pltpu.CompilerParams(dimension_semantics=sem, vmem_limit_bytes=vmem)


def _ln(x):
    mu = jnp.mean(x, axis=-1, keepdims=True)
    xc = x - mu
    var = jnp.mean(xc * xc, axis=-1, keepdims=True)
    return xc * lax.rsqrt(var + EPS)


def _rms(x):
    return x * lax.rsqrt(jnp.mean(x * x, axis=-1, keepdims=True) + EPS)


def _silu(x):
    return x * (1.0 / (1.0 + jnp.exp(-x)))


def _dot(a, b):
    return jnp.dot(a, b, preferred_element_type=F32)


def _dot_nt(a, b):
    return lax.dot_general(a, b, (((1,), (1,)), ((), ())), preferred_element_type=F32)


def _split3(x):
    hi = x.astype(BF16)
    r1 = x - hi.astype(F32)
    mid = r1.astype(BF16)
    lo = (r1 - mid.astype(F32)).astype(BF16)
    return hi, mid, lo


def _const_spec(shape):
    return pl.BlockSpec(shape, lambda *_: (0,) * len(shape))


_MOD_TN = 1536


def _mod_kernel(c_ref, w_ref, b_ref, o_ref):
    a = _silu(c_ref[...]).astype(BF16)
    o_ref[...] = _dot(a, w_ref[...].astype(BF16)) + b_ref[...]


def _modulation(cond, w_mod, b_mod):
    n = 6 * D_MODEL
    return pl.pallas_call(
        _mod_kernel,
        out_shape=jax.ShapeDtypeStruct((DEPTH, SUBLANES, n), F32),
        grid=(DEPTH, n // _MOD_TN),
        in_specs=[
            pl.BlockSpec((SUBLANES, D_MODEL), lambda l, j: (0, 0)),
            pl.BlockSpec((None, D_MODEL, _MOD_TN), lambda l, j: (l, 0, j)),
            pl.BlockSpec((None, 1, _MOD_TN), lambda l, j: (l, 0, j)),
        ],
        out_specs=pl.BlockSpec((None, SUBLANES, _MOD_TN), lambda l, j: (l, 0, j)),
        compiler_params=_cparams(("arbitrary", "arbitrary")),
        name="modulation",
    )(cond, w_mod, b_mod.reshape(DEPTH, 1, n))


_TM = 512


def _read_rows(slab_ref, tm):
    return jnp.concatenate(
        [slab_ref[pl.ds(j, tm, stride=ROW_SLAB), :] for j in range(ROW_SLAB)], axis=-1)


def _post_norm2(x_ref, moe_ref, pmod_ref, g_ref, b_ref):
    moe = _read_rows(moe_ref, _TM)
    return _ln(DN_ALPHA * x_ref[...] + pmod_ref[5:6, :] * moe) * g_ref[...] + b_ref[...]


def _in_proj_kernel(first, *refs):
    if first:
        x_ref, mod_ref, w_ref = refs[:3]
        outs = refs[3:]
        x = x_ref[...]
    else:
        x_ref, moe_ref, pmod_ref, g_ref, b_ref, mod_ref, w_ref, xo_ref = refs[:8]
        outs = refs[8:]
        x = _post_norm2(x_ref, moe_ref, pmod_ref, g_ref, b_ref)
        xo_ref[...] = x
    u = _ln(x) * (1.0 + mod_ref[1:2, :]) + mod_ref[0:1, :]
    p = _dot(u.astype(BF16), w_ref[...])
    off = 0
    for o_ref, w in zip(outs, _IN_WIDTHS):
        o_ref[...] = p[:, off:off + w]
        off += w


def _mod_spec(mod, t, seq):
    per = seq // _TM if mod.shape[0] > 1 else t // _TM
    return pl.BlockSpec((None, 6, D_MODEL), lambda i: (i // per, 0, 0))


def _row_spec(w):
    return pl.BlockSpec((_TM, w), lambda i: (i, 0))


_SLAB_SPEC = pl.BlockSpec((_TM * ROW_SLAB, LANES), lambda i: (i, 0))


def _in_proj(x, prev, mod, w_in_p, seq):
    t = x.shape[0]
    mod_spec = _mod_spec(mod, t, seq)
    outs = [jax.ShapeDtypeStruct((t, w), F32) for w in _IN_WIDTHS]
    out_specs = [_row_spec(w) for w in _IN_WIDTHS]
    w_spec = _const_spec((D_MODEL, IN_PAD))
    if prev is None:
        ins = [x, mod, w_in_p]
        in_specs = [_row_spec(D_MODEL), mod_spec, w_spec]
    else:
        moe_slab, pmod, g, b = prev
        ins = [x, moe_slab, pmod, g, b, mod, w_in_p]
        in_specs = [_row_spec(D_MODEL), _SLAB_SPEC, mod_spec,
                    _const_spec((1, D_MODEL)), _const_spec((1, D_MODEL)), mod_spec, w_spec]
        outs = [jax.ShapeDtypeStruct((t, D_MODEL), F32)] + outs
        out_specs = [_row_spec(D_MODEL)] + out_specs
    res = pl.pallas_call(
        functools.partial(_in_proj_kernel, prev is None),
        out_shape=outs, grid=(t // _TM,), in_specs=in_specs, out_specs=out_specs,
        compiler_params=_cparams(("arbitrary",), VMEM_LIMIT),
        name="in_proj",
    )(*ins)
    if prev is None:
        return (x,) + tuple(res)
    return tuple(res)


def _final_norm_kernel(x_ref, moe_ref, pmod_ref, g_ref, b_ref, o_ref):
    o_ref[...] = _post_norm2(x_ref, moe_ref, pmod_ref, g_ref, b_ref)


def _final_norm(x, moe_slab, pmod, g, b, seq):
    t = x.shape[0]
    return pl.pallas_call(
        _final_norm_kernel,
        out_shape=jax.ShapeDtypeStruct((t, D_MODEL), F32), grid=(t // _TM,),
        in_specs=[_row_spec(D_MODEL), _SLAB_SPEC, _mod_spec(pmod, t, seq),
                  _const_spec((1, D_MODEL)), _const_spec((1, D_MODEL))],
        out_specs=_row_spec(D_MODEL),
        compiler_params=_cparams(("arbitrary",)),
        name="final_norm",
    )(x, moe_slab, pmod, g, b)


def _ssd_kernel(has_h0, write_h, seq, *refs):
    refs = list(refs)
    z_ref, xbc_ref, sm_ref, cw_ref, cb_ref, dtb_ref, alog_ref, dsk_ref, ng_ref = refs[:9]
    pos = 9
    if has_h0:
        h0_ref = refs[pos]
        pos += 1
    y_ref = refs[pos]
    pos += 1
    if write_h:
        hf_ref = refs[pos]
        pos += 1
    pad_ref, xs_ref, bc_ref, ad_ref, st_ref = refs[pos:]
    ck = SSD_CHUNK
    nc = seq // ck
    halo = SUBLANES
    nh2 = 2 * SSD_HEADS

    pad_ref[0:halo, :] = jnp.zeros((halo, SSD_CONV_DIM), F32)
    pad_ref[halo + seq:2 * halo + seq, :] = jnp.zeros((halo, SSD_CONV_DIM), F32)
    pad_ref[halo:halo + seq, :] = xbc_ref[...]
    lane = lax.broadcasted_iota(I32, (ck, LANES), 1)

    def conv_chunk(c, carry):
        cs = pl.multiple_of(c * ck, ck)
        rows = pl.ds(cs, ck)
        cw_blk = 2 * LANES
        for cb in range(SSD_CONV_DIM // cw_blk):
            cols = slice(cb * cw_blk, (cb + 1) * cw_blk)
            win = pad_ref[pl.ds(cs, ck + 2 * halo), cols]
            acc = jnp.zeros((ck, cw_blk), F32) + cb_ref[:, cols]
            for k in range(SSD_CONV):
                s0 = halo + k - (SSD_CONV - 1) // 2
                acc = acc + win[s0:s0 + ck, :] * cw_ref[k:k + 1, cols]
            act = _silu(acc)
            if cb * cw_blk < SSD_DIM:
                xs_ref[rows, cols] = act
                y_ref[rows, cols] = act * dsk_ref[:, cols]
            else:
                bc_ref[rows, :] = act
        xr = sm_ref[rows, :] + dtb_ref[...]
        dt = jnp.maximum(xr, 0.0) + jnp.log(1.0 + jnp.exp(-jnp.abs(xr)))
        ad_ref[rows, :] = jnp.where(
            lane < nh2, dt * -jnp.exp(alog_ref[...]),
            jnp.where(lane < 2 * nh2, pltpu.roll(dt, nh2, axis=1), 0.0))
        return carry

    lax.fori_loop(0, nc, conv_chunk, 0)

    if has_h0:
        for d in range(2):
            h0 = h0_ref[d].reshape(SSD_DIM, SSD_STATE)
            h0 = jnp.concatenate([h0, jnp.zeros((SSD_DIM, LANES - SSD_STATE), F32)], axis=1)
            st_ref[d] = h0.T[:SSD_STATE, :]
    else:
        st_ref[...] = jnp.zeros(st_ref.shape, F32)

    ri = lax.broadcasted_iota(I32, (ck, ck), 0)
    ci = lax.broadcasted_iota(I32, (ck, ck), 1)
    tril = jnp.where(ci <= ri, 1.0, 0.0).astype(BF16)
    triu = jnp.where(ri <= ci, 1.0, 0.0).astype(BF16)
    lo_half = ci < SSD_STATE
    lo64 = lax.broadcasted_iota(I32, (SSD_STATE, LANES), 1) < SSD_STATE
    zeros_half = jnp.zeros((SSD_STATE, LANES), BF16)

    def chunk_dir(d, cs):
        ad = ad_ref[pl.ds(cs, ck), :]
        adt = ad.T
        ad3 = _split3(ad)
        acol = _dot(tril, ad3[0]) + _dot(tril, ad3[1]) + _dot(tril, ad3[2])
        at16 = adt[0:nh2, :]
        dt16 = adt[nh2:2 * nh2, :]
        at3 = _split3(at16)
        arow = _dot(at3[0], triu) + _dot(at3[1], triu) + _dot(at3[2], triu)
        bc = bc_ref[pl.ds(cs, ck), :]
        bm = bc[:, :LANES]
        cm = bc[:, LANES:]
        bt = bm.T
        bmb = bm.astype(BF16)
        cg = [jnp.where(lo_half, cm, 0.0), jnp.where(lo_half, 0.0, cm)]
        gm = [_dot_nt(cg[g].astype(BF16), bmb) for g in range(2)]
        alast16 = jnp.broadcast_to(arow[:, ck - 1:ck], (nh2, ck))
        if d == 1:
            acol = acol - ad
            arow = arow - at16
        for q in range(SSD_HEADS // 2):
            g = q // 2
            lhs_rows = []
            decl = []
            for hh in range(2):
                hi = d * SSD_HEADS + 2 * q + hh
                acol_b = jnp.broadcast_to(acol[:, hi:hi + 1], (ck, ck))
                arow_h = arow[hi:hi + 1, :]
                alast = alast16[hi:hi + 1, :]
                dtrow = dt16[hi:hi + 1, :]
                if d == 0:
                    lmat = jnp.where(ri >= ci, jnp.exp(acol_b - arow_h), 0.0)
                    wrow = jnp.exp(alast - arow_h)
                    ecol = jnp.exp(acol_b)
                else:
                    lmat = jnp.where(ci >= ri, jnp.exp(arow_h - acol_b), 0.0)
                    wrow = jnp.exp(arow_h)
                    ecol = jnp.exp(alast - acol_b)
                mm = (gm[g] * lmat * dtrow).astype(BF16)
                cs_h = (cg[g] * ecol).astype(BF16)
                bw = (bt[g * SSD_STATE:(g + 1) * SSD_STATE, :] * (wrow * dtrow)).astype(BF16)
                lhs_rows.append(jnp.concatenate([mm, cs_h], axis=1))
                lhs_rows.append(jnp.concatenate([bw, zeros_half], axis=1))
                decl.append(jnp.exp(alast))
            lhs = jnp.concatenate(lhs_rows, axis=0)
            cols = slice(q * LANES, (q + 1) * LANES)
            st_pair = st_ref[d, :, cols]
            stb = st_pair.astype(BF16)
            w_st = jnp.concatenate([stb, zeros_half] if g == 0 else [zeros_half, stb], axis=0)
            xs_pair = xs_ref[pl.ds(cs, ck), cols].astype(BF16)
            out = _dot(lhs, jnp.concatenate([xs_pair, w_st], axis=0))
            r2 = ck + SSD_STATE
            y_pair = jnp.where(lo_half, out[0:ck], out[r2:r2 + ck])
            st_new = jnp.where(lo64, out[ck:r2], out[r2 + ck:2 * r2])
            dec = jnp.where(lo64, decl[0], decl[1])
            st_ref[d, :, cols] = st_pair * dec + st_new
            y_ref[pl.ds(cs, ck), cols] += y_pair

    def body(c, carry):
        chunk_dir(0, pl.multiple_of(c * ck, ck))
        chunk_dir(1, pl.multiple_of((nc - 1 - c) * ck, ck))
        return carry

    lax.fori_loop(0, nc, body, 0)

    def gate_chunk(c, carry):
        rows = pl.ds(pl.multiple_of(c * ck, ck), ck)
        y = y_ref[rows, :] * _silu(z_ref[rows, :])
        y_ref[rows, :] = _rms(y) * ng_ref[...]
        return carry

    lax.fori_loop(0, nc, gate_chunk, 0)

    if write_h:
        for d in range(2):
            st = jnp.concatenate([st_ref[d], jnp.zeros((LANES - SSD_STATE, SSD_DIM), F32)], axis=0)
            hf_ref[d] = st.T[:, :SSD_STATE].reshape(SSD_HEADS, SSD_HEADDIM, SSD_STATE)


def _ssd(z, xbc, small, lp, nb, seq, h0, layer):
    has_h0 = h0 is not None
    write_h = not has_h0
    seq_spec = lambda w: pl.BlockSpec((seq, w), lambda b: (b, 0))
    ins = [z, xbc, small, lp["conv_w"], lp["conv_b"], lp["dt_bias"], lp["a_log"], lp["d_skip"],
           lp["ssd_norm_g"]]
    in_specs = [seq_spec(SSD_DIM), seq_spec(SSD_CONV_DIM), seq_spec(LANES),
                _const_spec((SSD_CONV, SSD_CONV_DIM)), _const_spec((1, SSD_CONV_DIM)),
                _const_spec((1, LANES)), _const_spec((1, LANES)), _const_spec((1, SSD_DIM)),
                _const_spec((1, SSD_DIM))]
    hshape = (2, SSD_HEADS, SSD_HEADDIM, SSD_STATE)
    if has_h0:
        ins.append(h0)
        in_specs.append(pl.BlockSpec((None, None) + hshape, lambda b: (b, layer, 0, 0, 0, 0)))
    outs = [jax.ShapeDtypeStruct((nb * seq, SSD_DIM), F32)]
    out_specs = [seq_spec(SSD_DIM)]
    if write_h:
        outs.append(jax.ShapeDtypeStruct((nb,) + hshape, F32))
        out_specs.append(pl.BlockSpec((None,) + hshape, lambda b: (b, 0, 0, 0, 0)))
    res = pl.pallas_call(
        functools.partial(_ssd_kernel, has_h0, write_h, seq),
        out_shape=outs, grid=(nb,), in_specs=in_specs, out_specs=out_specs,
        scratch_shapes=[pltpu.VMEM((seq + 2 * SUBLANES, SSD_CONV_DIM), F32),
                        pltpu.VMEM((seq, SSD_DIM), F32),
                        pltpu.VMEM((seq, 2 * LANES), F32),
                        pltpu.VMEM((seq, LANES), F32),
                        pltpu.VMEM((2, SSD_STATE, SSD_DIM), F32)],
        compiler_params=_cparams(("arbitrary",), VMEM_LIMIT),
        name="ssd",
    )(*ins)
    return res if write_h else (res[0], None)


def _mla_kernel(has_ctx, seq, *refs):
    refs = list(refs)
    cq_ref, ckv_ref, kr_ref, qg_ref, wq_ref, kvg_ref, wkv_ref = refs[:7]
    pos = 7
    if has_ctx:
        cckv_ref, ckr_ref, cos_ref, sin_ref = refs[pos:pos + 4]
        pos += 4
    y_ref = refs[pos]
    pos += 1
    if not has_ctx:
        ckvo_ref, kro_ref = refs[pos:pos + 2]

    qn = (_rms(cq_ref[...]) * qg_ref[...]).astype(BF16)
    qall = _dot(qn, wq_ref[...])
    q_nope = qall[:, :MLA_DIM]
    q_rope = qall[:, MLA_DIM:MLA_DIM + LANES]
    ckv_n = _rms(ckv_ref[...]) * kvg_ref[...]
    kr4 = kr_ref[:, :LANES]
    if has_ctx:
        q_rope = q_rope * cos_ref[...] + qall[:, MLA_DIM + LANES:] * sin_ref[...]
        kr4 = kr4 * cos_ref[...] + kr_ref[:, LANES:] * sin_ref[...]
        ckv_all = jnp.concatenate([cckv_ref[...], ckv_n], axis=0)
        kr_all = jnp.concatenate([ckr_ref[...], kr4], axis=0)
    else:
        ckvo_ref[...] = ckv_n
        kro_ref[...] = kr4[:, :ROPE_DIM]
        ckv_all = ckv_n
        kr_all = kr4
    kv = _dot(ckv_all.astype(BF16), wkv_ref[...])
    krb = kr_all.astype(BF16)
    lane = lax.broadcasted_iota(I32, (Q_TILE, LANES), 1)
    lo_half = lane < QK_NOPE
    for qt in range(seq // Q_TILE):
        rows = slice(qt * Q_TILE, (qt + 1) * Q_TILE)
        qr = q_rope[rows] * MLA_SCALE
        for pair in range(MLA_HEADS // 2):
            cols = slice(pair * LANES, (pair + 1) * LANES)
            kcat = jnp.concatenate([kv[:, cols].astype(BF16), krb], axis=1)
            vb = kv[:, MLA_DIM + pair * LANES:MLA_DIM + (pair + 1) * LANES].astype(BF16)
            qn_pair = q_nope[rows, cols] * MLA_SCALE
            outs = []
            for hh in range(2):
                h = 2 * pair + hh
                qa = jnp.where(lo_half if hh == 0 else jnp.logical_not(lo_half), qn_pair, 0.0)
                qb = jnp.where((lane >= h * ROPE_DIM) & (lane < (h + 1) * ROPE_DIM), qr, 0.0)
                s = _dot_nt(jnp.concatenate([qa, qb], axis=1).astype(BF16), kcat)
                p = jnp.exp(s - jnp.max(s, axis=-1, keepdims=True))
                o = _dot(p.astype(BF16), vb)
                outs.append(o / jnp.sum(p, axis=-1, keepdims=True))
            y_ref[rows, cols] = jnp.where(lo_half, outs[0], outs[1])


def _mla(cq, ckv, kr, lp, nb, seq, ctx, layer):
    has_ctx = ctx is not None
    seq_spec = lambda w: pl.BlockSpec((seq, w), lambda b: (b, 0))
    ins = [cq, ckv, kr, lp["q_norm_g"], lp["wq"], lp["kv_norm_g"], lp["wkv"]]
    in_specs = [seq_spec(Q_LORA), seq_spec(KV_LORA), seq_spec(2 * LANES),
                _const_spec((1, Q_LORA)), _const_spec((Q_LORA, 2 * MLA_DIM)),
                _const_spec((1, KV_LORA)), _const_spec((KV_LORA, 2 * MLA_DIM))]
    outs = [jax.ShapeDtypeStruct((nb * seq, MLA_DIM), F32)]
    out_specs = [seq_spec(MLA_DIM)]
    if has_ctx:
        cache_ckv, cache_kr4, cos4, sin4 = ctx
        past = cache_ckv.shape[2]
        ins += [cache_ckv, cache_kr4, cos4, sin4]
        in_specs += [pl.BlockSpec((None, None, past, KV_LORA), lambda b: (b, layer, 0, 0)),
                     pl.BlockSpec((None, None, past, LANES), lambda b: (b, layer, 0, 0)),
                     _const_spec((seq, LANES)), _const_spec((seq, LANES))]
    else:
        outs += [jax.ShapeDtypeStruct((nb, seq, KV_LORA), F32),
                 jax.ShapeDtypeStruct((nb, seq, ROPE_DIM), F32)]
        out_specs += [pl.BlockSpec((None, seq, KV_LORA), lambda b: (b, 0, 0)),
                      pl.BlockSpec((None, seq, ROPE_DIM), lambda b: (b, 0, 0))]
    return pl.pallas_call(
        functools.partial(_mla_kernel, has_ctx, seq),
        out_shape=outs, grid=(nb,), in_specs=in_specs, out_specs=out_specs,
        compiler_params=_cparams(("arbitrary",), VMEM_LIMIT),
        name="mla",
    )(*ins)


_FNET_ROWS = 1024


def _fnet_kernel(seq, g_ref, cbd_ref, sbd_ref, cs_ref, o_ref):
    gb = g_ref[...].astype(BF16)
    gc = _dot(gb, cbd_ref[...].astype(BF16)).astype(BF16)
    gs = _dot(gb, sbd_ref[...].astype(BF16)).astype(BF16)
    cs = cs_ref[...].astype(BF16)
    for i in range(_FNET_ROWS // seq):
        rows = slice(i * seq, (i + 1) * seq)
        o_ref[rows, :] = _dot(cs, jnp.concatenate([gc[rows], gs[rows]], axis=0))


def _dft_constants(seq):
    k = np.arange(FNET_GW)
    ang = 2.0 * np.pi * np.outer(k, k) / FNET_GW
    eye = np.eye(FNET_DIM // FNET_GW)
    cbd = np.kron(eye, np.cos(ang)) / np.sqrt(FNET_GW)
    sbd = np.kron(eye, np.sin(ang)) / np.sqrt(FNET_GW)
    s = np.arange(seq)
    angs = 2.0 * np.pi * np.outer(s, s) / seq
    cs = np.concatenate([np.cos(angs), -np.sin(angs)], axis=1) / np.sqrt(seq)
    return (jnp.asarray(cbd, F32), jnp.asarray(sbd, F32), jnp.asarray(cs, F32))


def _fnet(fin, nb, seq):
    cbd, sbd, cs = _dft_constants(seq)
    return pl.pallas_call(
        functools.partial(_fnet_kernel, seq),
        out_shape=jax.ShapeDtypeStruct((nb * seq, FNET_DIM), F32), grid=(nb * seq // _FNET_ROWS,),
        in_specs=[pl.BlockSpec((_FNET_ROWS, FNET_DIM), lambda b: (b, 0)),
                  _const_spec((FNET_DIM, FNET_DIM)), _const_spec((FNET_DIM, FNET_DIM)),
                  _const_spec((seq, 2 * seq))],
        out_specs=pl.BlockSpec((_FNET_ROWS, FNET_DIM), lambda b: (b, 0)),
        compiler_params=_cparams(("arbitrary",), VMEM_LIMIT),
        name="fnet",
    )(fin, cbd, sbd, cs)


def _out_proj_kernel(x_ref, ys_ref, ym_ref, yf_ref, mod_ref, w_ref, g_ref, b_ref, wr_ref,
                     x1_ref, u2p_ref, lg_ref):
    mix = (_dot(ys_ref[...].astype(BF16), w_ref[0:SSD_DIM, :])
           + _dot(ym_ref[...].astype(BF16), w_ref[SSD_DIM:SSD_DIM + MLA_DIM, :])
           + _dot(yf_ref[...].astype(BF16), w_ref[SSD_DIM + MLA_DIM:, :]))
    x1 = _ln(DN_ALPHA * x_ref[...] + mod_ref[2:3, :] * mix) * g_ref[...] + b_ref[...]
    x1_ref[...] = x1
    u2 = _ln(x1) * (1.0 + mod_ref[4:5, :]) + mod_ref[3:4, :]
    lg_ref[...] = _dot_nt(wr_ref[...], u2.astype(BF16))
    half = D_MODEL // 2
    words = pltpu.pack_elementwise([u2[:, :half], u2[:, half:]], packed_dtype=BF16)
    for j in range(PACK_SLAB):
        u2p_ref[pl.ds(j, _TM, stride=PACK_SLAB), :] = words[:, j * LANES:(j + 1) * LANES]


def _out_proj(x, yssd, ymla, yfft, mod, lp, seq):
    t = x.shape[0]
    return pl.pallas_call(
        _out_proj_kernel,
        out_shape=[jax.ShapeDtypeStruct((t, D_MODEL), F32),
                   jax.ShapeDtypeStruct((t * PACK_SLAB, LANES), U32),
                   jax.ShapeDtypeStruct((N_EXPERTS, t), F32)],
        grid=(t // _TM,),
        in_specs=[_row_spec(D_MODEL), _row_spec(SSD_DIM), _row_spec(MLA_DIM), _row_spec(FNET_DIM),
                  _mod_spec(mod, t, seq), _const_spec((D_MODEL, D_MODEL)),
                  _const_spec((1, D_MODEL)), _const_spec((1, D_MODEL)),
                  _const_spec((N_EXPERTS, D_MODEL))],
        out_specs=[_row_spec(D_MODEL),
                   pl.BlockSpec((_TM * PACK_SLAB, LANES), lambda i: (i, 0)),
                   pl.BlockSpec((N_EXPERTS, _TM), lambda i: (0, i))],
        compiler_params=_cparams(("arbitrary",), VMEM_LIMIT),
        name="out_proj",
    )(x, yssd, ymla, yfft, mod, lp["w_out"], lp["ln1_g"], lp["ln1_b"], lp["wrt"])


_RANK_SPLIT = 64.0


def _cumsum_tokens(x, triu, below):
    within = _dot(x.astype(BF16), triu)
    totals = jnp.broadcast_to(within[:, LANES - 1:LANES], within.shape)
    return within + _dot(below, totals.astype(BF16))


def _route_kernel(nblk, cap, lg_ref, idx_ref, aff_ref, p_ref, sel_ref):
    lg = lg_ref[...]
    e = jnp.exp(lg - jnp.max(lg, axis=0, keepdims=True))
    aff = e / jnp.sum(e, axis=0, keepdims=True)
    aff_ref[...] = aff
    capf = float(cap)

    def count(mask):
        s = jnp.sum(jnp.where(mask, 1.0, 0.0), axis=2, keepdims=True)
        return jnp.sum(s, axis=1, keepdims=True)

    def search(i, lo):
        cand = lo | jnp.left_shift(jnp.int32(1), 30 - i)
        return jnp.where(count(aff >= pltpu.bitcast(cand, F32)) >= capf, cand, lo)

    thr_bits = lax.fori_loop(0, 31, search, jnp.zeros((N_EXPERTS, 1, 1), I32))
    thr = pltpu.bitcast(thr_bits, F32)
    need = capf - count(aff > thr)

    rows = N_EXPERTS * nblk
    ri = lax.broadcasted_iota(I32, (LANES, LANES), 0)
    ci = lax.broadcasted_iota(I32, (LANES, LANES), 1)
    triu = jnp.where(ri <= ci, 1.0, 0.0).astype(BF16)
    rr = lax.broadcasted_iota(I32, (rows, rows), 0)
    rc = lax.broadcasted_iota(I32, (rows, rows), 1)
    below = jnp.where((rc < rr) & (rc >= (rr // nblk) * nblk), 1.0, 0.0).astype(BF16)
    flat = lambda v: v.reshape(rows, LANES)
    gt = flat(jnp.where(aff > thr, 1.0, 0.0))
    eq = flat(jnp.where(aff == thr, 1.0, 0.0))
    need_rows = flat(jnp.broadcast_to(need, aff.shape))
    eq_rank = _cumsum_tokens(eq, triu, below) - eq
    sel = gt + eq * jnp.where(eq_rank < need_rows, 1.0, 0.0)
    rank = _cumsum_tokens(sel, triu, below)
    sel_ref[...] = sel.reshape(N_EXPERTS, nblk, LANES)
    p_ref[...] = rank.reshape(N_EXPERTS, nblk, LANES)

    ones8 = jnp.ones((SUBLANES, LANES), BF16)
    bi = lax.broadcasted_iota(I32, (nblk, nblk), 0)
    bj = lax.broadcasted_iota(I32, (nblk, nblk), 1)
    triu_b = jnp.where(bi <= bj, 1.0, 0.0).astype(BF16)
    blk_ids = lax.broadcasted_iota(I32, (SUBLANES, nblk), 1).astype(F32).astype(BF16)
    r_col = lax.broadcasted_iota(I32, (cap, 1), 0).astype(F32)

    def per_expert(ex, carry):
        pe = p_ref[ex]
        tot_row = _dot_nt(ones8, sel_ref[ex].astype(BF16))[0:1, :]
        upto = _dot(jnp.broadcast_to(tot_row, (SUBLANES, nblk)).astype(BF16), triu_b)[0:1, :]
        onehot = jnp.where((upto - tot_row <= r_col) & (r_col < upto), 1.0, 0.0).astype(BF16)
        p_hi = jnp.floor(pe * (1.0 / _RANK_SPLIT))
        p_lo = pe - p_hi * _RANK_SPLIT
        ranks = _dot(onehot, p_hi.astype(BF16)) * _RANK_SPLIT + _dot(onehot, p_lo.astype(BF16))
        inside = jnp.where(ranks <= r_col, 1.0, 0.0).astype(BF16)
        cnt = _dot_nt(ones8, inside)[0:1, :] + float(LANES) * _dot_nt(blk_ids, onehot)[0:1, :]
        idx_ref[ex] = cnt.astype(I32)
        return carry

    lax.fori_loop(0, N_EXPERTS, per_expert, 0)


def _route(logits, cap):
    t = logits.shape[1]
    nblk = t // LANES
    idx, aff = pl.pallas_call(
        functools.partial(_route_kernel, nblk, cap),
        out_shape=[jax.ShapeDtypeStruct((N_EXPERTS, 1, cap), I32),
                   jax.ShapeDtypeStruct((N_EXPERTS, nblk, LANES), F32)],
        scratch_shapes=[pltpu.VMEM((N_EXPERTS, nblk, LANES), F32),
                        pltpu.VMEM((N_EXPERTS, nblk, LANES), F32)],
        compiler_params=_cparams(None, VMEM_LIMIT),
        name="route",
    )(logits.reshape(N_EXPERTS, nblk, LANES))
    return idx.reshape(N_EXPERTS, cap), aff.reshape(N_EXPERTS, t)


_UP_TF = 512
_UNROLL = 16


def _moe_up_kernel(caps, ua_ref, ub_ref, ia_ref, ib_ref, wg_ref, wu_ref, h_ref, tile_ref, xe_ref):
    ex = pl.program_id(0)
    m = sum(caps)

    @pl.when(pl.program_id(1) == 0)
    def _gather():
        row0 = 0
        for src_ref, idx_ref, cap in ((ua_ref, ia_ref, caps[0]), (ub_ref, ib_ref, caps[1])):
            def body(i, carry, src_ref=src_ref, idx_ref=idx_ref, cap=cap, row0=row0):
                for u in range(_UNROLL):
                    r = i * _UNROLL + u
                    tok = idx_ref[ex * cap + r]
                    src = pl.multiple_of(tok * PACK_SLAB, PACK_SLAB)
                    dst = pl.multiple_of((row0 + r) * PACK_SLAB, PACK_SLAB)
                    tile_ref[pl.ds(dst, PACK_SLAB), :] = src_ref[pl.ds(src, PACK_SLAB), :]
                return carry

            lax.fori_loop(0, cap // _UNROLL, body, 0)
            row0 += cap
        half = D_MODEL // 2
        for j in range(PACK_SLAB):
            w = tile_ref[pl.ds(j, m, stride=PACK_SLAB), :]
            for part in range(2):
                v = pltpu.unpack_elementwise(w, index=part, packed_dtype=BF16, unpacked_dtype=F32)
                lo = part * half + j * LANES
                xe_ref[:, lo:lo + LANES] = v.astype(BF16)

    x = xe_ref[...]
    g = _dot(x, wg_ref[...].astype(BF16))
    u = _dot(x, wu_ref[...].astype(BF16))
    h_ref[...] = (_silu(g) * u).astype(BF16)


def _moe_up(u2p_a, u2p_b, idx_a, idx_b, w_gate, w_up, layer):
    caps = (idx_a.shape[1], idx_b.shape[1])
    m = sum(caps)
    smem = pl.BlockSpec(memory_space=pltpu.SMEM)
    w_spec = pl.BlockSpec((None, None, D_MODEL, _UP_TF), lambda e, f: (layer, e, 0, f))
    return pl.pallas_call(
        functools.partial(_moe_up_kernel, caps),
        out_shape=jax.ShapeDtypeStruct((N_EXPERTS, m, EXPERT_FF), BF16),
        grid=(N_EXPERTS, EXPERT_FF // _UP_TF),
        in_specs=[pl.BlockSpec(u2p_a.shape, lambda e, f: (0, 0), pipeline_mode=pl.Buffered(1)),
                  pl.BlockSpec(u2p_b.shape, lambda e, f: (0, 0), pipeline_mode=pl.Buffered(1)),
                  smem, smem, w_spec, w_spec],
        out_specs=pl.BlockSpec((None, m, _UP_TF), lambda e, f: (e, 0, f)),
        scratch_shapes=[pltpu.VMEM((m * PACK_SLAB, LANES), U32), pltpu.VMEM((m, D_MODEL), BF16)],
        compiler_params=_cparams(("arbitrary", "arbitrary"), VMEM_LIMIT),
        name="moe_up",
    )(u2p_a, u2p_b, idx_a.reshape(-1), idx_b.reshape(-1), w_gate, w_up)


_DOWN_TN = 512
_RMW = 16


def _moe_down_kernel(cap, h_ref, wd_ref, idx_ref, aff_ref, out_ref, acc_ref, slab_ref):
    ex = pl.program_id(0)
    nn = pl.program_id(1)
    n_half = _DOWN_TN // LANES

    @pl.when((ex == 0) & (nn == 0))
    def _zero():
        acc_ref[...] = jnp.zeros(acc_ref.shape, F32)

    ye = _dot(h_ref[...], wd_ref[...].astype(BF16))
    for n in range(D_MODEL // _DOWN_TN):
        @pl.when(nn == n)
        def _to_slab(n=n):
            for j in range(n_half):
                slab_ref[pl.ds(n * n_half + j, cap, stride=ROW_SLAB), :] = ye[:, j * LANES:(j + 1) * LANES]

    @pl.when(nn == pl.num_programs(1) - 1)
    def _scatter():
        def body(i, carry):
            dsts, vals = [], []
            for u in range(_RMW):
                r = i * _RMW + u
                tok = idx_ref[ex * cap + r]
                dst = pl.multiple_of(tok * ROW_SLAB, ROW_SLAB)
                src = pl.multiple_of(r * ROW_SLAB, ROW_SLAB)
                vals.append(acc_ref[pl.ds(dst, ROW_SLAB), :]
                            + aff_ref[tok] * slab_ref[pl.ds(src, ROW_SLAB), :])
                dsts.append(dst)
            for dst, val in zip(dsts, vals):
                acc_ref[pl.ds(dst, ROW_SLAB), :] = val
            return carry

        lax.fori_loop(0, cap // _RMW, body, 0)

    @pl.when((ex == N_EXPERTS - 1) & (nn == pl.num_programs(1) - 1))
    def _write():
        pltpu.sync_copy(acc_ref, out_ref)


def _moe_down(h, w_down, idx, aff, row_block, layer):
    cap = idx.shape[1]
    t = aff.shape[1]
    return pl.pallas_call(
        functools.partial(_moe_down_kernel, cap),
        out_shape=jax.ShapeDtypeStruct((t * ROW_SLAB, LANES), F32),
        grid=(N_EXPERTS, D_MODEL // _DOWN_TN),
        in_specs=[pl.BlockSpec((None, cap, EXPERT_FF), lambda e, n: (e, row_block, 0)),
                  pl.BlockSpec((None, None, EXPERT_FF, _DOWN_TN), lambda e, n: (layer, e, 0, n)),
                  pl.BlockSpec(memory_space=pltpu.SMEM),
                  pl.BlockSpec((None, None, t), lambda e, n: (e, 0, 0), memory_space=pltpu.SMEM)],
        out_specs=pl.BlockSpec(memory_space=pl.ANY),
        scratch_shapes=[pltpu.VMEM((t * ROW_SLAB, LANES), F32),
                        pltpu.VMEM((cap * ROW_SLAB, LANES), F32)],
        compiler_params=_cparams(("arbitrary", "arbitrary"), VMEM_LIMIT),
        name="moe_down",
    )(h, w_down, idx.reshape(-1), aff[:, None, :])


def _swap_pairs(w):
    s = w.shape
    return jnp.flip(w.reshape(s[:-1] + (s[-1] // 2, 2)), axis=-1).reshape(s)


def _pack_w_in(w_in):
    o_dt, o_cq, o_kr, o_fin = 1280, 1296, 1680, 1712
    kr = w_in[:, :, o_kr:o_fin]
    pad = jnp.zeros(w_in.shape[:2] + (_IN_KR4 - _IN_SMALL - 2 * SSD_HEADS,), w_in.dtype)
    parts = [w_in[:, :, :o_dt], w_in[:, :, o_cq:o_kr], w_in[:, :, o_fin:], w_in[:, :, o_dt:o_cq], pad,
             jnp.tile(kr, (1, 1, MLA_HEADS)), jnp.tile(_swap_pairs(kr), (1, 1, MLA_HEADS))]
    return jnp.concatenate(parts, axis=2).astype(BF16)


def _pack_w_uq(w_uq):
    w = w_uq.reshape(w_uq.shape[:2] + (MLA_HEADS, QK_NOPE + ROPE_DIM))
    flat = lambda v: v.reshape(v.shape[:2] + (-1,))
    rope = w[..., QK_NOPE:]
    return jnp.concatenate([flat(w[..., :QK_NOPE]), flat(rope), flat(_swap_pairs(rope))],
                           axis=2).astype(BF16)


def _pack_w_ukv(w_ukv):
    w = w_ukv.reshape(w_ukv.shape[:2] + (MLA_HEADS, QK_NOPE + V_DIM))
    flat = lambda v: v.reshape(v.shape[:2] + (-1,))
    return jnp.concatenate([flat(w[..., :QK_NOPE]), flat(w[..., QK_NOPE:])], axis=2).astype(BF16)


def _rope_tables(seq):
    rows = seq // GRID_W
    r, col = jnp.meshgrid(jnp.arange(rows, dtype=F32), jnp.arange(GRID_W, dtype=F32), indexing="ij")
    pairs = ROPE_DIM // 4
    inv = ROPE_BASE ** (-jnp.arange(pairs, dtype=F32) / pairs)
    ang = jnp.concatenate([r.reshape(-1, 1) * inv, col.reshape(-1, 1) * inv], axis=-1)
    cos = jnp.repeat(jnp.cos(ang), 2, axis=1)
    sin = jnp.repeat(jnp.sin(ang), 2, axis=1) * jnp.tile(jnp.array([-1.0, 1.0], F32), ROPE_DIM // 2)
    return jnp.tile(cos, (1, MLA_HEADS)), jnp.tile(sin, (1, MLA_HEADS))


def _pad_lanes(v):
    return jnp.pad(v, ((0, 0), (0, LANES - v.shape[1])))[:, None, :]


def kernel(x_prompt, x_sample, cache_ckv, cache_krope, state_ssm, c, c_ctx,
           w_in, conv_w, conv_b, dt_bias, a_log, d_skip, ssd_norm_g,
           q_norm_g, w_uq, kv_norm_g, w_ukv, w_out, w_mod, b_mod,
           ln1_g, ln1_b, ln2_g, ln2_b, w_router, w_gate, w_up, w_down):
    nb_c, seq_c, _ = x_prompt.shape
    nb_l, seq_l, _ = x_sample.shape
    t_c, t_l = nb_c * seq_c, nb_l * seq_l
    caps = (CAPACITY_FACTOR * t_c // N_EXPERTS, CAPACITY_FACTOR * t_l // N_EXPERTS)

    w_in_p = _pack_w_in(w_in)
    wq = _pack_w_uq(w_uq)
    wkv = _pack_w_ukv(w_ukv)
    w_out_b = w_out.astype(BF16)
    wrt = jnp.swapaxes(w_router, 1, 2).astype(BF16)
    dtb = _pad_lanes(dt_bias.reshape(DEPTH, -1))
    alog = _pad_lanes(a_log.reshape(DEPTH, -1))
    dsk = jnp.repeat(d_skip, SSD_HEADDIM, axis=1)[:, None, :]
    cache_kr4 = jnp.tile(cache_krope, (1, 1, 1, MLA_HEADS))
    cos4, sin4 = _rope_tables(seq_l)

    cond = jnp.zeros((SUBLANES, D_MODEL), F32).at[0].set(c_ctx).at[1:1 + nb_l].set(c)
    mod_all = _modulation(cond, w_mod, b_mod).reshape(DEPTH, SUBLANES, 6, D_MODEL)

    streams = [
        dict(x=x_prompt.reshape(t_c, D_MODEL), nb=nb_c, seq=seq_c, cap=caps[0]),
        dict(x=x_sample.reshape(t_l, D_MODEL), nb=nb_l, seq=seq_l, cap=caps[1]),
    ]
    prev = [None, None]
    ckvs, krs, ssms = [], [], []
    for l in range(DEPTH):
        lp = dict(conv_w=conv_w[l], conv_b=conv_b[l][None], dt_bias=dtb[l], a_log=alog[l],
                  d_skip=dsk[l], ssd_norm_g=ssd_norm_g[l][None],
                  q_norm_g=q_norm_g[l][None], wq=wq[l], kv_norm_g=kv_norm_g[l][None], wkv=wkv[l],
                  w_out=w_out_b[l], ln1_g=ln1_g[l][None], ln1_b=ln1_b[l][None], wrt=wrt[l])
        mods = [mod_all[l, 0:1], mod_all[l, 1:1 + nb_l]]
        routed = []
        for si, st in enumerate(streams):
            nb, seq = st["nb"], st["seq"]
            x, z, xbc, cq, ckv, fin, small, kr = _in_proj(st["x"], prev[si], mods[si], w_in_p[l], seq)
            if si == 0:
                yssd, ssm = _ssd(z, xbc, small, lp, nb, seq, None, l)
                ymla, ckv_n, kr_o = _mla(cq, ckv, kr, lp, nb, seq, None, l)
                ckvs.append(ckv_n)
                krs.append(kr_o)
                ssms.append(ssm)
            else:
                yssd, _ = _ssd(z, xbc, small, lp, nb, seq, state_ssm, l)
                (ymla,) = _mla(cq, ckv, kr, lp, nb, seq, (cache_ckv, cache_kr4, cos4, sin4), l)
            yfft = _fnet(fin, nb, seq)
            x1, u2p, logits = _out_proj(x, yssd, ymla, yfft, mods[si], lp, seq)
            idx, aff = _route(logits, st["cap"])
            st["x"] = x1
            routed.append((u2p, idx, aff))
        h = _moe_up(routed[0][0], routed[1][0], routed[0][1], routed[1][1], w_gate, w_up, l)
        for si in range(2):
            moe = _moe_down(h, w_down, routed[si][1], routed[si][2], 0 if si == 0 else 2, l)
            prev[si] = (moe, mods[si], ln2_g[l][None], ln2_b[l][None])
    y_p = _final_norm(streams[0]["x"], *prev[0], seq_c).reshape(x_prompt.shape)
    y_s = _final_norm(streams[1]["x"], *prev[1], seq_l).reshape(x_sample.shape)
    return (y_p, y_s, jnp.stack(ckvs, axis=1), jnp.stack(krs, axis=1), jnp.stack(ssms, axis=1))
```

```python
import functools

import jax
import jax.numpy as jnp
import numpy as np
from jax import lax
from jax.experimental import pallas as pl
from jax.experimental.pallas import tpu as pltpu

F32 = jnp.float32
BF16 = jnp.bfloat16
I32 = jnp.int32
U32 = jnp.uint32

D_MODEL = 1024
DEPTH = 4
GRID_W = 64
SSD_HEADS = 8
SSD_HEADDIM = 64
SSD_DIM = SSD_HEADS * SSD_HEADDIM
SSD_STATE = 64
SSD_CONV = 5
SSD_CHUNK = 128
SSD_CONV_DIM = SSD_DIM + 4 * SSD_STATE
MLA_HEADS = 4
Q_LORA = 256
KV_LORA = 128
QK_NOPE = 64
ROPE_DIM = 32
V_DIM = 64
MLA_DIM = MLA_HEADS * V_DIM
MLA_SCALE = (QK_NOPE + ROPE_DIM) ** -0.5
ROPE_BASE = 10000.0
Q_TILE = 256
FNET_GW = 64
FNET_DIM = 256
N_EXPERTS = 16
EXPERT_FF = 1024
CAPACITY_FACTOR = 2
DN_ALPHA = (2 * DEPTH) ** 0.25
EPS = 1e-5

LANES = 128
SUBLANES = 8
ROW_SLAB = D_MODEL // LANES
PACK_SLAB = ROW_SLAB // 2
VMEM_LIMIT = 56 * 1024 * 1024

_IN_Z = 0
_IN_XBC = 512
_IN_CQ = 1280
_IN_CKV = 1536
_IN_FIN = 1664
_IN_SMALL = 1920
_IN_KR4 = 2048
IN_PAD = 2304
_IN_WIDTHS = (512, 768, 256, 128, 256, 128, 256)


def _cparams(sem, vmem=None):
    return pltpu.CompilerParams(dimension_semantics=sem, vmem_limit_bytes=vmem)


def _ln(x):
    mu = jnp.mean(x, axis=-1, keepdims=True)
    xc = x - mu
    var = jnp.mean(xc * xc, axis=-1, keepdims=True)
    return xc * lax.rsqrt(var + EPS)


def _rms(x):
    return x * lax.rsqrt(jnp.mean(x * x, axis=-1, keepdims=True) + EPS)


def _silu(x):
    return x * (1.0 / (1.0 + jnp.exp(-x)))


def _dot(a, b):
    return jnp.dot(a, b, preferred_element_type=F32)


def _dot_nt(a, b):
    return lax.dot_general(a, b, (((1,), (1,)), ((), ())), preferred_element_type=F32)


def _split3(x):
    hi = x.astype(BF16)
    r1 = x - hi.astype(F32)
    mid = r1.astype(BF16)
    lo = (r1 - mid.astype(F32)).astype(BF16)
    return hi, mid, lo


def _const_spec(shape):
    return pl.BlockSpec(shape, lambda *_: (0,) * len(shape))


def _layer_spec(shape, layer):
    return pl.BlockSpec((None,) + shape, lambda *_: (layer,) + (0,) * len(shape))


_MOD_TN = 1536


def _mod_kernel(c_ref, w_ref, b_ref, o_ref):
    a = _silu(c_ref[...]).astype(BF16)
    o_ref[...] = _dot(a, w_ref[...].astype(BF16)) + b_ref[...]


def _modulation(cond, w_mod, b_mod):
    n = 6 * D_MODEL
    return pl.pallas_call(
        _mod_kernel,
        out_shape=jax.ShapeDtypeStruct((DEPTH, SUBLANES, n), F32),
        grid=(DEPTH, n // _MOD_TN),
        in_specs=[
            pl.BlockSpec((SUBLANES, D_MODEL), lambda l, j: (0, 0)),
            pl.BlockSpec((None, D_MODEL, _MOD_TN), lambda l, j: (l, 0, j)),
            pl.BlockSpec((None, 1, _MOD_TN), lambda l, j: (l, 0, j)),
        ],
        out_specs=pl.BlockSpec((None, SUBLANES, _MOD_TN), lambda l, j: (l, 0, j)),
        compiler_params=_cparams(("arbitrary", "arbitrary")),
        name="modulation",
    )(cond, w_mod, b_mod.reshape(DEPTH, 1, n))


_TM = 512


def _read_rows(slab_ref, tm):
    return jnp.concatenate(
        [slab_ref[pl.ds(j, tm, stride=ROW_SLAB), :] for j in range(ROW_SLAB)], axis=-1)


def _post_norm2(x_ref, moe_ref, pmod_ref, g_ref, b_ref):
    moe = _read_rows(moe_ref, _TM)
    return _ln(DN_ALPHA * x_ref[...] + pmod_ref[5:6, :] * moe) * g_ref[...] + b_ref[...]


def _in_proj_kernel(first, *refs):
    if first:
        x_ref, mod_ref, w_ref = refs[:3]
        outs = refs[3:]
        x = x_ref[...]
    else:
        x_ref, moe_ref, pmod_ref, g_ref, b_ref, mod_ref, w_ref, xo_ref = refs[:8]
        outs = refs[8:]
        x = _post_norm2(x_ref, moe_ref, pmod_ref, g_ref, b_ref)
        xo_ref[...] = x
    u = _ln(x) * (1.0 + mod_ref[1:2, :]) + mod_ref[0:1, :]
    p = _dot(u.astype(BF16), w_ref[...])
    off = 0
    for o_ref, w in zip(outs, _IN_WIDTHS):
        o_ref[...] = p[:, off:off + w]
        off += w


def _mod_spec(mod_rows, t, seq, layer):
    row0, count = mod_rows
    per = seq // _TM if count > 1 else t // _TM
    return pl.BlockSpec((None, None, 6, D_MODEL), lambda i: (layer, row0 + i // per, 0, 0))


def _row_spec(w):
    return pl.BlockSpec((_TM, w), lambda i: (i, 0))


_SLAB_SPEC = pl.BlockSpec((_TM * ROW_SLAB, LANES), lambda i: (i, 0))


def _in_proj(x, prev, mod_all, mod_rows, w_in_p, seq, layer):
    t = x.shape[0]
    mod_spec = _mod_spec(mod_rows, t, seq, layer)
    outs = [jax.ShapeDtypeStruct((t, w), F32) for w in _IN_WIDTHS]
    out_specs = [_row_spec(w) for w in _IN_WIDTHS]
    w_spec = _layer_spec((D_MODEL, IN_PAD), layer)
    if prev is None:
        ins = [x, mod_all, w_in_p]
        in_specs = [_row_spec(D_MODEL), mod_spec, w_spec]
    else:
        moe_slab, g, b = prev
        ins = [x, moe_slab, mod_all, g, b, mod_all, w_in_p]
        in_specs = [_row_spec(D_MODEL), _SLAB_SPEC, _mod_spec(mod_rows, t, seq, layer - 1),
                    _layer_spec((1, D_MODEL), layer - 1), _layer_spec((1, D_MODEL), layer - 1),
                    mod_spec, w_spec]
        outs = [jax.ShapeDtypeStruct((t, D_MODEL), F32)] + outs
        out_specs = [_row_spec(D_MODEL)] + out_specs
    res = pl.pallas_call(
        functools.partial(_in_proj_kernel, prev is None),
        out_shape=outs, grid=(t // _TM,), in_specs=in_specs, out_specs=out_specs,
        compiler_params=_cparams(("arbitrary",), VMEM_LIMIT),
        name="in_proj",
    )(*ins)
    if prev is None:
        return (x,) + tuple(res)
    return tuple(res)


def _final_norm_kernel(x_ref, moe_ref, pmod_ref, g_ref, b_ref, o_ref):
    o_ref[...] = _post_norm2(x_ref, moe_ref, pmod_ref, g_ref, b_ref)


def _final_norm(x, moe_slab, g, b, mod_all, mod_rows, seq, layer):
    t = x.shape[0]
    return pl.pallas_call(
        _final_norm_kernel,
        out_shape=jax.ShapeDtypeStruct((t, D_MODEL), F32), grid=(t // _TM,),
        in_specs=[_row_spec(D_MODEL), _SLAB_SPEC, _mod_spec(mod_rows, t, seq, layer),
                  _layer_spec((1, D_MODEL), layer), _layer_spec((1, D_MODEL), layer)],
        out_specs=_row_spec(D_MODEL),
        compiler_params=_cparams(("arbitrary",)),
        name="final_norm",
    )(x, moe_slab, mod_all, g, b)


def _ssd_kernel(has_h0, write_h, seq, *refs):
    refs = list(refs)
    z_ref, xbc_ref, sm_ref, cw_ref, cb_ref, dtb_ref, alog_ref, dsk_ref, ng_ref = refs[:9]
    pos = 9
    if has_h0:
        h0_ref = refs[pos]
        pos += 1
    y_ref = refs[pos]
    pos += 1
    if write_h:
        hf_ref = refs[pos]
        pos += 1
    pad_ref, xs_ref, bc_ref, ad_ref, st_ref = refs[pos:]
    ck = SSD_CHUNK
    nc = seq // ck
    halo = SUBLANES
    nh2 = 2 * SSD_HEADS

    pad_ref[0:halo, :] = jnp.zeros((halo, SSD_CONV_DIM), F32)
    pad_ref[halo + seq:2 * halo + seq, :] = jnp.zeros((halo, SSD_CONV_DIM), F32)
    pad_ref[halo:halo + seq, :] = xbc_ref[...]
    lane = lax.broadcasted_iota(I32, (ck, LANES), 1)

    def conv_chunk(c, carry):
        cs = pl.multiple_of(c * ck, ck)
        rows = pl.ds(cs, ck)
        cw_blk = 2 * LANES
        for cb in range(SSD_CONV_DIM // cw_blk):
            cols = slice(cb * cw_blk, (cb + 1) * cw_blk)
            win = pad_ref[pl.ds(cs, ck + 2 * halo), cols]
            acc = jnp.zeros((ck, cw_blk), F32) + cb_ref[:, cols]
            for k in range(SSD_CONV):
                s0 = halo + k - (SSD_CONV - 1) // 2
                acc = acc + win[s0:s0 + ck, :] * cw_ref[k:k + 1, cols]
            act = _silu(acc)
            if cb * cw_blk < SSD_DIM:
                xs_ref[rows, cols] = act
                y_ref[rows, cols] = act * dsk_ref[:, cols]
            else:
                bc_ref[rows, :] = act
        xr = sm_ref[rows, :] + dtb_ref[...]
        dt = jnp.maximum(xr, 0.0) + jnp.log(1.0 + jnp.exp(-jnp.abs(xr)))
        ad_ref[rows, :] = jnp.where(
            lane < nh2, dt * -jnp.exp(alog_ref[...]),
            jnp.where(lane < 2 * nh2, pltpu.roll(dt, nh2, axis=1), 0.0))
        return carry

    lax.fori_loop(0, nc, conv_chunk, 0)

    if has_h0:
        for d in range(2):
            h0 = h0_ref[d].reshape(SSD_DIM, SSD_STATE)
            h0 = jnp.concatenate([h0, jnp.zeros((SSD_DIM, LANES - SSD_STATE), F32)], axis=1)
            st_ref[d] = h0.T[:SSD_STATE, :]
    else:
        st_ref[...] = jnp.zeros(st_ref.shape, F32)

    ri = lax.broadcasted_iota(I32, (ck, ck), 0)
    ci = lax.broadcasted_iota(I32, (ck, ck), 1)
    tril = jnp.where(ci <= ri, 1.0, 0.0).astype(BF16)
    triu = jnp.where(ri <= ci, 1.0, 0.0).astype(BF16)
    lo_half = ci < SSD_STATE
    lo64 = lax.broadcasted_iota(I32, (SSD_STATE, LANES), 1) < SSD_STATE
    zeros_half = jnp.zeros((SSD_STATE, LANES), BF16)

    def chunk_dir(d, cs):
        ad = ad_ref[pl.ds(cs, ck), :]
        adt = ad.T
        ad3 = _split3(ad)
        acol = _dot(tril, ad3[0]) + _dot(tril, ad3[1]) + _dot(tril, ad3[2])
        at16 = adt[0:nh2, :]
        dt16 = adt[nh2:2 * nh2, :]
        at3 = _split3(at16)
        arow = _dot(at3[0], triu) + _dot(at3[1], triu) + _dot(at3[2], triu)
        bc = bc_ref[pl.ds(cs, ck), :]
        bm = bc[:, :LANES]
        cm = bc[:, LANES:]
        bt = bm.T
        bmb = bm.astype(BF16)
        cg = [jnp.where(lo_half, cm, 0.0), jnp.where(lo_half, 0.0, cm)]
        gm = [_dot_nt(cg[g].astype(BF16), bmb) for g in range(2)]
        alast16 = jnp.broadcast_to(arow[:, ck - 1:ck], (nh2, ck))
        if d == 1:
            acol = acol - ad
            arow = arow - at16
        for q in range(SSD_HEADS // 2):
            g = q // 2
            lhs_rows = []
            decl = []
            for hh in range(2):
                hi = d * SSD_HEADS + 2 * q + hh
                acol_b = jnp.broadcast_to(acol[:, hi:hi + 1], (ck, ck))
                arow_h = arow[hi:hi + 1, :]
                alast = alast16[hi:hi + 1, :]
                dtrow = dt16[hi:hi + 1, :]
                if d == 0:
                    lmat = jnp.where(ri >= ci, jnp.exp(acol_b - arow_h), 0.0)
                    wrow = jnp.exp(alast - arow_h)
                    ecol = jnp.exp(acol_b)
                else:
                    lmat = jnp.where(ci >= ri, jnp.exp(arow_h - acol_b), 0.0)
                    wrow = jnp.exp(arow_h)
                    ecol = jnp.exp(alast - acol_b)
                mm = (gm[g] * lmat * dtrow).astype(BF16)
                cs_h = (cg[g] * ecol).astype(BF16)
                bw = (bt[g * SSD_STATE:(g + 1) * SSD_STATE, :] * (wrow * dtrow)).astype(BF16)
                lhs_rows.append(jnp.concatenate([mm, cs_h], axis=1))
                lhs_rows.append(jnp.concatenate([bw, zeros_half], axis=1))
                decl.append(jnp.exp(alast))
            lhs = jnp.concatenate(lhs_rows, axis=0)
            cols = slice(q * LANES, (q + 1) * LANES)
            st_pair = st_ref[d, :, cols]
            stb = st_pair.astype(BF16)
            w_st = jnp.concatenate([stb, zeros_half] if g == 0 else [zeros_half, stb], axis=0)
            xs_pair = xs_ref[pl.ds(cs, ck), cols].astype(BF16)
            out = _dot(lhs, jnp.concatenate([xs_pair, w_st], axis=0))
            r2 = ck + SSD_STATE
            y_pair = jnp.where(lo_half, out[0:ck], out[r2:r2 + ck])
            st_new = jnp.where(lo64, out[ck:r2], out[r2 + ck:2 * r2])
            dec = jnp.where(lo64, decl[0], decl[1])
            st_ref[d, :, cols] = st_pair * dec + st_new
            y_ref[pl.ds(cs, ck), cols] += y_pair

    def body(c, carry):
        chunk_dir(0, pl.multiple_of(c * ck, ck))
        chunk_dir(1, pl.multiple_of((nc - 1 - c) * ck, ck))
        return carry

    lax.fori_loop(0, nc, body, 0)

    def gate_chunk(c, carry):
        rows = pl.ds(pl.multiple_of(c * ck, ck), ck)
        y = y_ref[rows, :] * _silu(z_ref[rows, :])
        y_ref[rows, :] = _rms(y) * ng_ref[...]
        return carry

    lax.fori_loop(0, nc, gate_chunk, 0)

    if write_h:
        for d in range(2):
            st = jnp.concatenate([st_ref[d], jnp.zeros((LANES - SSD_STATE, SSD_DIM), F32)], axis=0)
            hf_ref[d] = st.T[:, :SSD_STATE].reshape(SSD_HEADS, SSD_HEADDIM, SSD_STATE)


def _ssd(z, xbc, small, lp, nb, seq, h0, layer):
    has_h0 = h0 is not None
    write_h = not has_h0
    seq_spec = lambda w: pl.BlockSpec((seq, w), lambda b: (b, 0))
    ins = [z, xbc, small, lp["conv_w"], lp["conv_b"], lp["dt_bias"], lp["a_log"], lp["d_skip"],
           lp["ssd_norm_g"]]
    in_specs = [seq_spec(SSD_DIM), seq_spec(SSD_CONV_DIM), seq_spec(LANES),
                _layer_spec((SSD_CONV, SSD_CONV_DIM), layer), _layer_spec((1, SSD_CONV_DIM), layer),
                _layer_spec((1, LANES), layer), _layer_spec((1, LANES), layer),
                _layer_spec((1, SSD_DIM), layer), _layer_spec((1, SSD_DIM), layer)]
    hshape = (2, SSD_HEADS, SSD_HEADDIM, SSD_STATE)
    if has_h0:
        ins.append(h0)
        in_specs.append(pl.BlockSpec((None, None) + hshape, lambda b: (b, layer, 0, 0, 0, 0)))
    outs = [jax.ShapeDtypeStruct((nb * seq, SSD_DIM), F32)]
    out_specs = [seq_spec(SSD_DIM)]
    if write_h:
        outs.append(jax.ShapeDtypeStruct((nb,) + hshape, F32))
        out_specs.append(pl.BlockSpec((None,) + hshape, lambda b: (b, 0, 0, 0, 0)))
    res = pl.pallas_call(
        functools.partial(_ssd_kernel, has_h0, write_h, seq),
        out_shape=outs, grid=(nb,), in_specs=in_specs, out_specs=out_specs,
        scratch_shapes=[pltpu.VMEM((seq + 2 * SUBLANES, SSD_CONV_DIM), F32),
                        pltpu.VMEM((seq, SSD_DIM), F32),
                        pltpu.VMEM((seq, 2 * LANES), F32),
                        pltpu.VMEM((seq, LANES), F32),
                        pltpu.VMEM((2, SSD_STATE, SSD_DIM), F32)],
        compiler_params=_cparams(("arbitrary",), VMEM_LIMIT),
        name="ssd",
    )(*ins)
    return res if write_h else (res[0], None)


def _mla_kernel(has_ctx, seq, *refs):
    refs = list(refs)
    cq_ref, ckv_ref, kr_ref, qg_ref, wq_ref, kvg_ref, wkv_ref = refs[:7]
    pos = 7
    if has_ctx:
        cckv_ref, ckr_ref, cos_ref, sin_ref = refs[pos:pos + 4]
        pos += 4
    y_ref = refs[pos]
    pos += 1
    if not has_ctx:
        ckvo_ref, kro_ref = refs[pos:pos + 2]

    qn = (_rms(cq_ref[...]) * qg_ref[...]).astype(BF16)
    qall = _dot(qn, wq_ref[...])
    q_nope = qall[:, :MLA_DIM]
    q_rope = qall[:, MLA_DIM:MLA_DIM + LANES]
    ckv_n = _rms(ckv_ref[...]) * kvg_ref[...]
    kr4 = kr_ref[:, :LANES]
    if has_ctx:
        q_rope = q_rope * cos_ref[...] + qall[:, MLA_DIM + LANES:] * sin_ref[...]
        kr4 = kr4 * cos_ref[...] + kr_ref[:, LANES:] * sin_ref[...]
        ckv_all = jnp.concatenate([cckv_ref[...], ckv_n], axis=0)
        kr_all = jnp.concatenate([ckr_ref[...], kr4], axis=0)
    else:
        ckvo_ref[...] = ckv_n
        kro_ref[...] = kr4[:, :ROPE_DIM]
        ckv_all = ckv_n
        kr_all = kr4
    kv = _dot(ckv_all.astype(BF16), wkv_ref[...])
    krb = kr_all.astype(BF16)
    lane = lax.broadcasted_iota(I32, (Q_TILE, LANES), 1)
    lo_half = lane < QK_NOPE
    for qt in range(seq // Q_TILE):
        rows = slice(qt * Q_TILE, (qt + 1) * Q_TILE)
        qr = q_rope[rows] * MLA_SCALE
        for pair in range(MLA_HEADS // 2):
            cols = slice(pair * LANES, (pair + 1) * LANES)
            kcat = jnp.concatenate([kv[:, cols].astype(BF16), krb], axis=1)
            vb = kv[:, MLA_DIM + pair * LANES:MLA_DIM + (pair + 1) * LANES].astype(BF16)
            qn_pair = q_nope[rows, cols] * MLA_SCALE
            outs = []
            for hh in range(2):
                h = 2 * pair + hh
                qa = jnp.where(lo_half if hh == 0 else jnp.logical_not(lo_half), qn_pair, 0.0)
                qb = jnp.where((lane >= h * ROPE_DIM) & (lane < (h + 1) * ROPE_DIM), qr, 0.0)
                s = _dot_nt(jnp.concatenate([qa, qb], axis=1).astype(BF16), kcat)
                p = jnp.exp(s - jnp.max(s, axis=-1, keepdims=True))
                o = _dot(p.astype(BF16), vb)
                outs.append(o / jnp.sum(p, axis=-1, keepdims=True))
            y_ref[rows, cols] = jnp.where(lo_half, outs[0], outs[1])


def _mla(cq, ckv, kr, lp, nb, seq, ctx, layer):
    has_ctx = ctx is not None
    seq_spec = lambda w: pl.BlockSpec((seq, w), lambda b: (b, 0))
    ins = [cq, ckv, kr, lp["q_norm_g"], lp["wq"], lp["kv_norm_g"], lp["wkv"]]
    in_specs = [seq_spec(Q_LORA), seq_spec(KV_LORA), seq_spec(2 * LANES),
                _layer_spec((1, Q_LORA), layer), _layer_spec((Q_LORA, 2 * MLA_DIM), layer),
                _layer_spec((1, KV_LORA), layer), _layer_spec((KV_LORA, 2 * MLA_DIM), layer)]
    outs = [jax.ShapeDtypeStruct((nb * seq, MLA_DIM), F32)]
    out_specs = [seq_spec(MLA_DIM)]
    if has_ctx:
        cache_ckv, cache_kr4, cos4, sin4 = ctx
        past = cache_ckv.shape[2]
        ins += [cache_ckv, cache_kr4, cos4, sin4]
        in_specs += [pl.BlockSpec((None, None, past, KV_LORA), lambda b: (b, layer, 0, 0)),
                     pl.BlockSpec((None, None, past, LANES), lambda b: (b, layer, 0, 0)),
                     _const_spec((seq, LANES)), _const_spec((seq, LANES))]
    else:
        outs += [jax.ShapeDtypeStruct((nb, seq, KV_LORA), F32),
                 jax.ShapeDtypeStruct((nb, seq, ROPE_DIM), F32)]
        out_specs += [pl.BlockSpec((None, seq, KV_LORA), lambda b: (b, 0, 0)),
                      pl.BlockSpec((None, seq, ROPE_DIM), lambda b: (b, 0, 0))]
    return pl.pallas_call(
        functools.partial(_mla_kernel, has_ctx, seq),
        out_shape=outs, grid=(nb,), in_specs=in_specs, out_specs=out_specs,
        compiler_params=_cparams(("arbitrary",), VMEM_LIMIT),
        name="mla",
    )(*ins)


_FNET_ROWS = 1024


def _fnet_kernel(seq, g_ref, cbd_ref, sbd_ref, cs_ref, o_ref):
    gb = g_ref[...].astype(BF16)
    gc = _dot(gb, cbd_ref[...].astype(BF16)).astype(BF16)
    gs = _dot(gb, sbd_ref[...].astype(BF16)).astype(BF16)
    cs = cs_ref[...].astype(BF16)
    for i in range(_FNET_ROWS // seq):
        rows = slice(i * seq, (i + 1) * seq)
        o_ref[rows, :] = _dot(cs, jnp.concatenate([gc[rows], gs[rows]], axis=0))


def _dft_constants(seq):
    k = np.arange(FNET_GW)
    ang = 2.0 * np.pi * np.outer(k, k) / FNET_GW
    eye = np.eye(FNET_DIM // FNET_GW)
    cbd = np.kron(eye, np.cos(ang)) / np.sqrt(FNET_GW)
    sbd = np.kron(eye, np.sin(ang)) / np.sqrt(FNET_GW)
    s = np.arange(seq)
    angs = 2.0 * np.pi * np.outer(s, s) / seq
    cs = np.concatenate([np.cos(angs), -np.sin(angs)], axis=1) / np.sqrt(seq)
    return (jnp.asarray(cbd, F32), jnp.asarray(sbd, F32), jnp.asarray(cs, F32))


def _fnet(fin, nb, seq):
    cbd, sbd, cs = _dft_constants(seq)
    return pl.pallas_call(
        functools.partial(_fnet_kernel, seq),
        out_shape=jax.ShapeDtypeStruct((nb * seq, FNET_DIM), F32), grid=(nb * seq // _FNET_ROWS,),
        in_specs=[pl.BlockSpec((_FNET_ROWS, FNET_DIM), lambda b: (b, 0)),
                  _const_spec((FNET_DIM, FNET_DIM)), _const_spec((FNET_DIM, FNET_DIM)),
                  _const_spec((seq, 2 * seq))],
        out_specs=pl.BlockSpec((_FNET_ROWS, FNET_DIM), lambda b: (b, 0)),
        compiler_params=_cparams(("arbitrary",), VMEM_LIMIT),
        name="fnet",
    )(fin, cbd, sbd, cs)


def _out_proj_kernel(x_ref, ys_ref, ym_ref, yf_ref, mod_ref, w_ref, g_ref, b_ref, wr_ref,
                     x1_ref, u2p_ref, lg_ref):
    mix = (_dot(ys_ref[...].astype(BF16), w_ref[0:SSD_DIM, :])
           + _dot(ym_ref[...].astype(BF16), w_ref[SSD_DIM:SSD_DIM + MLA_DIM, :])
           + _dot(yf_ref[...].astype(BF16), w_ref[SSD_DIM + MLA_DIM:, :]))
    x1 = _ln(DN_ALPHA * x_ref[...] + mod_ref[2:3, :] * mix) * g_ref[...] + b_ref[...]
    x1_ref[...] = x1
    u2 = _ln(x1) * (1.0 + mod_ref[4:5, :]) + mod_ref[3:4, :]
    lg_ref[...] = _dot_nt(wr_ref[...], u2.astype(BF16))
    half = D_MODEL // 2
    words = pltpu.pack_elementwise([u2[:, :half], u2[:, half:]], packed_dtype=BF16)
    for j in range(PACK_SLAB):
        u2p_ref[pl.ds(j, _TM, stride=PACK_SLAB), :] = words[:, j * LANES:(j + 1) * LANES]


def _out_proj(x, yssd, ymla, yfft, mod_all, mod_rows, lp, seq, layer):
    t = x.shape[0]
    return pl.pallas_call(
        _out_proj_kernel,
        out_shape=[jax.ShapeDtypeStruct((t, D_MODEL), F32),
                   jax.ShapeDtypeStruct((t * PACK_SLAB, LANES), U32),
                   jax.ShapeDtypeStruct((N_EXPERTS, t), F32)],
        grid=(t // _TM,),
        in_specs=[_row_spec(D_MODEL), _row_spec(SSD_DIM), _row_spec(MLA_DIM), _row_spec(FNET_DIM),
                  _mod_spec(mod_rows, t, seq, layer), _layer_spec((D_MODEL, D_MODEL), layer),
                  _layer_spec((1, D_MODEL), layer), _layer_spec((1, D_MODEL), layer),
                  _layer_spec((N_EXPERTS, D_MODEL), layer)],
        out_specs=[_row_spec(D_MODEL),
                   pl.BlockSpec((_TM * PACK_SLAB, LANES), lambda i: (i, 0)),
                   pl.BlockSpec((N_EXPERTS, _TM), lambda i: (0, i))],
        compiler_params=_cparams(("arbitrary",), VMEM_LIMIT),
        name="out_proj",
    )(x, yssd, ymla, yfft, mod_all, lp["w_out"], lp["ln1_g"], lp["ln1_b"], lp["wrt"])


_RANK_SPLIT = 64.0


def _cumsum_tokens(x, triu, below):
    within = _dot(x.astype(BF16), triu)
    totals = jnp.broadcast_to(within[:, LANES - 1:LANES], within.shape)
    return within + _dot(below, totals.astype(BF16))


def _route_kernel(nblk, cap, lg_ref, idx_ref, aff_ref, p_ref, sel_ref):
    lg = lg_ref[...]
    e = jnp.exp(lg - jnp.max(lg, axis=0, keepdims=True))
    aff = e / jnp.sum(e, axis=0, keepdims=True)
    aff_ref[...] = aff
    capf = float(cap)

    def count(mask):
        s = jnp.sum(jnp.where(mask, 1.0, 0.0), axis=2, keepdims=True)
        return jnp.sum(s, axis=1, keepdims=True)

    def search(i, lo):
        cand = lo | jnp.left_shift(jnp.int32(1), 30 - i)
        return jnp.where(count(aff >= pltpu.bitcast(cand, F32)) >= capf, cand, lo)

    thr_bits = lax.fori_loop(0, 31, search, jnp.zeros((N_EXPERTS, 1, 1), I32))
    thr = pltpu.bitcast(thr_bits, F32)
    need = capf - count(aff > thr)

    rows = N_EXPERTS * nblk
    ri = lax.broadcasted_iota(I32, (LANES, LANES), 0)
    ci = lax.broadcasted_iota(I32, (LANES, LANES), 1)
    triu = jnp.where(ri <= ci, 1.0, 0.0).astype(BF16)
    rr = lax.broadcasted_iota(I32, (rows, rows), 0)
    rc = lax.broadcasted_iota(I32, (rows, rows), 1)
    below = jnp.where((rc < rr) & (rc >= (rr // nblk) * nblk), 1.0, 0.0).astype(BF16)
    flat = lambda v: v.reshape(rows, LANES)
    gt = flat(jnp.where(aff > thr, 1.0, 0.0))
    eq = flat(jnp.where(aff == thr, 1.0, 0.0))
    need_rows = flat(jnp.broadcast_to(need, aff.shape))
    eq_rank = _cumsum_tokens(eq, triu, below) - eq
    sel = gt + eq * jnp.where(eq_rank < need_rows, 1.0, 0.0)
    rank = _cumsum_tokens(sel, triu, below)
    sel_ref[...] = sel.reshape(N_EXPERTS, nblk, LANES)
    p_ref[...] = rank.reshape(N_EXPERTS, nblk, LANES)

    ones8 = jnp.ones((SUBLANES, LANES), BF16)
    bi = lax.broadcasted_iota(I32, (nblk, nblk), 0)
    bj = lax.broadcasted_iota(I32, (nblk, nblk), 1)
    triu_b = jnp.where(bi <= bj, 1.0, 0.0).astype(BF16)
    blk_ids = lax.broadcasted_iota(I32, (SUBLANES, nblk), 1).astype(F32).astype(BF16)
    r_col = lax.broadcasted_iota(I32, (cap, 1), 0).astype(F32)

    def per_expert(ex, carry):
        pe = p_ref[ex]
        tot_row = _dot_nt(ones8, sel_ref[ex].astype(BF16))[0:1, :]
        upto = _dot(jnp.broadcast_to(tot_row, (SUBLANES, nblk)).astype(BF16), triu_b)[0:1, :]
        onehot = jnp.where((upto - tot_row <= r_col) & (r_col < upto), 1.0, 0.0).astype(BF16)
        p_hi = jnp.floor(pe * (1.0 / _RANK_SPLIT))
        p_lo = pe - p_hi * _RANK_SPLIT
        ranks = _dot(onehot, p_hi.astype(BF16)) * _RANK_SPLIT + _dot(onehot, p_lo.astype(BF16))
        inside = jnp.where(ranks <= r_col, 1.0, 0.0).astype(BF16)
        cnt = _dot_nt(ones8, inside)[0:1, :] + float(LANES) * _dot_nt(blk_ids, onehot)[0:1, :]
        idx_ref[ex] = cnt.astype(I32)
        return carry

    lax.fori_loop(0, N_EXPERTS, per_expert, 0)


def _route(logits, cap):
    t = logits.shape[1]
    nblk = t // LANES
    idx, aff = pl.pallas_call(
        functools.partial(_route_kernel, nblk, cap),
        out_shape=[jax.ShapeDtypeStruct((N_EXPERTS, 1, cap), I32),
                   jax.ShapeDtypeStruct((N_EXPERTS, nblk, LANES), F32)],
        scratch_shapes=[pltpu.VMEM((N_EXPERTS, nblk, LANES), F32),
                        pltpu.VMEM((N_EXPERTS, nblk, LANES), F32)],
        compiler_params=_cparams(None, VMEM_LIMIT),
        name="route",
    )(logits.reshape(N_EXPERTS, nblk, LANES))
    return idx.reshape(N_EXPERTS, cap), aff.reshape(N_EXPERTS, t)


_UP_TF = 512
_UNROLL = 16


def _gather_rows(tile_ref, streams, ex, step, n_steps):
    row0 = 0
    for src_ref, idx_ref, cap in streams:
        cnt = cap // n_steps
        for u in range(cnt):
            r = step * cnt + u
            src = pl.multiple_of(idx_ref[ex * cap + r] * PACK_SLAB, PACK_SLAB)
            dst = pl.multiple_of((row0 + r) * PACK_SLAB, PACK_SLAB)
            tile_ref[pl.ds(dst, PACK_SLAB), :] = src_ref[pl.ds(src, PACK_SLAB), :]
        row0 += cap


def _moe_up_kernel(caps, ua_ref, ub_ref, ia_ref, ib_ref, wg_ref, wu_ref, h_ref, tile_ref, xe_ref):
    ex = pl.program_id(0)
    ff = pl.program_id(1)
    n_steps = EXPERT_FF // _UP_TF
    m = sum(caps)
    streams = ((ua_ref, ia_ref, caps[0]), (ub_ref, ib_ref, caps[1]))

    @pl.when((ex == 0) & (ff == 0))
    def _first_rows():
        def body(i, carry):
            row0 = 0
            for src_ref, idx_ref, cap in streams:
                @pl.when(i < cap // _UNROLL)
                def _(src_ref=src_ref, idx_ref=idx_ref, row0=row0):
                    for u in range(_UNROLL):
                        r = i * _UNROLL + u
                        src = pl.multiple_of(idx_ref[r] * PACK_SLAB, PACK_SLAB)
                        dst = pl.multiple_of((row0 + r) * PACK_SLAB, PACK_SLAB)
                        tile_ref[pl.ds(dst, PACK_SLAB), :] = src_ref[pl.ds(src, PACK_SLAB), :]
                row0 += cap
            return carry

        lax.fori_loop(0, max(caps) // _UNROLL, body, 0)

    @pl.when(ff == 0)
    def _unpack():
        half = D_MODEL // 2
        for j in range(PACK_SLAB):
            w = tile_ref[pl.ds(j, m, stride=PACK_SLAB), :]
            for k in range(2):
                v = pltpu.unpack_elementwise(w, index=k, packed_dtype=BF16, unpacked_dtype=F32)
                lo = k * half + j * LANES
                xe_ref[:, lo:lo + LANES] = v.astype(BF16)

    x = xe_ref[...]
    g = _dot(x, wg_ref[...].astype(BF16))
    u = _dot(x, wu_ref[...].astype(BF16))
    h_ref[...] = (_silu(g) * u).astype(BF16)
    nxt = jnp.minimum(ex + 1, N_EXPERTS - 1)
    _gather_rows(tile_ref, streams, nxt, ff, n_steps)


def _moe_up(u2p_a, u2p_b, idx_a, idx_b, w_gate, w_up, layer):
    caps = (idx_a.shape[1], idx_b.shape[1])
    m = sum(caps)
    smem = pl.BlockSpec(memory_space=pltpu.SMEM)
    w_spec = pl.BlockSpec((None, None, D_MODEL, _UP_TF), lambda e, f: (layer, e, 0, f))
    return pl.pallas_call(
        functools.partial(_moe_up_kernel, caps),
        out_shape=jax.ShapeDtypeStruct((N_EXPERTS, m, EXPERT_FF), BF16),
        grid=(N_EXPERTS, EXPERT_FF // _UP_TF),
        in_specs=[pl.BlockSpec(u2p_a.shape, lambda e, f: (0, 0), pipeline_mode=pl.Buffered(1)),
                  pl.BlockSpec(u2p_b.shape, lambda e, f: (0, 0), pipeline_mode=pl.Buffered(1)),
                  smem, smem, w_spec, w_spec],
        out_specs=pl.BlockSpec((None, m, _UP_TF), lambda e, f: (e, 0, f)),
        scratch_shapes=[pltpu.VMEM((m * PACK_SLAB, LANES), U32), pltpu.VMEM((m, D_MODEL), BF16)],
        compiler_params=_cparams(("arbitrary", "arbitrary"), VMEM_LIMIT),
        name="moe_up",
    )(u2p_a, u2p_b, idx_a.reshape(-1), idx_b.reshape(-1), w_gate, w_up)


_DOWN_TN = 512
_RMW = 16


def _scatter_rows(acc_ref, slab_ref, idx_ref, aff_ref, idx0, src0, scale, rows):
    for b0 in range(0, rows, _RMW):
        dsts, vals = [], []
        for u in range(b0, b0 + _RMW):
            tok = idx_ref[idx0 + u]
            dst = pl.multiple_of(tok * ROW_SLAB, ROW_SLAB)
            src = pl.multiple_of(src0 + u * ROW_SLAB, ROW_SLAB)
            vals.append(acc_ref[pl.ds(dst, ROW_SLAB), :]
                        + (aff_ref[tok] * scale) * slab_ref[pl.ds(src, ROW_SLAB), :])
            dsts.append(dst)
        for dst, val in zip(dsts, vals):
            acc_ref[pl.ds(dst, ROW_SLAB), :] = val


def _moe_down_kernel(cap, h_ref, wd_ref, idx_ref, affp_ref, aff_ref, out_ref, acc_ref, slab_ref):
    ex = pl.program_id(0)
    nn = pl.program_id(1)
    n_steps = D_MODEL // _DOWN_TN
    n_half = _DOWN_TN // LANES
    part = cap // n_steps
    par = ex % 2

    @pl.when((ex == 0) & (nn == 0))
    def _zero():
        acc_ref[...] = jnp.zeros(acc_ref.shape, F32)
        slab_ref[...] = jnp.zeros(slab_ref.shape, F32)

    ye = _dot(h_ref[...], wd_ref[...].astype(BF16))
    prev = jnp.maximum(ex - 1, 0)
    live = jnp.where(ex > 0, 1.0, 0.0)
    row0 = nn * part
    _scatter_rows(acc_ref, slab_ref.at[1 - par], idx_ref, affp_ref, prev * cap + row0,
                  row0 * ROW_SLAB, live, part)
    dst = slab_ref.at[par]
    for j in range(n_half):
        dst[pl.ds(nn * n_half + j, cap, stride=ROW_SLAB), :] = ye[:, j * LANES:(j + 1) * LANES]

    @pl.when((ex == N_EXPERTS - 1) & (nn == n_steps - 1))
    def _tail():
        def body(i, carry):
            _scatter_rows(acc_ref, slab_ref.at[par], idx_ref, aff_ref, ex * cap + i * _RMW,
                          i * (_RMW * ROW_SLAB), 1.0, _RMW)
            return carry

        lax.fori_loop(0, cap // _RMW, body, 0)
        pltpu.sync_copy(acc_ref, out_ref)


def _moe_down(h, w_down, idx, aff, row_block, layer):
    cap = idx.shape[1]
    t = aff.shape[1]
    aff3 = aff[:, None, :]
    gate_spec = lambda shift: pl.BlockSpec(
        (None, None, t), lambda e, n: (jnp.maximum(e - shift, 0), 0, 0), memory_space=pltpu.SMEM)
    return pl.pallas_call(
        functools.partial(_moe_down_kernel, cap),
        out_shape=jax.ShapeDtypeStruct((t * ROW_SLAB, LANES), F32),
        grid=(N_EXPERTS, D_MODEL // _DOWN_TN),
        in_specs=[pl.BlockSpec((None, cap, EXPERT_FF), lambda e, n: (e, row_block, 0)),
                  pl.BlockSpec((None, None, EXPERT_FF, _DOWN_TN), lambda e, n: (layer, e, 0, n)),
                  pl.BlockSpec(memory_space=pltpu.SMEM), gate_spec(1), gate_spec(0)],
        out_specs=pl.BlockSpec(memory_space=pl.ANY),
        scratch_shapes=[pltpu.VMEM((t * ROW_SLAB, LANES), F32),
                        pltpu.VMEM((2, cap * ROW_SLAB, LANES), F32)],
        compiler_params=_cparams(("arbitrary", "arbitrary"), VMEM_LIMIT),
        name="moe_down",
    )(h, w_down, idx.reshape(-1), aff3, aff3)


def _swap_pairs(w):
    s = w.shape
    return jnp.flip(w.reshape(s[:-1] + (s[-1] // 2, 2)), axis=-1).reshape(s)


def _pack_w_in(w_in):
    o_dt, o_cq, o_kr, o_fin = 1280, 1296, 1680, 1712
    kr = w_in[:, :, o_kr:o_fin]
    pad = jnp.zeros(w_in.shape[:2] + (_IN_KR4 - _IN_SMALL - 2 * SSD_HEADS,), w_in.dtype)
    parts = [w_in[:, :, :o_dt], w_in[:, :, o_cq:o_kr], w_in[:, :, o_fin:], w_in[:, :, o_dt:o_cq], pad,
             jnp.tile(kr, (1, 1, MLA_HEADS)), jnp.tile(_swap_pairs(kr), (1, 1, MLA_HEADS))]
    return jnp.concatenate(parts, axis=2).astype(BF16)


def _pack_w_uq(w_uq):
    w = w_uq.reshape(w_uq.shape[:2] + (MLA_HEADS, QK_NOPE + ROPE_DIM))
    flat = lambda v: v.reshape(v.shape[:2] + (-1,))
    rope = w[..., QK_NOPE:]
    return jnp.concatenate([flat(w[..., :QK_NOPE]), flat(rope), flat(_swap_pairs(rope))],
                           axis=2).astype(BF16)


def _pack_w_ukv(w_ukv):
    w = w_ukv.reshape(w_ukv.shape[:2] + (MLA_HEADS, QK_NOPE + V_DIM))
    flat = lambda v: v.reshape(v.shape[:2] + (-1,))
    return jnp.concatenate([flat(w[..., :QK_NOPE]), flat(w[..., QK_NOPE:])], axis=2).astype(BF16)


def _rope_tables(seq):
    rows = seq // GRID_W
    r, col = jnp.meshgrid(jnp.arange(rows, dtype=F32), jnp.arange(GRID_W, dtype=F32), indexing="ij")
    pairs = ROPE_DIM // 4
    inv = ROPE_BASE ** (-jnp.arange(pairs, dtype=F32) / pairs)
    ang = jnp.concatenate([r.reshape(-1, 1) * inv, col.reshape(-1, 1) * inv], axis=-1)
    cos = jnp.repeat(jnp.cos(ang), 2, axis=1)
    sin = jnp.repeat(jnp.sin(ang), 2, axis=1) * jnp.tile(jnp.array([-1.0, 1.0], F32), ROPE_DIM // 2)
    return jnp.tile(cos, (1, MLA_HEADS)), jnp.tile(sin, (1, MLA_HEADS))


def _pad_lanes(v):
    return jnp.pad(v, ((0, 0), (0, LANES - v.shape[1])))[:, None, :]


def kernel(x_prompt, x_sample, cache_ckv, cache_krope, state_ssm, c, c_ctx,
           w_in, conv_w, conv_b, dt_bias, a_log, d_skip, ssd_norm_g,
           q_norm_g, w_uq, kv_norm_g, w_ukv, w_out, w_mod, b_mod,
           ln1_g, ln1_b, ln2_g, ln2_b, w_router, w_gate, w_up, w_down):
    nb_c, seq_c, _ = x_prompt.shape
    nb_l, seq_l, _ = x_sample.shape
    t_c, t_l = nb_c * seq_c, nb_l * seq_l
    caps = (CAPACITY_FACTOR * t_c // N_EXPERTS, CAPACITY_FACTOR * t_l // N_EXPERTS)

    w_in_p = _pack_w_in(w_in)
    wq = _pack_w_uq(w_uq)
    wkv = _pack_w_ukv(w_ukv)
    row = lambda v: v[:, None, :]
    lp = dict(conv_w=conv_w, conv_b=row(conv_b), dt_bias=_pad_lanes(dt_bias.reshape(DEPTH, -1)),
              a_log=_pad_lanes(a_log.reshape(DEPTH, -1)),
              d_skip=row(jnp.repeat(d_skip, SSD_HEADDIM, axis=1)), ssd_norm_g=row(ssd_norm_g),
              q_norm_g=row(q_norm_g), wq=wq, kv_norm_g=row(kv_norm_g), wkv=wkv,
              w_out=w_out.astype(BF16), ln1_g=row(ln1_g), ln1_b=row(ln1_b),
              wrt=jnp.swapaxes(w_router, 1, 2).astype(BF16))
    ln2 = (row(ln2_g), row(ln2_b))
    cache_kr4 = jnp.tile(cache_krope, (1, 1, 1, MLA_HEADS))
    cos4, sin4 = _rope_tables(seq_l)

    cond = jnp.zeros((SUBLANES, D_MODEL), F32).at[0].set(c_ctx).at[1:1 + nb_l].set(c)
    mod_all = _modulation(cond, w_mod, b_mod).reshape(DEPTH, SUBLANES, 6, D_MODEL)

    streams = [
        dict(x=x_prompt.reshape(t_c, D_MODEL), nb=nb_c, seq=seq_c, cap=caps[0], mod_rows=(0, 1)),
        dict(x=x_sample.reshape(t_l, D_MODEL), nb=nb_l, seq=seq_l, cap=caps[1], mod_rows=(1, nb_l)),
    ]
    prev = [None, None]
    ckvs, krs, ssms = [], [], []
    for l in range(DEPTH):
        routed = []
        for si, st in enumerate(streams):
            nb, seq, mod_rows = st["nb"], st["seq"], st["mod_rows"]
            x, z, xbc, cq, ckv, fin, small, kr = _in_proj(
                st["x"], prev[si], mod_all, mod_rows, w_in_p, seq, l)
            if si == 0:
                yssd, ssm = _ssd(z, xbc, small, lp, nb, seq, None, l)
                ymla, ckv_n, kr_o = _mla(cq, ckv, kr, lp, nb, seq, None, l)
                ckvs.append(ckv_n)
                krs.append(kr_o)
                ssms.append(ssm)
            else:
                yssd, _ = _ssd(z, xbc, small, lp, nb, seq, state_ssm, l)
                (ymla,) = _mla(cq, ckv, kr, lp, nb, seq, (cache_ckv, cache_kr4, cos4, sin4), l)
            yfft = _fnet(fin, nb, seq)
            x1, u2p, logits = _out_proj(x, yssd, ymla, yfft, mod_all, mod_rows, lp, seq, l)
            idx, aff = _route(logits, st["cap"])
            st["x"] = x1
            routed.append((u2p, idx, aff))
        h = _moe_up(routed[0][0], routed[1][0], routed[0][1], routed[1][1], w_gate, w_up, l)
        for si in range(2):
            moe = _moe_down(h, w_down, routed[si][1], routed[si][2], 0 if si == 0 else 2, l)
            prev[si] = (moe,) + ln2
    last = DEPTH - 1
    y_p, y_s = [_final_norm(st["x"], *prev[si], mod_all, st["mod_rows"], st["seq"], last)
                for si, st in enumerate(streams)]
    y_p, y_s = y_p.reshape(x_prompt.shape), y_s.reshape(x_sample.shape)
    return (y_p, y_s, jnp.stack(ckvs, axis=1), jnp.stack(krs, axis=1), jnp.stack(ssms, axis=1))
```

```python
import functools

import jax
import jax.numpy as jnp
import numpy as np
from jax import lax
from jax.experimental import pallas as pl
from jax.experimental.pallas import tpu as pltpu

F32 = jnp.float32
BF16 = jnp.bfloat16
I32 = jnp.int32
U32 = jnp.uint32

D_MODEL = 1024
DEPTH = 4
GRID_W = 64
SSD_HEADS = 8
SSD_HEADDIM = 64
SSD_DIM = SSD_HEADS * SSD_HEADDIM
SSD_STATE = 64
SSD_CONV = 5
SSD_CHUNK = 128
SSD_CONV_DIM = SSD_DIM + 4 * SSD_STATE
MLA_HEADS = 4
Q_LORA = 256
KV_LORA = 128
QK_NOPE = 64
ROPE_DIM = 32
V_DIM = 64
MLA_DIM = MLA_HEADS * V_DIM
MLA_SCALE = (QK_NOPE + ROPE_DIM) ** -0.5
ROPE_BASE = 10000.0
Q_TILE = 256
FNET_GW = 64
FNET_DIM = 256
N_EXPERTS = 16
EXPERT_FF = 1024
CAPACITY_FACTOR = 2
DN_ALPHA = (2 * DEPTH) ** 0.25
EPS = 1e-5
_LOG2E = 1.4426950408889634

LANES = 128
SUBLANES = 8
ROW_SLAB = D_MODEL // LANES
PACK_SLAB = ROW_SLAB // 2
VMEM_LIMIT = 56 * 1024 * 1024

_IN_Z = 0
_IN_XBC = 512
_IN_CQ = 1280
_IN_CKV = 1536
_IN_FIN = 1664
_IN_SMALL = 1920
_IN_KR4 = 2048
IN_PAD = 2304
_IN_WIDTHS = (512, 768, 256, 128, 256, 128, 256)


def _cparams(sem, vmem=None):
    return pltpu.CompilerParams(dimension_semantics=sem, vmem_limit_bytes=vmem)


def _ln(x):
    mu = jnp.mean(x, axis=-1, keepdims=True)
    xc = x - mu
    var = jnp.mean(xc * xc, axis=-1, keepdims=True)
    return xc * lax.rsqrt(var + EPS)


def _rms(x):
    return x * lax.rsqrt(jnp.mean(x * x, axis=-1, keepdims=True) + EPS)


def _silu(x):
    return x * (1.0 / (1.0 + jnp.exp(-x)))


def _dot(a, b):
    return jnp.dot(a, b, preferred_element_type=F32)


def _dot_nt(a, b):
    return lax.dot_general(a, b, (((1,), (1,)), ((), ())), preferred_element_type=F32)


def _split3(x):
    hi = x.astype(BF16)
    r1 = x - hi.astype(F32)
    mid = r1.astype(BF16)
    lo = (r1 - mid.astype(F32)).astype(BF16)
    return hi, mid, lo


def _const_spec(shape):
    return pl.BlockSpec(shape, lambda *_: (0,) * len(shape))


def _layer_spec(shape, layer):
    return pl.BlockSpec((None,) + shape, lambda *_: (layer,) + (0,) * len(shape))


_MOD_TN = 1536


def _mod_kernel(c_ref, w_ref, b_ref, o_ref):
    a = _silu(c_ref[...]).astype(BF16)
    o_ref[...] = _dot(a, w_ref[...].astype(BF16)) + b_ref[...]


def _modulation(cond, w_mod, b_mod):
    n = 6 * D_MODEL
    return pl.pallas_call(
        _mod_kernel,
        out_shape=jax.ShapeDtypeStruct((DEPTH, SUBLANES, n), F32),
        grid=(DEPTH, n // _MOD_TN),
        in_specs=[
            pl.BlockSpec((SUBLANES, D_MODEL), lambda l, j: (0, 0)),
            pl.BlockSpec((None, D_MODEL, _MOD_TN), lambda l, j: (l, 0, j)),
            pl.BlockSpec((None, 1, _MOD_TN), lambda l, j: (l, 0, j)),
        ],
        out_specs=pl.BlockSpec((None, SUBLANES, _MOD_TN), lambda l, j: (l, 0, j)),
        compiler_params=_cparams(("arbitrary", "arbitrary")),
        name="modulation",
    )(cond, w_mod, b_mod.reshape(DEPTH, 1, n))


_TM = 512


def _read_rows(slab_ref, tm):
    return jnp.concatenate(
        [slab_ref[pl.ds(j, tm, stride=ROW_SLAB), :] for j in range(ROW_SLAB)], axis=-1)


def _post_norm2(x_ref, moe_ref, pmod_ref, g_ref, b_ref):
    moe = _read_rows(moe_ref, _TM)
    return _ln(DN_ALPHA * x_ref[...] + pmod_ref[5:6, :] * moe) * g_ref[...] + b_ref[...]


def _in_proj_kernel(first, *refs):
    if first:
        x_ref, mod_ref, w_ref = refs[:3]
        outs = refs[3:]
        x = x_ref[...]
    else:
        x_ref, moe_ref, pmod_ref, g_ref, b_ref, mod_ref, w_ref, xo_ref = refs[:8]
        outs = refs[8:]
        x = _post_norm2(x_ref, moe_ref, pmod_ref, g_ref, b_ref)
        xo_ref[...] = x
    u = _ln(x) * (1.0 + mod_ref[1:2, :]) + mod_ref[0:1, :]
    p = _dot(u.astype(BF16), w_ref[...])
    off = 0
    for o_ref, w in zip(outs, _IN_WIDTHS):
        o_ref[...] = p[:, off:off + w]
        off += w


def _mod_spec(mod_rows, t, seq, layer):
    row0, count = mod_rows
    per = seq // _TM if count > 1 else t // _TM
    return pl.BlockSpec((None, None, 6, D_MODEL), lambda i: (layer, row0 + i // per, 0, 0))


def _row_spec(w):
    return pl.BlockSpec((_TM, w), lambda i: (i, 0))


_SLAB_SPEC = pl.BlockSpec((_TM * ROW_SLAB, LANES), lambda i: (i, 0))


def _in_proj(x, prev, mod_all, mod_rows, w_in_p, seq, layer):
    t = x.shape[0]
    mod_spec = _mod_spec(mod_rows, t, seq, layer)
    outs = [jax.ShapeDtypeStruct((t, w), F32) for w in _IN_WIDTHS]
    out_specs = [_row_spec(w) for w in _IN_WIDTHS]
    w_spec = _layer_spec((D_MODEL, IN_PAD), layer)
    if prev is None:
        ins = [x, mod_all, w_in_p]
        in_specs = [_row_spec(D_MODEL), mod_spec, w_spec]
    else:
        moe_slab, g, b = prev
        ins = [x, moe_slab, mod_all, g, b, mod_all, w_in_p]
        in_specs = [_row_spec(D_MODEL), _SLAB_SPEC, _mod_spec(mod_rows, t, seq, layer - 1),
                    _layer_spec((1, D_MODEL), layer - 1), _layer_spec((1, D_MODEL), layer - 1),
                    mod_spec, w_spec]
        outs = [jax.ShapeDtypeStruct((t, D_MODEL), F32)] + outs
        out_specs = [_row_spec(D_MODEL)] + out_specs
    res = pl.pallas_call(
        functools.partial(_in_proj_kernel, prev is None),
        out_shape=outs, grid=(t // _TM,), in_specs=in_specs, out_specs=out_specs,
        compiler_params=_cparams(("arbitrary",), VMEM_LIMIT),
        name="in_proj",
    )(*ins)
    if prev is None:
        return (x,) + tuple(res)
    return tuple(res)


def _final_norm_kernel(x_ref, moe_ref, pmod_ref, g_ref, b_ref, o_ref):
    o_ref[...] = _post_norm2(x_ref, moe_ref, pmod_ref, g_ref, b_ref)


def _final_norm(x, moe_slab, g, b, mod_all, mod_rows, seq, layer):
    t = x.shape[0]
    return pl.pallas_call(
        _final_norm_kernel,
        out_shape=jax.ShapeDtypeStruct((t, D_MODEL), F32), grid=(t // _TM,),
        in_specs=[_row_spec(D_MODEL), _SLAB_SPEC, _mod_spec(mod_rows, t, seq, layer),
                  _layer_spec((1, D_MODEL), layer), _layer_spec((1, D_MODEL), layer)],
        out_specs=_row_spec(D_MODEL),
        compiler_params=_cparams(("arbitrary",)),
        name="final_norm",
    )(x, moe_slab, mod_all, g, b)


def _ssd_kernel(has_h0, write_h, seq, *refs):
    refs = list(refs)
    z_ref, xbc_ref, sm_ref, cw_ref, cb_ref, dtb_ref, alog_ref, dsk_ref, ng_ref = refs[:9]
    pos = 9
    if has_h0:
        h0_ref = refs[pos]
    pos += 1
    y_ref = refs[pos]
    pos += 1
    if write_h:
        hf_ref = refs[pos]
        pos += 1
    pad_ref, xs_ref, cm_ref, gm_ref, bt_ref, acol_ref, rows_ref, st_ref = refs[pos:]
    ck = SSD_CHUNK
    nc = seq // ck
    halo = SUBLANES
    nh = SSD_HEADS
    nh2 = 2 * nh

    pad_ref[0:halo, :] = jnp.zeros((halo, SSD_CONV_DIM), F32)
    pad_ref[halo + seq:2 * halo + seq, :] = jnp.zeros((halo, SSD_CONV_DIM), F32)
    pad_ref[halo:halo + seq, :] = xbc_ref[...]
    ri = lax.broadcasted_iota(I32, (ck, ck), 0)
    ci = lax.broadcasted_iota(I32, (ck, ck), 1)
    triu = jnp.where(ri <= ci, 1.0, 0.0).astype(BF16)
    lo_half = ci < SSD_STATE
    lo64 = lax.broadcasted_iota(I32, (SSD_STATE, LANES), 1) < SSD_STATE
    zeros_half = jnp.zeros((SSD_STATE, LANES), BF16)
    fwd_rows = lax.broadcasted_iota(I32, (nh2, ck), 0) < nh

    def prepare_chunk(c, carry):
        cs = pl.multiple_of(c * ck, ck)
        rows = pl.ds(cs, ck)
        cw_blk = 2 * LANES
        for cb in range(SSD_CONV_DIM // cw_blk):
            cols = slice(cb * cw_blk, (cb + 1) * cw_blk)
            win = pad_ref[pl.ds(cs, ck + 2 * halo), cols]
            acc = jnp.zeros((ck, cw_blk), F32) + cb_ref[:, cols]
            for k in range(SSD_CONV):
                s0 = halo + k - (SSD_CONV - 1) // 2
                acc = acc + win[s0:s0 + ck, :] * cw_ref[k:k + 1, cols]
            act = _silu(acc)
            if cb * cw_blk < SSD_DIM:
                xs_ref[rows, cols] = act
                y_ref[rows, cols] = act * dsk_ref[:, cols]
        bm = act[:, :LANES]
        cm = act[:, LANES:]
        cm_ref[rows, :] = cm
        bt_ref[rows, :] = bm.T
        cg2 = jnp.concatenate([jnp.where(lo_half, cm, 0.0), jnp.where(lo_half, 0.0, cm)], axis=0)
        gm2 = _dot_nt(cg2.astype(BF16), bm.astype(BF16))
        gm_ref[rows, 0:LANES] = gm2[:ck]
        gm_ref[rows, LANES:] = gm2[ck:]

        xr = sm_ref[rows, :] + dtb_ref[...]
        dt = jnp.maximum(xr, 0.0) + jnp.log(1.0 + jnp.exp(-jnp.abs(xr)))
        ad = jnp.where(ci < nh2, dt * -jnp.exp(alog_ref[...]),
                       jnp.where(ci < 2 * nh2, pltpu.roll(dt, nh2, axis=1), 0.0))
        adt = ad.T
        at16 = adt[0:nh2, :]
        dt16 = adt[nh2:2 * nh2, :]
        csum = _dot(jnp.concatenate(_split3(at16), axis=0), triu)
        arow = csum[0:nh2] + csum[nh2:2 * nh2] + csum[2 * nh2:3 * nh2]
        acol = jnp.concatenate([arow, jnp.zeros((ck - nh2, ck), F32)], axis=0).T
        alast = jnp.broadcast_to(arow[:, ck - 1:ck], (nh2, ck)) * _LOG2E
        arow_s = jnp.where(fwd_rows, arow, arow - at16) * _LOG2E
        ldt = jnp.log2(dt16)
        rowarg = jnp.where(fwd_rows, arow_s - ldt, arow_s + ldt)
        wrow = jnp.exp2(jnp.where(fwd_rows, alast - arow_s, arow_s))
        acol_ref[rows, :] = jnp.where(ci < nh, acol, acol - ad) * _LOG2E
        rows_ref[pl.ds(pl.multiple_of(c * (4 * nh2), 4 * nh2), 4 * nh2), :] = jnp.concatenate(
            [rowarg, wrow * dt16, jnp.exp2(alast), alast], axis=0)
        return carry

    lax.fori_loop(0, nc, prepare_chunk, 0)

    if has_h0:
        for d in range(2):
            h0 = h0_ref[d].reshape(SSD_DIM, SSD_STATE)
            h0 = jnp.concatenate([h0, jnp.zeros((SSD_DIM, LANES - SSD_STATE), F32)], axis=1)
            st_ref[d] = h0.T[:SSD_STATE, :]
    else:
        st_ref[...] = jnp.zeros(st_ref.shape, F32)

    def chunk_dir(d, c):
        cs = pl.multiple_of(c * ck, ck)
        acol = acol_ref[pl.ds(cs, ck), :]
        rws = rows_ref[pl.ds(pl.multiple_of(c * (4 * nh2), 4 * nh2), 4 * nh2), :]
        cm = cm_ref[pl.ds(cs, ck), :]
        cg = [jnp.where(lo_half, cm, 0.0), jnp.where(lo_half, 0.0, cm)]
        for q in range(nh // 2):
            g = q // 2
            gmat = gm_ref[pl.ds(cs, ck), g * LANES:(g + 1) * LANES]
            btg = bt_ref[pl.ds(cs + g * SSD_STATE, SSD_STATE), :]
            lhs_rows = []
            decl = []
            for hh in range(2):
                hi = d * nh + 2 * q + hh
                acol_b = jnp.broadcast_to(acol[:, hi:hi + 1], (ck, ck))
                rowarg = rws[hi:hi + 1, :]
                if d == 0:
                    lmat = jnp.where(ri >= ci, jnp.exp2(acol_b - rowarg), 0.0)
                    ecol = jnp.exp2(acol_b)
                else:
                    lmat = jnp.where(ci >= ri, jnp.exp2(rowarg - acol_b), 0.0)
                    ecol = jnp.exp2(rws[3 * nh2 + hi:3 * nh2 + hi + 1, :] - acol_b)
                mm = (gmat * lmat).astype(BF16)
                cs_h = (cg[g] * ecol).astype(BF16)
                bw = (btg * rws[nh2 + hi:nh2 + hi + 1, :]).astype(BF16)
                lhs_rows.append(jnp.concatenate([mm, cs_h], axis=1))
                lhs_rows.append(jnp.concatenate([bw, zeros_half], axis=1))
                decl.append(rws[2 * nh2 + hi:2 * nh2 + hi + 1, :])
            lhs = jnp.concatenate(lhs_rows, axis=0)
            cols = slice(q * LANES, (q + 1) * LANES)
            st_pair = st_ref[d, :, cols]
            stb = st_pair.astype(BF16)
            w_st = jnp.concatenate([stb, zeros_half] if g == 0 else [zeros_half, stb], axis=0)
            xs_pair = xs_ref[pl.ds(cs, ck), cols].astype(BF16)
            out = _dot(lhs, jnp.concatenate([xs_pair, w_st], axis=0))
            r2 = ck + SSD_STATE
            y_pair = jnp.where(lo_half, out[0:ck], out[r2:r2 + ck])
            st_new = jnp.where(lo64, out[ck:r2], out[r2 + ck:2 * r2])
            dec = jnp.where(lo64, decl[0], decl[1])
            st_ref[d, :, cols] = st_pair * dec + st_new
            y_ref[pl.ds(cs, ck), cols] += y_pair

    def body(c, carry):
        chunk_dir(0, c)
        chunk_dir(1, nc - 1 - c)
        return carry

    lax.fori_loop(0, nc, body, 0)

    def gate_chunk(c, carry):
        rows = pl.ds(pl.multiple_of(c * ck, ck), ck)
        y = y_ref[rows, :] * _silu(z_ref[rows, :])
        y_ref[rows, :] = _rms(y) * ng_ref[...]
        return carry

    lax.fori_loop(0, nc, gate_chunk, 0)

    if write_h:
        for d in range(2):
            st = jnp.concatenate([st_ref[d], jnp.zeros((LANES - SSD_STATE, SSD_DIM), F32)], axis=0)
            hf_ref[d] = st.T[:, :SSD_STATE].reshape(SSD_HEADS, SSD_HEADDIM, SSD_STATE)


def _ssd(z, xbc, small, lp, nb, seq, h0, hist, layer):
    has_h0 = h0 is not None
    write_h = not has_h0
    seq_spec = lambda w: pl.BlockSpec((seq, w), lambda b: (b, 0))
    ins = [z, xbc, small, lp["conv_w"], lp["conv_b"], lp["dt_bias"], lp["a_log"], lp["d_skip"],
           lp["ssd_norm_g"]]
    in_specs = [seq_spec(SSD_DIM), seq_spec(SSD_CONV_DIM), seq_spec(LANES),
                _layer_spec((SSD_CONV, SSD_CONV_DIM), layer), _layer_spec((1, SSD_CONV_DIM), layer),
                _layer_spec((1, LANES), layer), _layer_spec((1, LANES), layer),
                _layer_spec((1, SSD_DIM), layer), _layer_spec((1, SSD_DIM), layer)]
    hshape = (2, SSD_HEADS, SSD_HEADDIM, SSD_STATE)
    layer_block = pl.BlockSpec((None, None) + hshape, lambda b: (b, layer, 0, 0, 0, 0))
    outs = [jax.ShapeDtypeStruct((nb * seq, SSD_DIM), F32)]
    out_specs = [seq_spec(SSD_DIM)]
    aliases = {}
    if has_h0:
        ins.append(h0)
        in_specs.append(layer_block)
    else:
        ins.append(hist)
        in_specs.append(pl.BlockSpec(memory_space=pl.ANY))
        aliases = {len(ins) - 1: 1}
        outs.append(jax.ShapeDtypeStruct(hist.shape, F32))
        out_specs.append(layer_block)
    res = pl.pallas_call(
        functools.partial(_ssd_kernel, has_h0, write_h, seq),
        out_shape=outs, grid=(nb,), in_specs=in_specs, out_specs=out_specs,
        input_output_aliases=aliases,
        scratch_shapes=[pltpu.VMEM((seq + 2 * SUBLANES, SSD_CONV_DIM), F32),
                        pltpu.VMEM((seq, SSD_DIM), F32),
                        pltpu.VMEM((seq, LANES), F32),
                        pltpu.VMEM((seq, 2 * LANES), F32),
                        pltpu.VMEM((seq, LANES), F32),
                        pltpu.VMEM((seq, LANES), F32),
                        pltpu.VMEM((seq // SSD_CHUNK * 8 * SSD_HEADS, LANES), F32),
                        pltpu.VMEM((2, SSD_STATE, SSD_DIM), F32)],
        compiler_params=_cparams(("arbitrary",), VMEM_LIMIT),
        name="ssd",
    )(*ins)
    return res if write_h else (res[0], None)


def _mla_kernel(has_ctx, seq, *refs):
    refs = list(refs)
    cq_ref, ckv_ref, kr_ref, qg_ref, wq_ref, kvg_ref, wkv_ref = refs[:7]
    pos = 7
    if has_ctx:
        cckv_ref, ckr_ref, cos_ref, sin_ref = refs[pos:pos + 4]
        pos += 4
    else:
        pos += 2
    y_ref = refs[pos]
    pos += 1
    if not has_ctx:
        ckvo_ref, kro_ref = refs[pos:pos + 2]

    qn = (_rms(cq_ref[...]) * qg_ref[...]).astype(BF16)
    qall = _dot(qn, wq_ref[...])
    q_nope = qall[:, :MLA_DIM]
    q_rope = qall[:, MLA_DIM:MLA_DIM + LANES]
    ckv_n = _rms(ckv_ref[...]) * kvg_ref[...]
    kr4 = kr_ref[:, :LANES]
    if has_ctx:
        q_rope = q_rope * cos_ref[...] + qall[:, MLA_DIM + LANES:] * sin_ref[...]
        kr4 = kr4 * cos_ref[...] + kr_ref[:, LANES:] * sin_ref[...]
        ckv_all = jnp.concatenate([cckv_ref[...], ckv_n], axis=0)
        kr_all = jnp.concatenate([ckr_ref[...], kr4], axis=0)
    else:
        ckvo_ref[...] = ckv_n
        kro_ref[...] = kr4[:, :ROPE_DIM]
        ckv_all = ckv_n
        kr_all = kr4
    kv = _dot(ckv_all.astype(BF16), wkv_ref[...])
    krb = kr_all.astype(BF16)
    lane = lax.broadcasted_iota(I32, (Q_TILE, LANES), 1)
    lo_half = lane < QK_NOPE
    for qt in range(seq // Q_TILE):
        rows = slice(qt * Q_TILE, (qt + 1) * Q_TILE)
        qr = q_rope[rows] * MLA_SCALE
        for pair in range(MLA_HEADS // 2):
            cols = slice(pair * LANES, (pair + 1) * LANES)
            kcat = jnp.concatenate([kv[:, cols].astype(BF16), krb], axis=1)
            vb = kv[:, MLA_DIM + pair * LANES:MLA_DIM + (pair + 1) * LANES].astype(BF16)
            qn_pair = q_nope[rows, cols] * MLA_SCALE
            outs = []
            for hh in range(2):
                h = 2 * pair + hh
                qa = jnp.where(lo_half if hh == 0 else jnp.logical_not(lo_half), qn_pair, 0.0)
                qb = jnp.where((lane >= h * ROPE_DIM) & (lane < (h + 1) * ROPE_DIM), qr, 0.0)
                s = _dot_nt(jnp.concatenate([qa, qb], axis=1).astype(BF16), kcat)
                p = jnp.exp(s - jnp.max(s, axis=-1, keepdims=True))
                o = _dot(p.astype(BF16), vb)
                outs.append(o / jnp.sum(p, axis=-1, keepdims=True))
            y_ref[rows, cols] = jnp.where(lo_half, outs[0], outs[1])


def _mla(cq, ckv, kr, lp, nb, seq, ctx, hist, layer):
    has_ctx = ctx is not None
    seq_spec = lambda w: pl.BlockSpec((seq, w), lambda b: (b, 0))
    ins = [cq, ckv, kr, lp["q_norm_g"], lp["wq"], lp["kv_norm_g"], lp["wkv"]]
    in_specs = [seq_spec(Q_LORA), seq_spec(KV_LORA), seq_spec(2 * LANES),
                _layer_spec((1, Q_LORA), layer), _layer_spec((Q_LORA, 2 * MLA_DIM), layer),
                _layer_spec((1, KV_LORA), layer), _layer_spec((KV_LORA, 2 * MLA_DIM), layer)]
    outs = [jax.ShapeDtypeStruct((nb * seq, MLA_DIM), F32)]
    out_specs = [seq_spec(MLA_DIM)]
    if has_ctx:
        cache_ckv, cache_kr4, cos4, sin4 = ctx
        past = cache_ckv.shape[2]
        ins += [cache_ckv, cache_kr4, cos4, sin4]
        in_specs += [pl.BlockSpec((None, None, past, KV_LORA), lambda b: (b, layer, 0, 0)),
                     pl.BlockSpec((None, None, past, LANES), lambda b: (b, layer, 0, 0)),
                     _const_spec((seq, LANES)), _const_spec((seq, LANES))]
    else:
        aliases = {len(ins): 1, len(ins) + 1: 2}
        ins += list(hist)
        in_specs += [pl.BlockSpec(memory_space=pl.ANY)] * 2
        outs += [jax.ShapeDtypeStruct(h.shape, F32) for h in hist]
        out_specs += [pl.BlockSpec((None, None, seq, KV_LORA), lambda b: (b, layer, 0, 0)),
                      pl.BlockSpec((None, None, seq, ROPE_DIM), lambda b: (b, layer, 0, 0))]
    return pl.pallas_call(
        functools.partial(_mla_kernel, has_ctx, seq),
        out_shape=outs, grid=(nb,), in_specs=in_specs, out_specs=out_specs,
        input_output_aliases={} if has_ctx else aliases,
        compiler_params=_cparams(("arbitrary",), VMEM_LIMIT),
        name="mla",
    )(*ins)


_FNET_ROWS = 1024


def _fnet_kernel(seq, g_ref, cbd_ref, sbd_ref, cs_ref, o_ref):
    gb = g_ref[...].astype(BF16)
    gc = _dot(gb, cbd_ref[...].astype(BF16)).astype(BF16)
    gs = _dot(gb, sbd_ref[...].astype(BF16)).astype(BF16)
    cs = cs_ref[...].astype(BF16)
    for i in range(_FNET_ROWS // seq):
        rows = slice(i * seq, (i + 1) * seq)
        o_ref[rows, :] = _dot(cs, jnp.concatenate([gc[rows], gs[rows]], axis=0))


def _dft_constants(seq):
    k = np.arange(FNET_GW)
    ang = 2.0 * np.pi * np.outer(k, k) / FNET_GW
    eye = np.eye(FNET_DIM // FNET_GW)
    cbd = np.kron(eye, np.cos(ang)) / np.sqrt(FNET_GW)
    sbd = np.kron(eye, np.sin(ang)) / np.sqrt(FNET_GW)
    s = np.arange(seq)
    angs = 2.0 * np.pi * np.outer(s, s) / seq
    cs = np.concatenate([np.cos(angs), -np.sin(angs)], axis=1) / np.sqrt(seq)
    return (jnp.asarray(cbd, F32), jnp.asarray(sbd, F32), jnp.asarray(cs, F32))


def _fnet(fin, nb, seq):
    cbd, sbd, cs = _dft_constants(seq)
    return pl.pallas_call(
        functools.partial(_fnet_kernel, seq),
        out_shape=jax.ShapeDtypeStruct((nb * seq, FNET_DIM), F32), grid=(nb * seq // _FNET_ROWS,),
        in_specs=[pl.BlockSpec((_FNET_ROWS, FNET_DIM), lambda b: (b, 0)),
                  _const_spec((FNET_DIM, FNET_DIM)), _const_spec((FNET_DIM, FNET_DIM)),
                  _const_spec((seq, 2 * seq))],
        out_specs=pl.BlockSpec((_FNET_ROWS, FNET_DIM), lambda b: (b, 0)),
        compiler_params=_cparams(("arbitrary",), VMEM_LIMIT),
        name="fnet",
    )(fin, cbd, sbd, cs)


def _out_proj_kernel(x_ref, ys_ref, ym_ref, yf_ref, mod_ref, w_ref, g_ref, b_ref, wr_ref,
                     x1_ref, u2p_ref, lg_ref):
    mix = (_dot(ys_ref[...].astype(BF16), w_ref[0:SSD_DIM, :])
           + _dot(ym_ref[...].astype(BF16), w_ref[SSD_DIM:SSD_DIM + MLA_DIM, :])
           + _dot(yf_ref[...].astype(BF16), w_ref[SSD_DIM + MLA_DIM:, :]))
    x1 = _ln(DN_ALPHA * x_ref[...] + mod_ref[2:3, :] * mix) * g_ref[...] + b_ref[...]
    x1_ref[...] = x1
    u2 = _ln(x1) * (1.0 + mod_ref[4:5, :]) + mod_ref[3:4, :]
    lg_ref[...] = _dot_nt(wr_ref[...], u2.astype(BF16))
    half = D_MODEL // 2
    words = pltpu.pack_elementwise([u2[:, :half], u2[:, half:]], packed_dtype=BF16)
    for j in range(PACK_SLAB):
        u2p_ref[pl.ds(j, _TM, stride=PACK_SLAB), :] = words[:, j * LANES:(j + 1) * LANES]


def _out_proj(x, yssd, ymla, yfft, mod_all, mod_rows, lp, seq, layer):
    t = x.shape[0]
    return pl.pallas_call(
        _out_proj_kernel,
        out_shape=[jax.ShapeDtypeStruct((t, D_MODEL), F32),
                   jax.ShapeDtypeStruct((t * PACK_SLAB, LANES), U32),
                   jax.ShapeDtypeStruct((N_EXPERTS, t), F32)],
        grid=(t // _TM,),
        in_specs=[_row_spec(D_MODEL), _row_spec(SSD_DIM), _row_spec(MLA_DIM), _row_spec(FNET_DIM),
                  _mod_spec(mod_rows, t, seq, layer), _layer_spec((D_MODEL, D_MODEL), layer),
                  _layer_spec((1, D_MODEL), layer), _layer_spec((1, D_MODEL), layer),
                  _layer_spec((N_EXPERTS, D_MODEL), layer)],
        out_specs=[_row_spec(D_MODEL),
                   pl.BlockSpec((_TM * PACK_SLAB, LANES), lambda i: (i, 0)),
                   pl.BlockSpec((N_EXPERTS, _TM), lambda i: (0, i))],
        compiler_params=_cparams(("arbitrary",), VMEM_LIMIT),
        name="out_proj",
    )(x, yssd, ymla, yfft, mod_all, lp["w_out"], lp["ln1_g"], lp["ln1_b"], lp["wrt"])


_RANK_SPLIT = 64.0


def _cumsum_tokens(x, triu, below):
    within = _dot(x.astype(BF16), triu)
    totals = jnp.broadcast_to(within[:, LANES - 1:LANES], within.shape)
    return within + _dot(below, totals.astype(BF16))


def _route_kernel(nblk, cap, lg_ref, idx_ref, aff_ref, p_ref, sel_ref):
    lg = lg_ref[...]
    e = jnp.exp(lg - jnp.max(lg, axis=0, keepdims=True))
    aff = e / jnp.sum(e, axis=0, keepdims=True)
    aff_ref[...] = aff
    capf = float(cap)

    def count(mask):
        s = jnp.sum(jnp.where(mask, 1.0, 0.0), axis=2, keepdims=True)
        return jnp.sum(s, axis=1, keepdims=True)

    def search(i, lo):
        cand = lo | jnp.left_shift(jnp.int32(1), 30 - i)
        return jnp.where(count(aff >= pltpu.bitcast(cand, F32)) >= capf, cand, lo)

    thr_bits = lax.fori_loop(0, 31, search, jnp.zeros((N_EXPERTS, 1, 1), I32))
    thr = pltpu.bitcast(thr_bits, F32)
    need = capf - count(aff > thr)

    rows = N_EXPERTS * nblk
    ri = lax.broadcasted_iota(I32, (LANES, LANES), 0)
    ci = lax.broadcasted_iota(I32, (LANES, LANES), 1)
    triu = jnp.where(ri <= ci, 1.0, 0.0).astype(BF16)
    rr = lax.broadcasted_iota(I32, (rows, rows), 0)
    rc = lax.broadcasted_iota(I32, (rows, rows), 1)
    below = jnp.where((rc < rr) & (rc >= (rr // nblk) * nblk), 1.0, 0.0).astype(BF16)
    flat = lambda v: v.reshape(rows, LANES)
    gt = flat(jnp.where(aff > thr, 1.0, 0.0))
    eq = flat(jnp.where(aff == thr, 1.0, 0.0))
    need_rows = flat(jnp.broadcast_to(need, aff.shape))
    eq_rank = _cumsum_tokens(eq, triu, below) - eq
    sel = gt + eq * jnp.where(eq_rank < need_rows, 1.0, 0.0)
    rank = _cumsum_tokens(sel, triu, below)
    sel_ref[...] = sel.reshape(N_EXPERTS, nblk, LANES)
    p_ref[...] = rank.reshape(N_EXPERTS, nblk, LANES)

    ones8 = jnp.ones((SUBLANES, LANES), BF16)
    bi = lax.broadcasted_iota(I32, (nblk, nblk), 0)
    bj = lax.broadcasted_iota(I32, (nblk, nblk), 1)
    triu_b = jnp.where(bi <= bj, 1.0, 0.0).astype(BF16)
    blk_ids = lax.broadcasted_iota(I32, (SUBLANES, nblk), 1).astype(F32).astype(BF16)
    r_col = lax.broadcasted_iota(I32, (cap, 1), 0).astype(F32)

    def per_expert(ex, carry):
        pe = p_ref[ex]
        tot_row = _dot_nt(ones8, sel_ref[ex].astype(BF16))[0:1, :]
        upto = _dot(jnp.broadcast_to(tot_row, (SUBLANES, nblk)).astype(BF16), triu_b)[0:1, :]
        onehot = jnp.where((upto - tot_row <= r_col) & (r_col < upto), 1.0, 0.0).astype(BF16)
        p_hi = jnp.floor(pe * (1.0 / _RANK_SPLIT))
        p_lo = pe - p_hi * _RANK_SPLIT
        ranks = _dot(onehot, p_hi.astype(BF16)) * _RANK_SPLIT + _dot(onehot, p_lo.astype(BF16))
        inside = jnp.where(ranks <= r_col, 1.0, 0.0).astype(BF16)
        cnt = _dot_nt(ones8, inside)[0:1, :] + float(LANES) * _dot_nt(blk_ids, onehot)[0:1, :]
        idx_ref[ex] = cnt.astype(I32)
        return carry

    lax.fori_loop(0, N_EXPERTS, per_expert, 0)


def _route(logits, cap):
    t = logits.shape[1]
    nblk = t // LANES
    idx, aff = pl.pallas_call(
        functools.partial(_route_kernel, nblk, cap),
        out_shape=[jax.ShapeDtypeStruct((N_EXPERTS, 1, cap), I32),
                   jax.ShapeDtypeStruct((N_EXPERTS, nblk, LANES), F32)],
        scratch_shapes=[pltpu.VMEM((N_EXPERTS, nblk, LANES), F32),
                        pltpu.VMEM((N_EXPERTS, nblk, LANES), F32)],
        compiler_params=_cparams(None, VMEM_LIMIT),
        name="route",
    )(logits.reshape(N_EXPERTS, nblk, LANES))
    return idx.reshape(N_EXPERTS, cap), aff.reshape(N_EXPERTS, t)


_UP_TF = 512
_UNROLL = 16


def _gather_rows(tile_ref, streams, ex, step, n_steps):
    row0 = 0
    for src_ref, idx_ref, cap in streams:
        cnt = cap // n_steps
        for u in range(cnt):
            r = step * cnt + u
            src = pl.multiple_of(idx_ref[ex * cap + r] * PACK_SLAB, PACK_SLAB)
            dst = pl.multiple_of((row0 + r) * PACK_SLAB, PACK_SLAB)
            tile_ref[pl.ds(dst, PACK_SLAB), :] = src_ref[pl.ds(src, PACK_SLAB), :]
        row0 += cap


def _moe_up_kernel(caps, ua_ref, ub_ref, ia_ref, ib_ref, wg_ref, wu_ref, h_ref, tile_ref, xe_ref):
    ex = pl.program_id(0)
    ff = pl.program_id(1)
    n_steps = EXPERT_FF // _UP_TF
    m = sum(caps)
    streams = ((ua_ref, ia_ref, caps[0]), (ub_ref, ib_ref, caps[1]))

    @pl.when((ex == 0) & (ff == 0))
    def _first_rows():
        def body(i, carry):
            row0 = 0
            for src_ref, idx_ref, cap in streams:
                @pl.when(i < cap // _UNROLL)
                def _(src_ref=src_ref, idx_ref=idx_ref, row0=row0):
                    for u in range(_UNROLL):
                        r = i * _UNROLL + u
                        src = pl.multiple_of(idx_ref[r] * PACK_SLAB, PACK_SLAB)
                        dst = pl.multiple_of((row0 + r) * PACK_SLAB, PACK_SLAB)
                        tile_ref[pl.ds(dst, PACK_SLAB), :] = src_ref[pl.ds(src, PACK_SLAB), :]
                row0 += cap
            return carry

        lax.fori_loop(0, max(caps) // _UNROLL, body, 0)

    @pl.when(ff == 0)
    def _unpack():
        half = D_MODEL // 2
        for j in range(PACK_SLAB):
            w = tile_ref[pl.ds(j, m, stride=PACK_SLAB), :]
            for k in range(2):
                v = pltpu.unpack_elementwise(w, index=k, packed_dtype=BF16, unpacked_dtype=F32)
                lo = k * half + j * LANES
                xe_ref[:, lo:lo + LANES] = v.astype(BF16)

    x = xe_ref[...]
    g = _dot(x, wg_ref[...].astype(BF16))
    u = _dot(x, wu_ref[...].astype(BF16))
    h_ref[...] = (_silu(g) * u).astype(BF16)
    nxt = jnp.minimum(ex + 1, N_EXPERTS - 1)
    _gather_rows(tile_ref, streams, nxt, ff, n_steps)


def _moe_up(u2p_a, u2p_b, idx_a, idx_b, w_gate, w_up, layer):
    caps = (idx_a.shape[1], idx_b.shape[1])
    m = sum(caps)
    smem = pl.BlockSpec(memory_space=pltpu.SMEM)
    w_spec = pl.BlockSpec((None, None, D_MODEL, _UP_TF), lambda e, f: (layer, e, 0, f))
    return pl.pallas_call(
        functools.partial(_moe_up_kernel, caps),
        out_shape=jax.ShapeDtypeStruct((N_EXPERTS, m, EXPERT_FF), BF16),
        grid=(N_EXPERTS, EXPERT_FF // _UP_TF),
        in_specs=[pl.BlockSpec(u2p_a.shape, lambda e, f: (0, 0), pipeline_mode=pl.Buffered(1)),
                  pl.BlockSpec(u2p_b.shape, lambda e, f: (0, 0), pipeline_mode=pl.Buffered(1)),
                  smem, smem, w_spec, w_spec],
        out_specs=pl.BlockSpec((None, m, _UP_TF), lambda e, f: (e, 0, f)),
        scratch_shapes=[pltpu.VMEM((m * PACK_SLAB, LANES), U32), pltpu.VMEM((m, D_MODEL), BF16)],
        compiler_params=_cparams(("arbitrary", "arbitrary"), VMEM_LIMIT),
        name="moe_up",
    )(u2p_a, u2p_b, idx_a.reshape(-1), idx_b.reshape(-1), w_gate, w_up)


_DOWN_TN = 512
_RMW = 16


def _scatter_rows(acc_ref, slab_ref, idx_ref, aff_ref, idx0, src0, scale, rows):
    for b0 in range(0, rows, _RMW):
        dsts, vals = [], []
        for u in range(b0, b0 + _RMW):
            tok = idx_ref[idx0 + u]
            dst = pl.multiple_of(tok * ROW_SLAB, ROW_SLAB)
            src = pl.multiple_of(src0 + u * ROW_SLAB, ROW_SLAB)
            vals.append(acc_ref[pl.ds(dst, ROW_SLAB), :]
                        + (aff_ref[tok] * scale) * slab_ref[pl.ds(src, ROW_SLAB), :])
            dsts.append(dst)
        for dst, val in zip(dsts, vals):
            acc_ref[pl.ds(dst, ROW_SLAB), :] = val


def _moe_down_kernel(cap, h_ref, wd_ref, idx_ref, affp_ref, aff_ref, out_ref, acc_ref, slab_ref):
    ex = pl.program_id(0)
    nn = pl.program_id(1)
    n_steps = D_MODEL // _DOWN_TN
    n_half = _DOWN_TN // LANES
    part = cap // n_steps
    par = ex % 2

    @pl.when((ex == 0) & (nn == 0))
    def _zero():
        acc_ref[...] = jnp.zeros(acc_ref.shape, F32)
        slab_ref[...] = jnp.zeros(slab_ref.shape, F32)

    ye = _dot(h_ref[...], wd_ref[...].astype(BF16))
    prev = jnp.maximum(ex - 1, 0)
    live = jnp.where(ex > 0, 1.0, 0.0)
    row0 = nn * part
    _scatter_rows(acc_ref, slab_ref.at[1 - par], idx_ref, affp_ref, prev * cap + row0,
                  row0 * ROW_SLAB, live, part)
    dst = slab_ref.at[par]
    for j in range(n_half):
        dst[pl.ds(nn * n_half + j, cap, stride=ROW_SLAB), :] = ye[:, j * LANES:(j + 1) * LANES]

    @pl.when((ex == N_EXPERTS - 1) & (nn == n_steps - 1))
    def _tail():
        def body(i, carry):
            _scatter_rows(acc_ref, slab_ref.at[par], idx_ref, aff_ref, ex * cap + i * _RMW,
                          i * (_RMW * ROW_SLAB), 1.0, _RMW)
            return carry

        lax.fori_loop(0, cap // _RMW, body, 0)
        pltpu.sync_copy(acc_ref, out_ref)


def _moe_down(h, w_down, idx, aff, row_block, layer):
    cap = idx.shape[1]
    t = aff.shape[1]
    aff3 = aff[:, None, :]
    gate_spec = lambda shift: pl.BlockSpec(
        (None, None, t), lambda e, n: (jnp.maximum(e - shift, 0), 0, 0), memory_space=pltpu.SMEM)
    return pl.pallas_call(
        functools.partial(_moe_down_kernel, cap),
        out_shape=jax.ShapeDtypeStruct((t * ROW_SLAB, LANES), F32),
        grid=(N_EXPERTS, D_MODEL // _DOWN_TN),
        in_specs=[pl.BlockSpec((None, cap, EXPERT_FF), lambda e, n: (e, row_block, 0)),
                  pl.BlockSpec((None, None, EXPERT_FF, _DOWN_TN), lambda e, n: (layer, e, 0, n)),
                  pl.BlockSpec(memory_space=pltpu.SMEM), gate_spec(1), gate_spec(0)],
        out_specs=pl.BlockSpec(memory_space=pl.ANY),
        scratch_shapes=[pltpu.VMEM((t * ROW_SLAB, LANES), F32),
                        pltpu.VMEM((2, cap * ROW_SLAB, LANES), F32)],
        compiler_params=_cparams(("arbitrary", "arbitrary"), VMEM_LIMIT),
        name="moe_down",
    )(h, w_down, idx.reshape(-1), aff3, aff3)


def _swap_pairs(w):
    s = w.shape
    return jnp.flip(w.reshape(s[:-1] + (s[-1] // 2, 2)), axis=-1).reshape(s)


def _pack_w_in(w_in):
    o_dt, o_cq, o_kr, o_fin = 1280, 1296, 1680, 1712
    kr = w_in[:, :, o_kr:o_fin]
    pad = jnp.zeros(w_in.shape[:2] + (_IN_KR4 - _IN_SMALL - 2 * SSD_HEADS,), w_in.dtype)
    parts = [w_in[:, :, :o_dt], w_in[:, :, o_cq:o_kr], w_in[:, :, o_fin:], w_in[:, :, o_dt:o_cq], pad,
             jnp.tile(kr, (1, 1, MLA_HEADS)), jnp.tile(_swap_pairs(kr), (1, 1, MLA_HEADS))]
    return jnp.concatenate(parts, axis=2).astype(BF16)


def _pack_w_uq(w_uq):
    w = w_uq.reshape(w_uq.shape[:2] + (MLA_HEADS, QK_NOPE + ROPE_DIM))
    flat = lambda v: v.reshape(v.shape[:2] + (-1,))
    rope = w[..., QK_NOPE:]
    return jnp.concatenate([flat(w[..., :QK_NOPE]), flat(rope), flat(_swap_pairs(rope))],
                           axis=2).astype(BF16)


def _pack_w_ukv(w_ukv):
    w = w_ukv.reshape(w_ukv.shape[:2] + (MLA_HEADS, QK_NOPE + V_DIM))
    flat = lambda v: v.reshape(v.shape[:2] + (-1,))
    return jnp.concatenate([flat(w[..., :QK_NOPE]), flat(w[..., QK_NOPE:])], axis=2).astype(BF16)


def _rope_tables(seq):
    rows = seq // GRID_W
    r, col = jnp.meshgrid(jnp.arange(rows, dtype=F32), jnp.arange(GRID_W, dtype=F32), indexing="ij")
    pairs = ROPE_DIM // 4
    inv = ROPE_BASE ** (-jnp.arange(pairs, dtype=F32) / pairs)
    ang = jnp.concatenate([r.reshape(-1, 1) * inv, col.reshape(-1, 1) * inv], axis=-1)
    cos = jnp.repeat(jnp.cos(ang), 2, axis=1)
    sin = jnp.repeat(jnp.sin(ang), 2, axis=1) * jnp.tile(jnp.array([-1.0, 1.0], F32), ROPE_DIM // 2)
    return jnp.tile(cos, (1, MLA_HEADS)), jnp.tile(sin, (1, MLA_HEADS))


def _pad_lanes(v):
    return jnp.pad(v, ((0, 0), (0, LANES - v.shape[1])))[:, None, :]


def kernel(x_prompt, x_sample, cache_ckv, cache_krope, state_ssm, c, c_ctx,
           w_in, conv_w, conv_b, dt_bias, a_log, d_skip, ssd_norm_g,
           q_norm_g, w_uq, kv_norm_g, w_ukv, w_out, w_mod, b_mod,
           ln1_g, ln1_b, ln2_g, ln2_b, w_router, w_gate, w_up, w_down):
    nb_c, seq_c, _ = x_prompt.shape
    nb_l, seq_l, _ = x_sample.shape
    t_c, t_l = nb_c * seq_c, nb_l * seq_l
    caps = (CAPACITY_FACTOR * t_c // N_EXPERTS, CAPACITY_FACTOR * t_l // N_EXPERTS)

    w_in_p = _pack_w_in(w_in)
    wq = _pack_w_uq(w_uq)
    wkv = _pack_w_ukv(w_ukv)
    row = lambda v: v[:, None, :]
    lp = dict(conv_w=conv_w, conv_b=row(conv_b), dt_bias=_pad_lanes(dt_bias.reshape(DEPTH, -1)),
              a_log=_pad_lanes(a_log.reshape(DEPTH, -1)),
              d_skip=row(jnp.repeat(d_skip, SSD_HEADDIM, axis=1)), ssd_norm_g=row(ssd_norm_g),
              q_norm_g=row(q_norm_g), wq=wq, kv_norm_g=row(kv_norm_g), wkv=wkv,
              w_out=w_out.astype(BF16), ln1_g=row(ln1_g), ln1_b=row(ln1_b),
              wrt=jnp.swapaxes(w_router, 1, 2).astype(BF16))
    ln2 = (row(ln2_g), row(ln2_b))
    cache_kr4 = jnp.tile(cache_krope, (1, 1, 1, MLA_HEADS))
    cos4, sin4 = _rope_tables(seq_l)

    cond = jnp.zeros((SUBLANES, D_MODEL), F32).at[0].set(c_ctx).at[1:1 + nb_l].set(c)
    mod_all = _modulation(cond, w_mod, b_mod).reshape(DEPTH, SUBLANES, 6, D_MODEL)

    streams = [
        dict(x=x_prompt.reshape(t_c, D_MODEL), nb=nb_c, seq=seq_c, cap=caps[0], mod_rows=(0, 1)),
        dict(x=x_sample.reshape(t_l, D_MODEL), nb=nb_l, seq=seq_l, cap=caps[1], mod_rows=(1, nb_l)),
    ]
    prev = [None, None]
    ssm_hist = jnp.zeros((nb_c, DEPTH) + state_ssm.shape[2:], F32)
    kv_hist = (jnp.zeros((nb_c, DEPTH, seq_c, KV_LORA), F32),
               jnp.zeros((nb_c, DEPTH, seq_c, ROPE_DIM), F32))
    for l in range(DEPTH):
        routed = []
        for si, st in enumerate(streams):
            nb, seq, mod_rows = st["nb"], st["seq"], st["mod_rows"]
            x, z, xbc, cq, ckv, fin, small, kr = _in_proj(
                st["x"], prev[si], mod_all, mod_rows, w_in_p, seq, l)
            if si == 0:
                yssd, ssm_hist = _ssd(z, xbc, small, lp, nb, seq, None, ssm_hist, l)
                ymla, *kv_hist = _mla(cq, ckv, kr, lp, nb, seq, None, kv_hist, l)
            else:
                yssd, _ = _ssd(z, xbc, small, lp, nb, seq, state_ssm, None, l)
                (ymla,) = _mla(cq, ckv, kr, lp, nb, seq, (cache_ckv, cache_kr4, cos4, sin4), None, l)
            yfft = _fnet(fin, nb, seq)
            x1, u2p, logits = _out_proj(x, yssd, ymla, yfft, mod_all, mod_rows, lp, seq, l)
            idx, aff = _route(logits, st["cap"])
            st["x"] = x1
            routed.append((u2p, idx, aff))
        h = _moe_up(routed[0][0], routed[1][0], routed[0][1], routed[1][1], w_gate, w_up, l)
        for si in range(2):
            moe = _moe_down(h, w_down, routed[si][1], routed[si][2], 0 if si == 0 else 2, l)
            prev[si] = (moe,) + ln2
    last = DEPTH - 1
    y_p, y_s = [_final_norm(st["x"], *prev[si], mod_all, st["mod_rows"], st["seq"], last)
                for si, st in enumerate(streams)]
    y_p, y_s = y_p.reshape(x_prompt.shape), y_s.reshape(x_sample.shape)
    return (y_p, y_s, kv_hist[0], kv_hist[1], ssm_hist)
```

```python
import functools

import jax
import jax.numpy as jnp
import numpy as np
from jax import lax
from jax.experimental import pallas as pl
from jax.experimental.pallas import tpu as pltpu

F32 = jnp.float32
BF16 = jnp.bfloat16
I32 = jnp.int32
U32 = jnp.uint32

D_MODEL = 1024
DEPTH = 4
GRID_W = 64
SSD_HEADS = 8
SSD_HEADDIM = 64
SSD_DIM = SSD_HEADS * SSD_HEADDIM
SSD_STATE = 64
SSD_CONV = 5
SSD_CHUNK = 128
SSD_CONV_DIM = SSD_DIM + 4 * SSD_STATE
MLA_HEADS = 4
Q_LORA = 256
KV_LORA = 128
QK_NOPE = 64
ROPE_DIM = 32
V_DIM = 64
MLA_DIM = MLA_HEADS * V_DIM
MLA_SCALE = (QK_NOPE + ROPE_DIM) ** -0.5
ROPE_BASE = 10000.0
Q_TILE = 256
FNET_GW = 64
FNET_DIM = 256
N_EXPERTS = 16
EXPERT_FF = 1024
CAPACITY_FACTOR = 2
DN_ALPHA = (2 * DEPTH) ** 0.25
EPS = 1e-5
_LOG2E = 1.4426950408889634

LANES = 128
SUBLANES = 8
ROW_SLAB = D_MODEL // LANES
PACK_SLAB = ROW_SLAB // 2
VMEM_LIMIT = 56 * 1024 * 1024

_IN_Z = 0
_IN_XBC = 512
_IN_CQ = 1280
_IN_CKV = 1536
_IN_FIN = 1664
_IN_SMALL = 1920
_IN_KR4 = 2048
IN_PAD = 2304
_IN_WIDTHS = (512, 768, 256, 128, 256, 128, 256)


def _cparams(sem, vmem=None):
    return pltpu.CompilerParams(dimension_semantics=sem, vmem_limit_bytes=vmem)


def _ln(x):
    mu = jnp.mean(x, axis=-1, keepdims=True)
    xc = x - mu
    var = jnp.mean(xc * xc, axis=-1, keepdims=True)
    return xc * lax.rsqrt(var + EPS)


def _rms(x):
    return x * lax.rsqrt(jnp.mean(x * x, axis=-1, keepdims=True) + EPS)


def _silu(x):
    return x * (1.0 / (1.0 + jnp.exp(-x)))


def _dot(a, b):
    return jnp.dot(a, b, preferred_element_type=F32)


def _dot_nt(a, b):
    return lax.dot_general(a, b, (((1,), (1,)), ((), ())), preferred_element_type=F32)


def _split3(x):
    hi = x.astype(BF16)
    r1 = x - hi.astype(F32)
    mid = r1.astype(BF16)
    lo = (r1 - mid.astype(F32)).astype(BF16)
    return hi, mid, lo


def _const_spec(shape):
    return pl.BlockSpec(shape, lambda *_: (0,) * len(shape))


def _layer_spec(shape, layer):
    return pl.BlockSpec((None,) + shape, lambda *_: (layer,) + (0,) * len(shape))


_MOD_TN = 1536


def _mod_kernel(c_ref, w_ref, b_ref, o_ref):
    a = _silu(c_ref[...]).astype(BF16)
    o_ref[...] = _dot(a, w_ref[...].astype(BF16)) + b_ref[...]


def _modulation(cond, w_mod, b_mod):
    n = 6 * D_MODEL
    return pl.pallas_call(
        _mod_kernel,
        out_shape=jax.ShapeDtypeStruct((DEPTH, SUBLANES, n), F32),
        grid=(DEPTH, n // _MOD_TN),
        in_specs=[
            pl.BlockSpec((SUBLANES, D_MODEL), lambda l, j: (0, 0)),
            pl.BlockSpec((None, D_MODEL, _MOD_TN), lambda l, j: (l, 0, j)),
            pl.BlockSpec((None, 1, _MOD_TN), lambda l, j: (l, 0, j)),
        ],
        out_specs=pl.BlockSpec((None, SUBLANES, _MOD_TN), lambda l, j: (l, 0, j)),
        compiler_params=_cparams(("arbitrary", "arbitrary")),
        name="modulation",
    )(cond, w_mod, b_mod.reshape(DEPTH, 1, n))


_TM = 512


def _read_rows(slab_ref, tm):
    return jnp.concatenate(
        [slab_ref[pl.ds(j, tm, stride=ROW_SLAB), :] for j in range(ROW_SLAB)], axis=-1)


def _post_norm2(x_ref, moe_ref, pmod_ref, g_ref, b_ref):
    moe = _read_rows(moe_ref, _TM)
    return _ln(DN_ALPHA * x_ref[...] + pmod_ref[5:6, :] * moe) * g_ref[...] + b_ref[...]


def _in_proj_kernel(first, *refs):
    if first:
        x_ref, mod_ref, w_ref = refs[:3]
        rest = refs[3:]
    else:
        x_ref, moe_ref, pmod_ref, g_ref, b_ref, mod_ref, w_ref, xo_ref = refs[:8]
        rest = refs[8:]
    outs, bufs = rest[:len(_IN_WIDTHS)], rest[len(_IN_WIDTHS):]
    step = pl.program_id(0)

    @pl.when(step == 0)
    def _no_previous_tile():
        bufs[1][...] = jnp.zeros(bufs[1].shape, BF16)

    def stage(u_prev_ref, u_next_ref):
        if first:
            x = x_ref[...]
        else:
            x = _post_norm2(x_ref, moe_ref, pmod_ref, g_ref, b_ref)
            xo_ref[...] = x
        u_next_ref[...] = (_ln(x) * (1.0 + mod_ref[1:2, :]) + mod_ref[0:1, :]).astype(BF16)
        p = _dot(u_prev_ref[...], w_ref[...])
        off = 0
        for o_ref, w in zip(outs, _IN_WIDTHS):
            o_ref[...] = p[:, off:off + w]
            off += w

    for parity in range(2):
        @pl.when(step % 2 == parity)
        def _(parity=parity):
            stage(bufs[1 - parity], bufs[parity])


def _tile(i, lag, n_tiles):
    return i if n_tiles is None else jnp.clip(i - lag, 0, n_tiles - 1)


def _mod_spec(mod_rows, t, seq, layer, lag=0, n_tiles=None):
    row0, count = mod_rows
    per = seq // _TM if count > 1 else t // _TM
    return pl.BlockSpec((None, None, 6, D_MODEL),
                        lambda i: (layer, row0 + _tile(i, lag, n_tiles) // per, 0, 0))


def _row_spec(w, lag=0, n_tiles=None):
    return pl.BlockSpec((_TM, w), lambda i: (_tile(i, lag, n_tiles), 0))


def _slab_spec(lag=0, n_tiles=None):
    return pl.BlockSpec((_TM * ROW_SLAB, LANES), lambda i: (_tile(i, lag, n_tiles), 0))


def _in_proj(x, prev, mod_all, mod_rows, w_in_p, seq, layer):
    t = x.shape[0]
    n = t // _TM
    mod_spec = _mod_spec(mod_rows, t, seq, layer, 0, n)
    outs = [jax.ShapeDtypeStruct((t, w), F32) for w in _IN_WIDTHS]
    out_specs = [_row_spec(w, 1, n) for w in _IN_WIDTHS]
    w_spec = _layer_spec((D_MODEL, IN_PAD), layer)
    if prev is None:
        ins = [x, mod_all, w_in_p]
        in_specs = [_row_spec(D_MODEL, 0, n), mod_spec, w_spec]
    else:
        moe_slab, g, b = prev
        ins = [x, moe_slab, mod_all, g, b, mod_all, w_in_p]
        in_specs = [_row_spec(D_MODEL, 0, n), _slab_spec(0, n),
                    _mod_spec(mod_rows, t, seq, layer - 1, 0, n),
                    _layer_spec((1, D_MODEL), layer - 1), _layer_spec((1, D_MODEL), layer - 1),
                    mod_spec, w_spec]
        outs = [jax.ShapeDtypeStruct((t, D_MODEL), F32)] + outs
        out_specs = [_row_spec(D_MODEL, 0, n)] + out_specs
    res = pl.pallas_call(
        functools.partial(_in_proj_kernel, prev is None),
        out_shape=outs, grid=(n + 1,), in_specs=in_specs, out_specs=out_specs,
        scratch_shapes=[pltpu.VMEM((_TM, D_MODEL), BF16)] * 2,
        compiler_params=_cparams(("arbitrary",), VMEM_LIMIT),
        name="in_proj",
    )(*ins)
    if prev is None:
        return (x,) + tuple(res)
    return tuple(res)


def _final_norm_kernel(x_ref, moe_ref, pmod_ref, g_ref, b_ref, o_ref):
    o_ref[...] = _post_norm2(x_ref, moe_ref, pmod_ref, g_ref, b_ref)


def _final_norm(x, moe_slab, g, b, mod_all, mod_rows, seq, layer):
    t = x.shape[0]
    return pl.pallas_call(
        _final_norm_kernel,
        out_shape=jax.ShapeDtypeStruct((t, D_MODEL), F32), grid=(t // _TM,),
        in_specs=[_row_spec(D_MODEL), _slab_spec(), _mod_spec(mod_rows, t, seq, layer),
                  _layer_spec((1, D_MODEL), layer), _layer_spec((1, D_MODEL), layer)],
        out_specs=_row_spec(D_MODEL),
        compiler_params=_cparams(("arbitrary",)),
        name="final_norm",
    )(x, moe_slab, mod_all, g, b)


def _ssd_kernel(has_h0, write_h, seq, *refs):
    refs = list(refs)
    z_ref, xbc_ref, sm_ref, cw_ref, cb_ref, dtb_ref, alog_ref, dsk_ref, ng_ref = refs[:9]
    pos = 9
    if has_h0:
        h0_ref = refs[pos]
    pos += 1
    y_ref = refs[pos]
    pos += 1
    if write_h:
        hf_ref = refs[pos]
        pos += 1
    pad_ref, xs_ref, cm_ref, gm_ref, bt_ref, acol_ref, rows_ref, st_ref = refs[pos:]
    ck = SSD_CHUNK
    nc = seq // ck
    halo = SUBLANES
    nh = SSD_HEADS
    nh2 = 2 * nh

    pad_ref[0:halo, :] = jnp.zeros((halo, SSD_CONV_DIM), F32)
    pad_ref[halo + seq:2 * halo + seq, :] = jnp.zeros((halo, SSD_CONV_DIM), F32)
    pad_ref[halo:halo + seq, :] = xbc_ref[...]
    ri = lax.broadcasted_iota(I32, (ck, ck), 0)
    ci = lax.broadcasted_iota(I32, (ck, ck), 1)
    triu = jnp.where(ri <= ci, 1.0, 0.0).astype(BF16)
    lo_half = ci < SSD_STATE
    lo64 = lax.broadcasted_iota(I32, (SSD_STATE, LANES), 1) < SSD_STATE
    zeros_half = jnp.zeros((SSD_STATE, LANES), BF16)
    fwd_rows = lax.broadcasted_iota(I32, (nh2, ck), 0) < nh

    def prepare_chunk(c, carry):
        cs = pl.multiple_of(c * ck, ck)
        rows = pl.ds(cs, ck)
        cw_blk = 2 * LANES
        for cb in range(SSD_CONV_DIM // cw_blk):
            cols = slice(cb * cw_blk, (cb + 1) * cw_blk)
            win = pad_ref[pl.ds(cs, ck + 2 * halo), cols]
            acc = jnp.zeros((ck, cw_blk), F32) + cb_ref[:, cols]
            for k in range(SSD_CONV):
                s0 = halo + k - (SSD_CONV - 1) // 2
                acc = acc + win[s0:s0 + ck, :] * cw_ref[k:k + 1, cols]
            act = _silu(acc)
            if cb * cw_blk < SSD_DIM:
                xs_ref[rows, cols] = act
                y_ref[rows, cols] = act * dsk_ref[:, cols]
        bm = act[:, :LANES]
        cm = act[:, LANES:]
        cm_ref[rows, :] = cm
        bt_ref[rows, :] = bm.T
        cg2 = jnp.concatenate([jnp.where(lo_half, cm, 0.0), jnp.where(lo_half, 0.0, cm)], axis=0)
        gm2 = _dot_nt(cg2.astype(BF16), bm.astype(BF16))
        gm_ref[rows, 0:LANES] = gm2[:ck]
        gm_ref[rows, LANES:] = gm2[ck:]

        xr = sm_ref[rows, :] + dtb_ref[...]
        dt = jnp.maximum(xr, 0.0) + jnp.log(1.0 + jnp.exp(-jnp.abs(xr)))
        ad = jnp.where(ci < nh2, dt * -jnp.exp(alog_ref[...]),
                       jnp.where(ci < 2 * nh2, pltpu.roll(dt, nh2, axis=1), 0.0))
        adt = ad.T
        at16 = adt[0:nh2, :]
        dt16 = adt[nh2:2 * nh2, :]
        csum = _dot(jnp.concatenate(_split3(at16), axis=0), triu)
        arow = csum[0:nh2] + csum[nh2:2 * nh2] + csum[2 * nh2:3 * nh2]
        acol = jnp.concatenate([arow, jnp.zeros((ck - nh2, ck), F32)], axis=0).T
        alast = jnp.broadcast_to(arow[:, ck - 1:ck], (nh2, ck)) * _LOG2E
        arow_s = jnp.where(fwd_rows, arow, arow - at16) * _LOG2E
        ldt = jnp.log2(dt16)
        rowarg = jnp.where(fwd_rows, arow_s - ldt, arow_s + ldt)
        wrow = jnp.exp2(jnp.where(fwd_rows, alast - arow_s, arow_s))
        acol_ref[rows, :] = jnp.where(ci < nh, acol, acol - ad) * _LOG2E
        rows_ref[pl.ds(pl.multiple_of(c * (4 * nh2), 4 * nh2), 4 * nh2), :] = jnp.concatenate(
            [rowarg, wrow * dt16, jnp.exp2(alast), alast], axis=0)
        return carry

    lax.fori_loop(0, nc, prepare_chunk, 0)

    if has_h0:
        for d in range(2):
            h0 = h0_ref[d].reshape(SSD_DIM, SSD_STATE)
            h0 = jnp.concatenate([h0, jnp.zeros((SSD_DIM, LANES - SSD_STATE), F32)], axis=1)
            st_ref[d] = h0.T[:SSD_STATE, :]
    else:
        st_ref[...] = jnp.zeros(st_ref.shape, F32)

    def chunk_dir(d, c):
        cs = pl.multiple_of(c * ck, ck)
        acol = acol_ref[pl.ds(cs, ck), :]
        rws = rows_ref[pl.ds(pl.multiple_of(c * (4 * nh2), 4 * nh2), 4 * nh2), :]
        cm = cm_ref[pl.ds(cs, ck), :]
        cg = [jnp.where(lo_half, cm, 0.0), jnp.where(lo_half, 0.0, cm)]
        for q in range(nh // 2):
            g = q // 2
            gmat = gm_ref[pl.ds(cs, ck), g * LANES:(g + 1) * LANES]
            btg = bt_ref[pl.ds(cs + g * SSD_STATE, SSD_STATE), :]
            lhs_rows = []
            decl = []
            for hh in range(2):
                hi = d * nh + 2 * q + hh
                acol_b = jnp.broadcast_to(acol[:, hi:hi + 1], (ck, ck))
                rowarg = rws[hi:hi + 1, :]
                if d == 0:
                    lmat = jnp.where(ri >= ci, jnp.exp2(acol_b - rowarg), 0.0)
                    ecol = jnp.exp2(acol_b)
                else:
                    lmat = jnp.where(ci >= ri, jnp.exp2(rowarg - acol_b), 0.0)
                    ecol = jnp.exp2(rws[3 * nh2 + hi:3 * nh2 + hi + 1, :] - acol_b)
                mm = (gmat * lmat).astype(BF16)
                cs_h = (cg[g] * ecol).astype(BF16)
                bw = (btg * rws[nh2 + hi:nh2 + hi + 1, :]).astype(BF16)
                lhs_rows.append(jnp.concatenate([mm, cs_h], axis=1))
                lhs_rows.append(jnp.concatenate([bw, zeros_half], axis=1))
                decl.append(rws[2 * nh2 + hi:2 * nh2 + hi + 1, :])
            lhs = jnp.concatenate(lhs_rows, axis=0)
            cols = slice(q * LANES, (q + 1) * LANES)
            st_pair = st_ref[d, :, cols]
            stb = st_pair.astype(BF16)
            w_st = jnp.concatenate([stb, zeros_half] if g == 0 else [zeros_half, stb], axis=0)
            xs_pair = xs_ref[pl.ds(cs, ck), cols].astype(BF16)
            out = _dot(lhs, jnp.concatenate([xs_pair, w_st], axis=0))
            r2 = ck + SSD_STATE
            y_pair = jnp.where(lo_half, out[0:ck], out[r2:r2 + ck])
            st_new = jnp.where(lo64, out[ck:r2], out[r2 + ck:2 * r2])
            dec = jnp.where(lo64, decl[0], decl[1])
            st_ref[d, :, cols] = st_pair * dec + st_new
            y_ref[pl.ds(cs, ck), cols] += y_pair

    def body(c, carry):
        chunk_dir(0, c)
        chunk_dir(1, nc - 1 - c)
        return carry

    lax.fori_loop(0, nc, body, 0)

    def gate_chunk(c, carry):
        rows = pl.ds(pl.multiple_of(c * ck, ck), ck)
        y = y_ref[rows, :] * _silu(z_ref[rows, :])
        y_ref[rows, :] = _rms(y) * ng_ref[...]
        return carry

    lax.fori_loop(0, nc, gate_chunk, 0)

    if write_h:
        for d in range(2):
            st = jnp.concatenate([st_ref[d], jnp.zeros((LANES - SSD_STATE, SSD_DIM), F32)], axis=0)
            hf_ref[d] = st.T[:, :SSD_STATE].reshape(SSD_HEADS, SSD_HEADDIM, SSD_STATE)


def _ssd(z, xbc, small, lp, nb, seq, h0, hist, layer):
    has_h0 = h0 is not None
    write_h = not has_h0
    seq_spec = lambda w: pl.BlockSpec((seq, w), lambda b: (b, 0))
    ins = [z, xbc, small, lp["conv_w"], lp["conv_b"], lp["dt_bias"], lp["a_log"], lp["d_skip"],
           lp["ssd_norm_g"]]
    in_specs = [seq_spec(SSD_DIM), seq_spec(SSD_CONV_DIM), seq_spec(LANES),
                _layer_spec((SSD_CONV, SSD_CONV_DIM), layer), _layer_spec((1, SSD_CONV_DIM), layer),
                _layer_spec((1, LANES), layer), _layer_spec((1, LANES), layer),
                _layer_spec((1, SSD_DIM), layer), _layer_spec((1, SSD_DIM), layer)]
    hshape = (2, SSD_HEADS, SSD_HEADDIM, SSD_STATE)
    layer_block = pl.BlockSpec((None, None) + hshape, lambda b: (b, layer, 0, 0, 0, 0))
    outs = [jax.ShapeDtypeStruct((nb * seq, SSD_DIM), F32)]
    out_specs = [seq_spec(SSD_DIM)]
    aliases = {}
    if has_h0:
        ins.append(h0)
        in_specs.append(layer_block)
    else:
        ins.append(hist)
        in_specs.append(pl.BlockSpec(memory_space=pl.ANY))
        aliases = {len(ins) - 1: 1}
        outs.append(jax.ShapeDtypeStruct(hist.shape, F32))
        out_specs.append(layer_block)
    res = pl.pallas_call(
        functools.partial(_ssd_kernel, has_h0, write_h, seq),
        out_shape=outs, grid=(nb,), in_specs=in_specs, out_specs=out_specs,
        input_output_aliases=aliases,
        scratch_shapes=[pltpu.VMEM((seq + 2 * SUBLANES, SSD_CONV_DIM), F32),
                        pltpu.VMEM((seq, SSD_DIM), F32),
                        pltpu.VMEM((seq, LANES), F32),
                        pltpu.VMEM((seq, 2 * LANES), F32),
                        pltpu.VMEM((seq, LANES), F32),
                        pltpu.VMEM((seq, LANES), F32),
                        pltpu.VMEM((seq // SSD_CHUNK * 8 * SSD_HEADS, LANES), F32),
                        pltpu.VMEM((2, SSD_STATE, SSD_DIM), F32)],
        compiler_params=_cparams(("arbitrary",), VMEM_LIMIT),
        name="ssd",
    )(*ins)
    return res if write_h else (res[0], None)


def _mla_kernel(has_ctx, seq, *refs):
    refs = list(refs)
    cq_ref, ckv_ref, kr_ref, qg_ref, wq_ref, kvg_ref, wkv_ref = refs[:7]
    pos = 7
    if has_ctx:
        cckv_ref, ckr_ref, cos_ref, sin_ref = refs[pos:pos + 4]
        pos += 4
    else:
        pos += 2
    y_ref = refs[pos]
    pos += 1
    if not has_ctx:
        ckvo_ref, kro_ref = refs[pos:pos + 2]

    qn = (_rms(cq_ref[...]) * qg_ref[...]).astype(BF16)
    qall = _dot(qn, wq_ref[...])
    q_nope = qall[:, :MLA_DIM]
    q_rope = qall[:, MLA_DIM:MLA_DIM + LANES]
    ckv_n = _rms(ckv_ref[...]) * kvg_ref[...]
    kr4 = kr_ref[:, :LANES]
    if has_ctx:
        q_rope = q_rope * cos_ref[...] + qall[:, MLA_DIM + LANES:] * sin_ref[...]
        kr4 = kr4 * cos_ref[...] + kr_ref[:, LANES:] * sin_ref[...]
        ckv_all = jnp.concatenate([cckv_ref[...], ckv_n], axis=0)
        kr_all = jnp.concatenate([ckr_ref[...], kr4], axis=0)
    else:
        ckvo_ref[...] = ckv_n
        kro_ref[...] = kr4[:, :ROPE_DIM]
        ckv_all = ckv_n
        kr_all = kr4
    kv = _dot(ckv_all.astype(BF16), wkv_ref[...])
    krb = kr_all.astype(BF16)
    lane = lax.broadcasted_iota(I32, (Q_TILE, LANES), 1)
    lo_half = lane < QK_NOPE
    for qt in range(seq // Q_TILE):
        rows = slice(qt * Q_TILE, (qt + 1) * Q_TILE)
        qr = q_rope[rows] * MLA_SCALE
        for pair in range(MLA_HEADS // 2):
            cols = slice(pair * LANES, (pair + 1) * LANES)
            kcat = jnp.concatenate([kv[:, cols].astype(BF16), krb], axis=1)
            vb = kv[:, MLA_DIM + pair * LANES:MLA_DIM + (pair + 1) * LANES].astype(BF16)
            qn_pair = q_nope[rows, cols] * MLA_SCALE
            outs = []
            for hh in range(2):
                h = 2 * pair + hh
                qa = jnp.where(lo_half if hh == 0 else jnp.logical_not(lo_half), qn_pair, 0.0)
                qb = jnp.where((lane >= h * ROPE_DIM) & (lane < (h + 1) * ROPE_DIM), qr, 0.0)
                s = _dot_nt(jnp.concatenate([qa, qb], axis=1).astype(BF16), kcat)
                p = jnp.exp(s - jnp.max(s, axis=-1, keepdims=True))
                o = _dot(p.astype(BF16), vb)
                outs.append(o / jnp.sum(p, axis=-1, keepdims=True))
            y_ref[rows, cols] = jnp.where(lo_half, outs[0], outs[1])


def _mla(cq, ckv, kr, lp, nb, seq, ctx, hist, layer):
    has_ctx = ctx is not None
    seq_spec = lambda w: pl.BlockSpec((seq, w), lambda b: (b, 0))
    ins = [cq, ckv, kr, lp["q_norm_g"], lp["wq"], lp["kv_norm_g"], lp["wkv"]]
    in_specs = [seq_spec(Q_LORA), seq_spec(KV_LORA), seq_spec(2 * LANES),
                _layer_spec((1, Q_LORA), layer), _layer_spec((Q_LORA, 2 * MLA_DIM), layer),
                _layer_spec((1, KV_LORA), layer), _layer_spec((KV_LORA, 2 * MLA_DIM), layer)]
    outs = [jax.ShapeDtypeStruct((nb * seq, MLA_DIM), F32)]
    out_specs = [seq_spec(MLA_DIM)]
    if has_ctx:
        cache_ckv, cache_kr4, cos4, sin4 = ctx
        past = cache_ckv.shape[2]
        ins += [cache_ckv, cache_kr4, cos4, sin4]
        in_specs += [pl.BlockSpec((None, None, past, KV_LORA), lambda b: (b, layer, 0, 0)),
                     pl.BlockSpec((None, None, past, LANES), lambda b: (b, layer, 0, 0)),
                     _const_spec((seq, LANES)), _const_spec((seq, LANES))]
    else:
        aliases = {len(ins): 1, len(ins) + 1: 2}
        ins += list(hist)
        in_specs += [pl.BlockSpec(memory_space=pl.ANY)] * 2
        outs += [jax.ShapeDtypeStruct(h.shape, F32) for h in hist]
        out_specs += [pl.BlockSpec((None, None, seq, KV_LORA), lambda b: (b, layer, 0, 0)),
                      pl.BlockSpec((None, None, seq, ROPE_DIM), lambda b: (b, layer, 0, 0))]
    return pl.pallas_call(
        functools.partial(_mla_kernel, has_ctx, seq),
        out_shape=outs, grid=(nb,), in_specs=in_specs, out_specs=out_specs,
        input_output_aliases={} if has_ctx else aliases,
        compiler_params=_cparams(("arbitrary",), VMEM_LIMIT),
        name="mla",
    )(*ins)


_FNET_ROWS = 1024


def _fnet_kernel(seq, g_ref, cbd_ref, sbd_ref, cs_ref, o_ref):
    gb = g_ref[...].astype(BF16)
    gc = _dot(gb, cbd_ref[...].astype(BF16)).astype(BF16)
    gs = _dot(gb, sbd_ref[...].astype(BF16)).astype(BF16)
    cs = cs_ref[...].astype(BF16)
    for i in range(_FNET_ROWS // seq):
        rows = slice(i * seq, (i + 1) * seq)
        o_ref[rows, :] = _dot(cs, jnp.concatenate([gc[rows], gs[rows]], axis=0))


def _dft_constants(seq):
    k = np.arange(FNET_GW)
    ang = 2.0 * np.pi * np.outer(k, k) / FNET_GW
    eye = np.eye(FNET_DIM // FNET_GW)
    cbd = np.kron(eye, np.cos(ang)) / np.sqrt(FNET_GW)
    sbd = np.kron(eye, np.sin(ang)) / np.sqrt(FNET_GW)
    s = np.arange(seq)
    angs = 2.0 * np.pi * np.outer(s, s) / seq
    cs = np.concatenate([np.cos(angs), -np.sin(angs)], axis=1) / np.sqrt(seq)
    return (jnp.asarray(cbd, F32), jnp.asarray(sbd, F32), jnp.asarray(cs, F32))


def _fnet(fin, nb, seq):
    cbd, sbd, cs = _dft_constants(seq)
    return pl.pallas_call(
        functools.partial(_fnet_kernel, seq),
        out_shape=jax.ShapeDtypeStruct((nb * seq, FNET_DIM), F32), grid=(nb * seq // _FNET_ROWS,),
        in_specs=[pl.BlockSpec((_FNET_ROWS, FNET_DIM), lambda b: (b, 0)),
                  _const_spec((FNET_DIM, FNET_DIM)), _const_spec((FNET_DIM, FNET_DIM)),
                  _const_spec((seq, 2 * seq))],
        out_specs=pl.BlockSpec((_FNET_ROWS, FNET_DIM), lambda b: (b, 0)),
        compiler_params=_cparams(("arbitrary",), VMEM_LIMIT),
        name="fnet",
    )(fin, cbd, sbd, cs)


def _out_proj_kernel(x_ref, ys_ref, ym_ref, yf_ref, mod_ref, w_ref, g_ref, b_ref, wr_ref,
                     x1_ref, u2p_ref, lg_ref, mix_a_ref, mix_b_ref):
    step = pl.program_id(0)
    bufs = (mix_a_ref, mix_b_ref)

    @pl.when(step == 0)
    def _no_previous_tile():
        bufs[1][...] = jnp.zeros(bufs[1].shape, F32)

    def stage(mix_prev_ref, mix_next_ref):
        y = jnp.concatenate([ys_ref[...].astype(BF16), ym_ref[...].astype(BF16),
                             yf_ref[...].astype(BF16)], axis=1)
        mix_next_ref[...] = _dot(y, w_ref[...])
        x1 = (_ln(DN_ALPHA * x_ref[...] + mod_ref[2:3, :] * mix_prev_ref[...]) * g_ref[...]
              + b_ref[...])
        x1_ref[...] = x1
        u2 = _ln(x1) * (1.0 + mod_ref[4:5, :]) + mod_ref[3:4, :]
        lg_ref[...] = _dot_nt(wr_ref[...], u2.astype(BF16))
        half = D_MODEL // 2
        words = pltpu.pack_elementwise([u2[:, :half], u2[:, half:]], packed_dtype=BF16)
        for j in range(PACK_SLAB):
            u2p_ref[pl.ds(j, _TM, stride=PACK_SLAB), :] = words[:, j * LANES:(j + 1) * LANES]

    for parity in range(2):
        @pl.when(step % 2 == parity)
        def _(parity=parity):
            stage(bufs[1 - parity], bufs[parity])


def _out_proj(x, yssd, ymla, yfft, mod_all, mod_rows, lp, seq, layer):
    t = x.shape[0]
    n = t // _TM
    return pl.pallas_call(
        _out_proj_kernel,
        out_shape=[jax.ShapeDtypeStruct((t, D_MODEL), F32),
                   jax.ShapeDtypeStruct((t * PACK_SLAB, LANES), U32),
                   jax.ShapeDtypeStruct((N_EXPERTS, t), F32)],
        grid=(n + 1,),
        in_specs=[_row_spec(D_MODEL, 1, n), _row_spec(SSD_DIM, 0, n), _row_spec(MLA_DIM, 0, n),
                  _row_spec(FNET_DIM, 0, n), _mod_spec(mod_rows, t, seq, layer, 1, n),
                  _layer_spec((D_MODEL, D_MODEL), layer),
                  _layer_spec((1, D_MODEL), layer), _layer_spec((1, D_MODEL), layer),
                  _layer_spec((N_EXPERTS, D_MODEL), layer)],
        out_specs=[_row_spec(D_MODEL, 1, n),
                   pl.BlockSpec((_TM * PACK_SLAB, LANES), lambda i: (_tile(i, 1, n), 0)),
                   pl.BlockSpec((N_EXPERTS, _TM), lambda i: (0, _tile(i, 1, n)))],
        scratch_shapes=[pltpu.VMEM((_TM, D_MODEL), F32)] * 2,
        compiler_params=_cparams(("arbitrary",), VMEM_LIMIT),
        name="out_proj",
    )(x, yssd, ymla, yfft, mod_all, lp["w_out"], lp["ln1_g"], lp["ln1_b"], lp["wrt"])


_RANK_SPLIT = 64.0


def _cumsum_tokens(x, triu, below):
    within = _dot(x.astype(BF16), triu)
    totals = jnp.broadcast_to(within[:, LANES - 1:LANES], within.shape)
    return within + _dot(below, totals.astype(BF16))


def _route_kernel(nblk, cap, lg_ref, idx_ref, aff_ref, p_ref, sel_ref):
    lg = lg_ref[...]
    e = jnp.exp(lg - jnp.max(lg, axis=0, keepdims=True))
    aff = e / jnp.sum(e, axis=0, keepdims=True)
    aff_ref[...] = aff
    capf = float(cap)

    def count(mask):
        s = jnp.sum(jnp.where(mask, 1.0, 0.0), axis=2, keepdims=True)
        return jnp.sum(s, axis=1, keepdims=True)

    def search(i, lo):
        cand = lo | jnp.left_shift(jnp.int32(1), 30 - i)
        return jnp.where(count(aff >= pltpu.bitcast(cand, F32)) >= capf, cand, lo)

    thr_bits = lax.fori_loop(0, 31, search, jnp.zeros((N_EXPERTS, 1, 1), I32))
    thr = pltpu.bitcast(thr_bits, F32)
    need = capf - count(aff > thr)

    rows = N_EXPERTS * nblk
    ri = lax.broadcasted_iota(I32, (LANES, LANES), 0)
    ci = lax.broadcasted_iota(I32, (LANES, LANES), 1)
    triu = jnp.where(ri <= ci, 1.0, 0.0).astype(BF16)
    rr = lax.broadcasted_iota(I32, (rows, rows), 0)
    rc = lax.broadcasted_iota(I32, (rows, rows), 1)
    below = jnp.where((rc < rr) & (rc >= (rr // nblk) * nblk), 1.0, 0.0).astype(BF16)
    flat = lambda v: v.reshape(rows, LANES)
    gt = flat(jnp.where(aff > thr, 1.0, 0.0))
    eq = flat(jnp.where(aff == thr, 1.0, 0.0))
    need_rows = flat(jnp.broadcast_to(need, aff.shape))
    eq_rank = _cumsum_tokens(eq, triu, below) - eq
    sel = gt + eq * jnp.where(eq_rank < need_rows, 1.0, 0.0)
    rank = _cumsum_tokens(sel, triu, below)
    sel_ref[...] = sel.reshape(N_EXPERTS, nblk, LANES)
    p_ref[...] = rank.reshape(N_EXPERTS, nblk, LANES)

    ones8 = jnp.ones((SUBLANES, LANES), BF16)
    bi = lax.broadcasted_iota(I32, (nblk, nblk), 0)
    bj = lax.broadcasted_iota(I32, (nblk, nblk), 1)
    triu_b = jnp.where(bi <= bj, 1.0, 0.0).astype(BF16)
    blk_ids = lax.broadcasted_iota(I32, (SUBLANES, nblk), 1).astype(F32).astype(BF16)
    r_col = lax.broadcasted_iota(I32, (cap, 1), 0).astype(F32)

    def per_expert(ex, carry):
        pe = p_ref[ex]
        tot_row = _dot_nt(ones8, sel_ref[ex].astype(BF16))[0:1, :]
        upto = _dot(jnp.broadcast_to(tot_row, (SUBLANES, nblk)).astype(BF16), triu_b)[0:1, :]
        onehot = jnp.where((upto - tot_row <= r_col) & (r_col < upto), 1.0, 0.0).astype(BF16)
        p_hi = jnp.floor(pe * (1.0 / _RANK_SPLIT))
        p_lo = pe - p_hi * _RANK_SPLIT
        ranks = _dot(onehot, p_hi.astype(BF16)) * _RANK_SPLIT + _dot(onehot, p_lo.astype(BF16))
        inside = jnp.where(ranks <= r_col, 1.0, 0.0).astype(BF16)
        cnt = _dot_nt(ones8, inside)[0:1, :] + float(LANES) * _dot_nt(blk_ids, onehot)[0:1, :]
        idx_ref[ex] = cnt.astype(I32)
        return carry

    lax.fori_loop(0, N_EXPERTS, per_expert, 0)


def _route(logits, cap):
    t = logits.shape[1]
    nblk = t // LANES
    idx, aff = pl.pallas_call(
        functools.partial(_route_kernel, nblk, cap),
        out_shape=[jax.ShapeDtypeStruct((N_EXPERTS, 1, cap), I32),
                   jax.ShapeDtypeStruct((N_EXPERTS, nblk, LANES), F32)],
        scratch_shapes=[pltpu.VMEM((N_EXPERTS, nblk, LANES), F32),
                        pltpu.VMEM((N_EXPERTS, nblk, LANES), F32)],
        compiler_params=_cparams(None, VMEM_LIMIT),
        name="route",
    )(logits.reshape(N_EXPERTS, nblk, LANES))
    return idx.reshape(N_EXPERTS, cap), aff.reshape(N_EXPERTS, t)


_UP_TF = 512
_UNROLL = 16


def _gather_rows(tile_ref, streams, ex, step, n_steps):
    row0 = 0
    for src_ref, idx_ref, cap in streams:
        cnt = cap // n_steps
        for u in range(cnt):
            r = step * cnt + u
            src = pl.multiple_of(idx_ref[ex * cap + r] * PACK_SLAB, PACK_SLAB)
            dst = pl.multiple_of((row0 + r) * PACK_SLAB, PACK_SLAB)
            tile_ref[pl.ds(dst, PACK_SLAB), :] = src_ref[pl.ds(src, PACK_SLAB), :]
        row0 += cap


def _moe_up_kernel(caps, ua_ref, ub_ref, ia_ref, ib_ref, wg_ref, wu_ref, h_ref, tile_ref, xe_ref):
    ex = pl.program_id(0)
    ff = pl.program_id(1)
    n_steps = EXPERT_FF // _UP_TF
    m = sum(caps)
    streams = ((ua_ref, ia_ref, caps[0]), (ub_ref, ib_ref, caps[1]))

    @pl.when((ex == 0) & (ff == 0))
    def _first_rows():
        def body(i, carry):
            row0 = 0
            for src_ref, idx_ref, cap in streams:
                @pl.when(i < cap // _UNROLL)
                def _(src_ref=src_ref, idx_ref=idx_ref, row0=row0):
                    for u in range(_UNROLL):
                        r = i * _UNROLL + u
                        src = pl.multiple_of(idx_ref[r] * PACK_SLAB, PACK_SLAB)
                        dst = pl.multiple_of((row0 + r) * PACK_SLAB, PACK_SLAB)
                        tile_ref[pl.ds(dst, PACK_SLAB), :] = src_ref[pl.ds(src, PACK_SLAB), :]
                row0 += cap
            return carry

        lax.fori_loop(0, max(caps) // _UNROLL, body, 0)

    @pl.when(ff == 0)
    def _unpack():
        half = D_MODEL // 2
        for j in range(PACK_SLAB):
            w = tile_ref[pl.ds(j, m, stride=PACK_SLAB), :]
            for k in range(2):
                v = pltpu.unpack_elementwise(w, index=k, packed_dtype=BF16, unpacked_dtype=F32)
                lo = k * half + j * LANES
                xe_ref[:, lo:lo + LANES] = v.astype(BF16)

    nxt = jnp.minimum(ex + 1, N_EXPERTS - 1)
    _gather_rows(tile_ref, streams, nxt, ff, n_steps)
    x = xe_ref[...]
    g = _dot(x, wg_ref[...].astype(BF16))
    u = _dot(x, wu_ref[...].astype(BF16))
    h_ref[...] = (_silu(g) * u).astype(BF16)


def _moe_up(u2p_a, u2p_b, idx_a, idx_b, w_gate, w_up, layer):
    caps = (idx_a.shape[1], idx_b.shape[1])
    m = sum(caps)
    smem = pl.BlockSpec(memory_space=pltpu.SMEM)
    w_spec = pl.BlockSpec((None, None, D_MODEL, _UP_TF), lambda e, f: (layer, e, 0, f))
    return pl.pallas_call(
        functools.partial(_moe_up_kernel, caps),
        out_shape=jax.ShapeDtypeStruct((N_EXPERTS, m, EXPERT_FF), BF16),
        grid=(N_EXPERTS, EXPERT_FF // _UP_TF),
        in_specs=[pl.BlockSpec(u2p_a.shape, lambda e, f: (0, 0), pipeline_mode=pl.Buffered(1)),
                  pl.BlockSpec(u2p_b.shape, lambda e, f: (0, 0), pipeline_mode=pl.Buffered(1)),
                  smem, smem, w_spec, w_spec],
        out_specs=pl.BlockSpec((None, m, _UP_TF), lambda e, f: (e, 0, f)),
        scratch_shapes=[pltpu.VMEM((m * PACK_SLAB, LANES), U32), pltpu.VMEM((m, D_MODEL), BF16)],
        compiler_params=_cparams(("arbitrary", "arbitrary"), VMEM_LIMIT),
        name="moe_up",
    )(u2p_a, u2p_b, idx_a.reshape(-1), idx_b.reshape(-1), w_gate, w_up)


_DOWN_TN = 512
_RMW = 16


def _scatter_rows(acc_ref, slab_ref, idx_ref, aff_ref, idx0, src0, scale, rows):
    for b0 in range(0, rows, _RMW):
        dsts, vals = [], []
        for u in range(b0, b0 + _RMW):
            tok = idx_ref[idx0 + u]
            dst = pl.multiple_of(tok * ROW_SLAB, ROW_SLAB)
            src = pl.multiple_of(src0 + u * ROW_SLAB, ROW_SLAB)
            vals.append(acc_ref[pl.ds(dst, ROW_SLAB), :]
                        + (aff_ref[tok] * scale) * slab_ref[pl.ds(src, ROW_SLAB), :])
            dsts.append(dst)
        for dst, val in zip(dsts, vals):
            acc_ref[pl.ds(dst, ROW_SLAB), :] = val


def _moe_down_kernel(cap, h_ref, wd_ref, idx_ref, affp_ref, aff_ref, out_ref, acc_ref,
                     slab_a_ref, slab_b_ref):
    ex = pl.program_id(0)
    nn = pl.program_id(1)
    n_steps = D_MODEL // _DOWN_TN
    n_half = _DOWN_TN // LANES
    part = cap // n_steps
    slabs = (slab_a_ref, slab_b_ref)

    @pl.when((ex == 0) & (nn == 0))
    def _zero():
        acc_ref[...] = jnp.zeros(acc_ref.shape, F32)
        slab_b_ref[...] = jnp.zeros(slab_b_ref.shape, F32)

    prev = jnp.maximum(ex - 1, 0)
    live = jnp.where(ex > 0, 1.0, 0.0)
    row0 = nn * part

    def stage(slab_prev_ref, slab_next_ref):
        _scatter_rows(acc_ref, slab_prev_ref, idx_ref, affp_ref, prev * cap + row0,
                      row0 * ROW_SLAB, live, part)
        ye = _dot(h_ref[...], wd_ref[...].astype(BF16))
        for j in range(n_half):
            slab_next_ref[pl.ds(nn * n_half + j, cap, stride=ROW_SLAB), :] = (
                ye[:, j * LANES:(j + 1) * LANES])

    for parity in range(2):
        @pl.when(ex % 2 == parity)
        def _(parity=parity):
            stage(slabs[1 - parity], slabs[parity])

    @pl.when((ex == N_EXPERTS - 1) & (nn == n_steps - 1))
    def _tail():
        def body(i, carry):
            _scatter_rows(acc_ref, slabs[(N_EXPERTS - 1) % 2], idx_ref, aff_ref,
                          ex * cap + i * _RMW, i * (_RMW * ROW_SLAB), 1.0, _RMW)
            return carry

        lax.fori_loop(0, cap // _RMW, body, 0)
        pltpu.sync_copy(acc_ref, out_ref)


def _moe_down(h, w_down, idx, aff, row_block, layer):
    cap = idx.shape[1]
    t = aff.shape[1]
    aff3 = aff[:, None, :]
    gate_spec = lambda shift: pl.BlockSpec(
        (None, None, t), lambda e, n: (jnp.maximum(e - shift, 0), 0, 0), memory_space=pltpu.SMEM)
    return pl.pallas_call(
        functools.partial(_moe_down_kernel, cap),
        out_shape=jax.ShapeDtypeStruct((t * ROW_SLAB, LANES), F32),
        grid=(N_EXPERTS, D_MODEL // _DOWN_TN),
        in_specs=[pl.BlockSpec((None, cap, EXPERT_FF), lambda e, n: (e, row_block, 0)),
                  pl.BlockSpec((None, None, EXPERT_FF, _DOWN_TN), lambda e, n: (layer, e, 0, n)),
                  pl.BlockSpec(memory_space=pltpu.SMEM), gate_spec(1), gate_spec(0)],
        out_specs=pl.BlockSpec(memory_space=pl.ANY),
        scratch_shapes=[pltpu.VMEM((t * ROW_SLAB, LANES), F32),
                        pltpu.VMEM((cap * ROW_SLAB, LANES), F32),
                        pltpu.VMEM((cap * ROW_SLAB, LANES), F32)],
        compiler_params=_cparams(("arbitrary", "arbitrary"), VMEM_LIMIT),
        name="moe_down",
    )(h, w_down, idx.reshape(-1), aff3, aff3)


def _swap_pairs(w):
    s = w.shape
    return jnp.flip(w.reshape(s[:-1] + (s[-1] // 2, 2)), axis=-1).reshape(s)


def _pack_w_in(w_in):
    o_dt, o_cq, o_kr, o_fin = 1280, 1296, 1680, 1712
    kr = w_in[:, :, o_kr:o_fin]
    pad = jnp.zeros(w_in.shape[:2] + (_IN_KR4 - _IN_SMALL - 2 * SSD_HEADS,), w_in.dtype)
    parts = [w_in[:, :, :o_dt], w_in[:, :, o_cq:o_kr], w_in[:, :, o_fin:], w_in[:, :, o_dt:o_cq], pad,
             jnp.tile(kr, (1, 1, MLA_HEADS)), jnp.tile(_swap_pairs(kr), (1, 1, MLA_HEADS))]
    return jnp.concatenate(parts, axis=2).astype(BF16)


def _pack_w_uq(w_uq):
    w = w_uq.reshape(w_uq.shape[:2] + (MLA_HEADS, QK_NOPE + ROPE_DIM))
    flat = lambda v: v.reshape(v.shape[:2] + (-1,))
    rope = w[..., QK_NOPE:]
    return jnp.concatenate([flat(w[..., :QK_NOPE]), flat(rope), flat(_swap_pairs(rope))],
                           axis=2).astype(BF16)


def _pack_w_ukv(w_ukv):
    w = w_ukv.reshape(w_ukv.shape[:2] + (MLA_HEADS, QK_NOPE + V_DIM))
    flat = lambda v: v.reshape(v.shape[:2] + (-1,))
    return jnp.concatenate([flat(w[..., :QK_NOPE]), flat(w[..., QK_NOPE:])], axis=2).astype(BF16)


def _rope_tables(seq):
    rows = seq // GRID_W
    r, col = jnp.meshgrid(jnp.arange(rows, dtype=F32), jnp.arange(GRID_W, dtype=F32), indexing="ij")
    pairs = ROPE_DIM // 4
    inv = ROPE_BASE ** (-jnp.arange(pairs, dtype=F32) / pairs)
    ang = jnp.concatenate([r.reshape(-1, 1) * inv, col.reshape(-1, 1) * inv], axis=-1)
    cos = jnp.repeat(jnp.cos(ang), 2, axis=1)
    sin = jnp.repeat(jnp.sin(ang), 2, axis=1) * jnp.tile(jnp.array([-1.0, 1.0], F32), ROPE_DIM // 2)
    return jnp.tile(cos, (1, MLA_HEADS)), jnp.tile(sin, (1, MLA_HEADS))


def _pad_lanes(v):
    return jnp.pad(v, ((0, 0), (0, LANES - v.shape[1])))[:, None, :]


def kernel(x_prompt, x_sample, cache_ckv, cache_krope, state_ssm, c, c_ctx,
           w_in, conv_w, conv_b, dt_bias, a_log, d_skip, ssd_norm_g,
           q_norm_g, w_uq, kv_norm_g, w_ukv, w_out, w_mod, b_mod,
           ln1_g, ln1_b, ln2_g, ln2_b, w_router, w_gate, w_up, w_down):
    nb_c, seq_c, _ = x_prompt.shape
    nb_l, seq_l, _ = x_sample.shape
    t_c, t_l = nb_c * seq_c, nb_l * seq_l
    caps = (CAPACITY_FACTOR * t_c // N_EXPERTS, CAPACITY_FACTOR * t_l // N_EXPERTS)

    w_in_p = _pack_w_in(w_in)
    wq = _pack_w_uq(w_uq)
    wkv = _pack_w_ukv(w_ukv)
    row = lambda v: v[:, None, :]
    lp = dict(conv_w=conv_w, conv_b=row(conv_b), dt_bias=_pad_lanes(dt_bias.reshape(DEPTH, -1)),
              a_log=_pad_lanes(a_log.reshape(DEPTH, -1)),
              d_skip=row(jnp.repeat(d_skip, SSD_HEADDIM, axis=1)), ssd_norm_g=row(ssd_norm_g),
              q_norm_g=row(q_norm_g), wq=wq, kv_norm_g=row(kv_norm_g), wkv=wkv,
              w_out=w_out.astype(BF16), ln1_g=row(ln1_g), ln1_b=row(ln1_b),
              wrt=jnp.swapaxes(w_router, 1, 2).astype(BF16))
    ln2 = (row(ln2_g), row(ln2_b))
    cache_kr4 = jnp.tile(cache_krope, (1, 1, 1, MLA_HEADS))
    cos4, sin4 = _rope_tables(seq_l)

    cond = jnp.zeros((SUBLANES, D_MODEL), F32).at[0].set(c_ctx).at[1:1 + nb_l].set(c)
    mod_all = _modulation(cond, w_mod, b_mod).reshape(DEPTH, SUBLANES, 6, D_MODEL)

    streams = [
        dict(x=x_prompt.reshape(t_c, D_MODEL), nb=nb_c, seq=seq_c, cap=caps[0], mod_rows=(0, 1)),
        dict(x=x_sample.reshape(t_l, D_MODEL), nb=nb_l, seq=seq_l, cap=caps[1], mod_rows=(1, nb_l)),
    ]
    prev = [None, None]
    ssm_hist = jnp.zeros((nb_c, DEPTH) + state_ssm.shape[2:], F32)
    kv_hist = (jnp.zeros((nb_c, DEPTH, seq_c, KV_LORA), F32),
               jnp.zeros((nb_c, DEPTH, seq_c, ROPE_DIM), F32))
    for l in range(DEPTH):
        routed = []
        for si, st in enumerate(streams):
            nb, seq, mod_rows = st["nb"], st["seq"], st["mod_rows"]
            x, z, xbc, cq, ckv, fin, small, kr = _in_proj(
                st["x"], prev[si], mod_all, mod_rows, w_in_p, seq, l)
            if si == 0:
                yssd, ssm_hist = _ssd(z, xbc, small, lp, nb, seq, None, ssm_hist, l)
                ymla, *kv_hist = _mla(cq, ckv, kr, lp, nb, seq, None, kv_hist, l)
            else:
                yssd, _ = _ssd(z, xbc, small, lp, nb, seq, state_ssm, None, l)
                (ymla,) = _mla(cq, ckv, kr, lp, nb, seq, (cache_ckv, cache_kr4, cos4, sin4), None, l)
            yfft = _fnet(fin, nb, seq)
            x1, u2p, logits = _out_proj(x, yssd, ymla, yfft, mod_all, mod_rows, lp, seq, l)
            idx, aff = _route(logits, st["cap"])
            st["x"] = x1
            routed.append((u2p, idx, aff))
        h = _moe_up(routed[0][0], routed[1][0], routed[0][1], routed[1][1], w_gate, w_up, l)
        for si in range(2):
            moe = _moe_down(h, w_down, routed[si][1], routed[si][2], 0 if si == 0 else 2, l)
            prev[si] = (moe,) + ln2
    last = DEPTH - 1
    y_p, y_s = [_final_norm(st["x"], *prev[si], mod_all, st["mod_rows"], st["seq"], last)
                for si, st in enumerate(streams)]
    y_p, y_s = y_p.reshape(x_prompt.shape), y_s.reshape(x_sample.shape)
    return (y_p, y_s, kv_hist[0], kv_hist[1], ssm_hist)
```

```python
import functools

import jax
import jax.numpy as jnp
import numpy as np
from jax import lax
from jax.experimental import pallas as pl
from jax.experimental.pallas import tpu as pltpu

F32 = jnp.float32
BF16 = jnp.bfloat16
I32 = jnp.int32
U32 = jnp.uint32

D_MODEL = 1024
DEPTH = 4
GRID_W = 64
SSD_HEADS = 8
SSD_HEADDIM = 64
SSD_DIM = SSD_HEADS * SSD_HEADDIM
SSD_STATE = 64
SSD_CONV = 5
SSD_CHUNK = 128
SSD_CONV_DIM = SSD_DIM + 4 * SSD_STATE
MLA_HEADS = 4
Q_LORA = 256
KV_LORA = 128
QK_NOPE = 64
ROPE_DIM = 32
V_DIM = 64
MLA_DIM = MLA_HEADS * V_DIM
MLA_SCALE = (QK_NOPE + ROPE_DIM) ** -0.5
ROPE_BASE = 10000.0
Q_TILE = 256
FNET_GW = 64
FNET_DIM = 256
N_EXPERTS = 16
EXPERT_FF = 1024
CAPACITY_FACTOR = 2
DN_ALPHA = (2 * DEPTH) ** 0.25
EPS = 1e-5
_LOG2E = 1.4426950408889634

LANES = 128
SUBLANES = 8
ROW_SLAB = D_MODEL // LANES
PACK_SLAB = ROW_SLAB // 2
VMEM_LIMIT = 56 * 1024 * 1024

_IN_Z = 0
_IN_XBC = 512
_IN_CQ = 1280
_IN_CKV = 1536
_IN_FIN = 1664
_IN_SMALL = 1920
_IN_KR4 = 2048
IN_PAD = 2304
_IN_WIDTHS = (512, 768, 256, 128, 256, 128, 256)


def _cparams(sem, vmem=None):
    return pltpu.CompilerParams(dimension_semantics=sem, vmem_limit_bytes=vmem)


def _ln(x):
    mu = jnp.mean(x, axis=-1, keepdims=True)
    xc = x - mu
    var = jnp.mean(xc * xc, axis=-1, keepdims=True)
    return xc * lax.rsqrt(var + EPS)


def _rms(x):
    return x * lax.rsqrt(jnp.mean(x * x, axis=-1, keepdims=True) + EPS)


def _silu(x):
    return x * (1.0 / (1.0 + jnp.exp(-x)))


def _dot(a, b):
    return jnp.dot(a, b, preferred_element_type=F32)


def _dot_nt(a, b):
    return lax.dot_general(a, b, (((1,), (1,)), ((), ())), preferred_element_type=F32)


def _split3(x):
    hi = x.astype(BF16)
    r1 = x - hi.astype(F32)
    mid = r1.astype(BF16)
    lo = (r1 - mid.astype(F32)).astype(BF16)
    return hi, mid, lo


def _const_spec(shape):
    return pl.BlockSpec(shape, lambda *_: (0,) * len(shape))


def _layer_spec(shape, layer):
    return pl.BlockSpec((None,) + shape, lambda *_: (layer,) + (0,) * len(shape))


_MOD_TN = 1536


def _mod_kernel(c_ref, w_ref, b_ref, o_ref):
    a = _silu(c_ref[...]).astype(BF16)
    o_ref[...] = _dot(a, w_ref[...].astype(BF16)) + b_ref[...]


def _modulation(cond, w_mod, b_mod):
    n = 6 * D_MODEL
    return pl.pallas_call(
        _mod_kernel,
        out_shape=jax.ShapeDtypeStruct((DEPTH, SUBLANES, n), F32),
        grid=(DEPTH, n // _MOD_TN),
        in_specs=[
            pl.BlockSpec((SUBLANES, D_MODEL), lambda l, j: (0, 0)),
            pl.BlockSpec((None, D_MODEL, _MOD_TN), lambda l, j: (l, 0, j)),
            pl.BlockSpec((None, 1, _MOD_TN), lambda l, j: (l, 0, j)),
        ],
        out_specs=pl.BlockSpec((None, SUBLANES, _MOD_TN), lambda l, j: (l, 0, j)),
        compiler_params=_cparams(("arbitrary", "arbitrary")),
        name="modulation",
    )(cond, w_mod, b_mod.reshape(DEPTH, 1, n))


_TM = 512


def _read_rows(slab_ref, tm):
    return jnp.concatenate(
        [slab_ref[pl.ds(j, tm, stride=ROW_SLAB), :] for j in range(ROW_SLAB)], axis=-1)


def _post_norm2(x_ref, moe_ref, pmod_ref, g_ref, b_ref):
    moe = _read_rows(moe_ref, _TM)
    return _ln(DN_ALPHA * x_ref[...] + pmod_ref[5:6, :] * moe) * g_ref[...] + b_ref[...]


def _in_proj_kernel(first, *refs):
    if first:
        x_ref, mod_ref, w_ref = refs[:3]
        outs = refs[3:]
        x = x_ref[...]
    else:
        x_ref, moe_ref, pmod_ref, g_ref, b_ref, mod_ref, w_ref, xo_ref = refs[:8]
        outs = refs[8:]
        x = _post_norm2(x_ref, moe_ref, pmod_ref, g_ref, b_ref)
        xo_ref[...] = x
    u = _ln(x) * (1.0 + mod_ref[1:2, :]) + mod_ref[0:1, :]
    p = _dot(u.astype(BF16), w_ref[...])
    off = 0
    for o_ref, w in zip(outs, _IN_WIDTHS):
        o_ref[...] = p[:, off:off + w]
        off += w


def _mod_spec(mod_rows, t, seq, layer):
    row0, count = mod_rows
    per = seq // _TM if count > 1 else t // _TM
    return pl.BlockSpec((None, None, 6, D_MODEL), lambda i: (layer, row0 + i // per, 0, 0))


def _row_spec(w):
    return pl.BlockSpec((_TM, w), lambda i: (i, 0))


_SLAB_SPEC = pl.BlockSpec((_TM * ROW_SLAB, LANES), lambda i: (i, 0))


def _in_proj(x, prev, mod_all, mod_rows, w_in_p, seq, layer):
    t = x.shape[0]
    mod_spec = _mod_spec(mod_rows, t, seq, layer)
    outs = [jax.ShapeDtypeStruct((t, w), F32) for w in _IN_WIDTHS]
    out_specs = [_row_spec(w) for w in _IN_WIDTHS]
    w_spec = _layer_spec((D_MODEL, IN_PAD), layer)
    if prev is None:
        ins = [x, mod_all, w_in_p]
        in_specs = [_row_spec(D_MODEL), mod_spec, w_spec]
    else:
        moe_slab, g, b = prev
        ins = [x, moe_slab, mod_all, g, b, mod_all, w_in_p]
        in_specs = [_row_spec(D_MODEL), _SLAB_SPEC, _mod_spec(mod_rows, t, seq, layer - 1),
                    _layer_spec((1, D_MODEL), layer - 1), _layer_spec((1, D_MODEL), layer - 1),
                    mod_spec, w_spec]
        outs = [jax.ShapeDtypeStruct((t, D_MODEL), F32)] + outs
        out_specs = [_row_spec(D_MODEL)] + out_specs
    res = pl.pallas_call(
        functools.partial(_in_proj_kernel, prev is None),
        out_shape=outs, grid=(t // _TM,), in_specs=in_specs, out_specs=out_specs,
        compiler_params=_cparams(("arbitrary",), VMEM_LIMIT),
        name="in_proj",
    )(*ins)
    if prev is None:
        return (x,) + tuple(res)
    return tuple(res)


def _final_norm_kernel(x_ref, moe_ref, pmod_ref, g_ref, b_ref, o_ref):
    o_ref[...] = _post_norm2(x_ref, moe_ref, pmod_ref, g_ref, b_ref)


def _final_norm(x, moe_slab, g, b, mod_all, mod_rows, seq, layer):
    t = x.shape[0]
    return pl.pallas_call(
        _final_norm_kernel,
        out_shape=jax.ShapeDtypeStruct((t, D_MODEL), F32), grid=(t // _TM,),
        in_specs=[_row_spec(D_MODEL), _SLAB_SPEC, _mod_spec(mod_rows, t, seq, layer),
                  _layer_spec((1, D_MODEL), layer), _layer_spec((1, D_MODEL), layer)],
        out_specs=_row_spec(D_MODEL),
        compiler_params=_cparams(("arbitrary",)),
        name="final_norm",
    )(x, moe_slab, mod_all, g, b)


def _ssd_kernel(has_h0, write_h, seq, *refs):
    refs = list(refs)
    z_ref, xbc_ref, sm_ref, cw_ref, cb_ref, dtb_ref, alog_ref, dsk_ref, ng_ref = refs[:9]
    pos = 9
    if has_h0:
        h0_ref = refs[pos]
    pos += 1
    y_ref = refs[pos]
    pos += 1
    if write_h:
        hf_ref = refs[pos]
        pos += 1
    pad_ref, xs_ref, cm_ref, gm_ref, bt_ref, acol_ref, rows_ref, st_ref = refs[pos:]
    ck = SSD_CHUNK
    nc = seq // ck
    halo = SUBLANES
    nh = SSD_HEADS
    nh2 = 2 * nh

    pad_ref[0:halo, :] = jnp.zeros((halo, SSD_CONV_DIM), F32)
    pad_ref[halo + seq:2 * halo + seq, :] = jnp.zeros((halo, SSD_CONV_DIM), F32)
    pad_ref[halo:halo + seq, :] = xbc_ref[...]
    ri = lax.broadcasted_iota(I32, (ck, ck), 0)
    ci = lax.broadcasted_iota(I32, (ck, ck), 1)
    triu = jnp.where(ri <= ci, 1.0, 0.0).astype(BF16)
    lo_half = ci < SSD_STATE
    lo64 = lax.broadcasted_iota(I32, (SSD_STATE, LANES), 1) < SSD_STATE
    zeros_half = jnp.zeros((SSD_STATE, LANES), BF16)
    fwd_rows = lax.broadcasted_iota(I32, (nh2, ck), 0) < nh

    def prepare_chunk(c, carry):
        cs = pl.multiple_of(c * ck, ck)
        rows = pl.ds(cs, ck)
        cw_blk = 2 * LANES
        for cb in range(SSD_CONV_DIM // cw_blk):
            cols = slice(cb * cw_blk, (cb + 1) * cw_blk)
            win = pad_ref[pl.ds(cs, ck + 2 * halo), cols]
            acc = jnp.zeros((ck, cw_blk), F32) + cb_ref[:, cols]
            for k in range(SSD_CONV):
                s0 = halo + k - (SSD_CONV - 1) // 2
                acc = acc + win[s0:s0 + ck, :] * cw_ref[k:k + 1, cols]
            act = _silu(acc)
            if cb * cw_blk < SSD_DIM:
                xs_ref[rows, cols] = act
                y_ref[rows, cols] = act * dsk_ref[:, cols]
        bm = act[:, :LANES]
        cm = act[:, LANES:]
        cm_ref[rows, :] = cm
        bt_ref[rows, :] = bm.T
        cg2 = jnp.concatenate([jnp.where(lo_half, cm, 0.0), jnp.where(lo_half, 0.0, cm)], axis=0)
        gm2 = _dot_nt(cg2.astype(BF16), bm.astype(BF16))
        gm_ref[rows, 0:LANES] = gm2[:ck]
        gm_ref[rows, LANES:] = gm2[ck:]

        xr = sm_ref[rows, :] + dtb_ref[...]
        dt = jnp.maximum(xr, 0.0) + jnp.log(1.0 + jnp.exp(-jnp.abs(xr)))
        ad = jnp.where(ci < nh2, dt * -jnp.exp(alog_ref[...]),
                       jnp.where(ci < 2 * nh2, pltpu.roll(dt, nh2, axis=1), 0.0))
        adt = ad.T
        at16 = adt[0:nh2, :]
        dt16 = adt[nh2:2 * nh2, :]
        csum = _dot(jnp.concatenate(_split3(at16), axis=0), triu)
        arow = csum[0:nh2] + csum[nh2:2 * nh2] + csum[2 * nh2:3 * nh2]
        acol = jnp.concatenate([arow, jnp.zeros((ck - nh2, ck), F32)], axis=0).T
        alast = jnp.broadcast_to(arow[:, ck - 1:ck], (nh2, ck)) * _LOG2E
        arow_s = jnp.where(fwd_rows, arow, arow - at16) * _LOG2E
        ldt = jnp.log2(dt16)
        rowarg = jnp.where(fwd_rows, arow_s - ldt, arow_s + ldt)
        wrow = jnp.exp2(jnp.where(fwd_rows, alast - arow_s, arow_s))
        acol_ref[rows, :] = jnp.where(ci < nh, acol, acol - ad) * _LOG2E
        rows_ref[pl.ds(pl.multiple_of(c * (4 * nh2), 4 * nh2), 4 * nh2), :] = jnp.concatenate(
            [rowarg, wrow * dt16, jnp.exp2(alast), alast], axis=0)
        return carry

    lax.fori_loop(0, nc, prepare_chunk, 0)

    if has_h0:
        for d in range(2):
            h0 = h0_ref[d].reshape(SSD_DIM, SSD_STATE)
            h0 = jnp.concatenate([h0, jnp.zeros((SSD_DIM, LANES - SSD_STATE), F32)], axis=1)
            st_ref[d] = h0.T[:SSD_STATE, :]
    else:
        st_ref[...] = jnp.zeros(st_ref.shape, F32)

    def chunk_dir(d, c):
        cs = pl.multiple_of(c * ck, ck)
        acol = acol_ref[pl.ds(cs, ck), :]
        rws = rows_ref[pl.ds(pl.multiple_of(c * (4 * nh2), 4 * nh2), 4 * nh2), :]
        cm = cm_ref[pl.ds(cs, ck), :]
        cg = [jnp.where(lo_half, cm, 0.0), jnp.where(lo_half, 0.0, cm)]
        for q in range(nh // 2):
            g = q // 2
            gmat = gm_ref[pl.ds(cs, ck), g * LANES:(g + 1) * LANES]
            btg = bt_ref[pl.ds(cs + g * SSD_STATE, SSD_STATE), :]
            lhs_rows = []
            decl = []
            for hh in range(2):
                hi = d * nh + 2 * q + hh
                acol_b = jnp.broadcast_to(acol[:, hi:hi + 1], (ck, ck))
                rowarg = rws[hi:hi + 1, :]
                if d == 0:
                    lmat = jnp.where(ri >= ci, jnp.exp2(acol_b - rowarg), 0.0)
                    ecol = jnp.exp2(acol_b)
                else:
                    lmat = jnp.where(ci >= ri, jnp.exp2(rowarg - acol_b), 0.0)
                    ecol = jnp.exp2(rws[3 * nh2 + hi:3 * nh2 + hi + 1, :] - acol_b)
                mm = (gmat * lmat).astype(BF16)
                cs_h = (cg[g] * ecol).astype(BF16)
                bw = (btg * rws[nh2 + hi:nh2 + hi + 1, :]).astype(BF16)
                lhs_rows.append(jnp.concatenate([mm, cs_h], axis=1))
                lhs_rows.append(jnp.concatenate([bw, zeros_half], axis=1))
                decl.append(rws[2 * nh2 + hi:2 * nh2 + hi + 1, :])
            lhs = jnp.concatenate(lhs_rows, axis=0)
            cols = slice(q * LANES, (q + 1) * LANES)
            st_pair = st_ref[d, :, cols]
            stb = st_pair.astype(BF16)
            w_st = jnp.concatenate([stb, zeros_half] if g == 0 else [zeros_half, stb], axis=0)
            xs_pair = xs_ref[pl.ds(cs, ck), cols].astype(BF16)
            out = _dot(lhs, jnp.concatenate([xs_pair, w_st], axis=0))
            r2 = ck + SSD_STATE
            y_pair = jnp.where(lo_half, out[0:ck], out[r2:r2 + ck])
            st_new = jnp.where(lo64, out[ck:r2], out[r2 + ck:2 * r2])
            dec = jnp.where(lo64, decl[0], decl[1])
            st_ref[d, :, cols] = st_pair * dec + st_new
            y_ref[pl.ds(cs, ck), cols] += y_pair

    def body(c, carry):
        chunk_dir(0, c)
        chunk_dir(1, nc - 1 - c)
        return carry

    lax.fori_loop(0, nc, body, 0)

    def gate_chunk(c, carry):
        rows = pl.ds(pl.multiple_of(c * ck, ck), ck)
        y = y_ref[rows, :] * _silu(z_ref[rows, :])
        y_ref[rows, :] = _rms(y) * ng_ref[...]
        return carry

    lax.fori_loop(0, nc, gate_chunk, 0)

    if write_h:
        for d in range(2):
            st = jnp.concatenate([st_ref[d], jnp.zeros((LANES - SSD_STATE, SSD_DIM), F32)], axis=0)
            hf_ref[d] = st.T[:, :SSD_STATE].reshape(SSD_HEADS, SSD_HEADDIM, SSD_STATE)


def _ssd(z, xbc, small, lp, nb, seq, h0, hist, layer):
    has_h0 = h0 is not None
    write_h = not has_h0
    seq_spec = lambda w: pl.BlockSpec((seq, w), lambda b: (b, 0))
    ins = [z, xbc, small, lp["conv_w"], lp["conv_b"], lp["dt_bias"], lp["a_log"], lp["d_skip"],
           lp["ssd_norm_g"]]
    in_specs = [seq_spec(SSD_DIM), seq_spec(SSD_CONV_DIM), seq_spec(LANES),
                _layer_spec((SSD_CONV, SSD_CONV_DIM), layer), _layer_spec((1, SSD_CONV_DIM), layer),
                _layer_spec((1, LANES), layer), _layer_spec((1, LANES), layer),
                _layer_spec((1, SSD_DIM), layer), _layer_spec((1, SSD_DIM), layer)]
    hshape = (2, SSD_HEADS, SSD_HEADDIM, SSD_STATE)
    layer_block = pl.BlockSpec((None, None) + hshape, lambda b: (b, layer, 0, 0, 0, 0))
    outs = [jax.ShapeDtypeStruct((nb * seq, SSD_DIM), F32)]
    out_specs = [seq_spec(SSD_DIM)]
    aliases = {}
    if has_h0:
        ins.append(h0)
        in_specs.append(layer_block)
    else:
        ins.append(hist)
        in_specs.append(pl.BlockSpec(memory_space=pl.ANY))
        aliases = {len(ins) - 1: 1}
        outs.append(jax.ShapeDtypeStruct(hist.shape, F32))
        out_specs.append(layer_block)
    res = pl.pallas_call(
        functools.partial(_ssd_kernel, has_h0, write_h, seq),
        out_shape=outs, grid=(nb,), in_specs=in_specs, out_specs=out_specs,
        input_output_aliases=aliases,
        scratch_shapes=[pltpu.VMEM((seq + 2 * SUBLANES, SSD_CONV_DIM), F32),
                        pltpu.VMEM((seq, SSD_DIM), F32),
                        pltpu.VMEM((seq, LANES), F32),
                        pltpu.VMEM((seq, 2 * LANES), F32),
                        pltpu.VMEM((seq, LANES), F32),
                        pltpu.VMEM((seq, LANES), F32),
                        pltpu.VMEM((seq // SSD_CHUNK * 8 * SSD_HEADS, LANES), F32),
                        pltpu.VMEM((2, SSD_STATE, SSD_DIM), F32)],
        compiler_params=_cparams(("arbitrary",), VMEM_LIMIT),
        name="ssd",
    )(*ins)
    return res if write_h else (res[0], None)


_MLA_GROUP_ROWS = 1024


def _mla_kernel(has_ctx, seq, group, *refs):
    refs = list(refs)
    cq_ref, ckv_ref, kr_ref, qg_ref, wq_ref, kvg_ref, wkv_ref = refs[:7]
    pos = 7
    if has_ctx:
        cckv_ref, ckr_ref, cos_ref, sin_ref = refs[pos:pos + 4]
        pos += 4
    else:
        pos += 2
    y_ref = refs[pos]
    pos += 1
    if not has_ctx:
        ckvo_ref, kro_ref = refs[pos:pos + 2]

    qn = (_rms(cq_ref[...]) * qg_ref[...]).astype(BF16)
    qall = _dot(qn, wq_ref[...])
    q_nope = qall[:, :MLA_DIM]
    q_rope = qall[:, MLA_DIM:MLA_DIM + LANES]
    ckv_n = _rms(ckv_ref[...]) * kvg_ref[...]
    kr4 = kr_ref[:, :LANES]
    if has_ctx:
        q_rope = q_rope * cos_ref[...] + qall[:, MLA_DIM + LANES:] * sin_ref[...]
        kr4 = kr4 * cos_ref[...] + kr_ref[:, LANES:] * sin_ref[...]
        ckv_all = jnp.concatenate([cckv_ref[...], ckv_n], axis=0)
        kr_all = jnp.concatenate([ckr_ref[...], kr4], axis=0)
    else:
        ckvo_ref[...] = ckv_n.reshape(group, seq, KV_LORA)
        kro_ref[...] = kr4[:, :ROPE_DIM].reshape(group, seq, ROPE_DIM)
        ckv_all = ckv_n
        kr_all = kr4
    kv = _dot(ckv_all.astype(BF16), wkv_ref[...])
    krb = kr_all.astype(BF16)
    lane = lax.broadcasted_iota(I32, (Q_TILE, LANES), 1)
    lo_half = lane < QK_NOPE
    n_keys = kv.shape[0] // group
    for qt in range(group * seq // Q_TILE):
        rows = slice(qt * Q_TILE, (qt + 1) * Q_TILE)
        s0 = (qt * Q_TILE // seq) * n_keys
        keys = slice(s0, s0 + n_keys)
        qr = q_rope[rows] * MLA_SCALE
        for pair in range(MLA_HEADS // 2):
            cols = slice(pair * LANES, (pair + 1) * LANES)
            kcat = jnp.concatenate([kv[keys, cols].astype(BF16), krb[keys]], axis=1)
            vb = kv[keys, MLA_DIM + pair * LANES:MLA_DIM + (pair + 1) * LANES].astype(BF16)
            qn_pair = q_nope[rows, cols] * MLA_SCALE
            outs = []
            for hh in range(2):
                h = 2 * pair + hh
                qa = jnp.where(lo_half if hh == 0 else jnp.logical_not(lo_half), qn_pair, 0.0)
                qb = jnp.where((lane >= h * ROPE_DIM) & (lane < (h + 1) * ROPE_DIM), qr, 0.0)
                s = _dot_nt(jnp.concatenate([qa, qb], axis=1).astype(BF16), kcat)
                p = jnp.exp(s - jnp.max(s, axis=-1, keepdims=True))
                o = _dot(p.astype(BF16), vb)
                outs.append(o / jnp.sum(p, axis=-1, keepdims=True))
            y_ref[rows, cols] = jnp.where(lo_half, outs[0], outs[1])


def _mla(cq, ckv, kr, lp, nb, seq, ctx, hist, layer):
    has_ctx = ctx is not None
    group = 1 if has_ctx else _MLA_GROUP_ROWS // seq
    seq_spec = lambda w: pl.BlockSpec((group * seq, w), lambda b: (b, 0))
    ins = [cq, ckv, kr, lp["q_norm_g"], lp["wq"], lp["kv_norm_g"], lp["wkv"]]
    in_specs = [seq_spec(Q_LORA), seq_spec(KV_LORA), seq_spec(2 * LANES),
                _layer_spec((1, Q_LORA), layer), _layer_spec((Q_LORA, 2 * MLA_DIM), layer),
                _layer_spec((1, KV_LORA), layer), _layer_spec((KV_LORA, 2 * MLA_DIM), layer)]
    outs = [jax.ShapeDtypeStruct((nb * seq, MLA_DIM), F32)]
    out_specs = [seq_spec(MLA_DIM)]
    if has_ctx:
        cache_ckv, cache_kr4, cos4, sin4 = ctx
        past = cache_ckv.shape[2]
        ins += [cache_ckv, cache_kr4, cos4, sin4]
        in_specs += [pl.BlockSpec((None, None, past, KV_LORA), lambda b: (b, layer, 0, 0)),
                     pl.BlockSpec((None, None, past, LANES), lambda b: (b, layer, 0, 0)),
                     _const_spec((seq, LANES)), _const_spec((seq, LANES))]
    else:
        aliases = {len(ins): 1, len(ins) + 1: 2}
        ins += list(hist)
        in_specs += [pl.BlockSpec(memory_space=pl.ANY)] * 2
        outs += [jax.ShapeDtypeStruct(h.shape, F32) for h in hist]
        out_specs += [pl.BlockSpec((group, None, seq, KV_LORA), lambda b: (b, layer, 0, 0)),
                      pl.BlockSpec((group, None, seq, ROPE_DIM), lambda b: (b, layer, 0, 0))]
    return pl.pallas_call(
        functools.partial(_mla_kernel, has_ctx, seq, group),
        out_shape=outs, grid=(nb // group,), in_specs=in_specs, out_specs=out_specs,
        input_output_aliases={} if has_ctx else aliases,
        compiler_params=_cparams(("arbitrary",), VMEM_LIMIT),
        name="mla",
    )(*ins)


_FNET_ROWS = 1024


def _fnet_kernel(seq, g_ref, cbd_ref, sbd_ref, cs_ref, o_ref):
    gb = g_ref[...].astype(BF16)
    gc = _dot(gb, cbd_ref[...].astype(BF16)).astype(BF16)
    gs = _dot(gb, sbd_ref[...].astype(BF16)).astype(BF16)
    cs = cs_ref[...].astype(BF16)
    for i in range(_FNET_ROWS // seq):
        rows = slice(i * seq, (i + 1) * seq)
        o_ref[rows, :] = _dot(cs, jnp.concatenate([gc[rows], gs[rows]], axis=0))


def _dft_constants(seq):
    k = np.arange(FNET_GW)
    ang = 2.0 * np.pi * np.outer(k, k) / FNET_GW
    eye = np.eye(FNET_DIM // FNET_GW)
    cbd = np.kron(eye, np.cos(ang)) / np.sqrt(FNET_GW)
    sbd = np.kron(eye, np.sin(ang)) / np.sqrt(FNET_GW)
    s = np.arange(seq)
    angs = 2.0 * np.pi * np.outer(s, s) / seq
    cs = np.concatenate([np.cos(angs), -np.sin(angs)], axis=1) / np.sqrt(seq)
    return (jnp.asarray(cbd, F32), jnp.asarray(sbd, F32), jnp.asarray(cs, F32))


def _fnet(fin, nb, seq):
    cbd, sbd, cs = _dft_constants(seq)
    return pl.pallas_call(
        functools.partial(_fnet_kernel, seq),
        out_shape=jax.ShapeDtypeStruct((nb * seq, FNET_DIM), F32), grid=(nb * seq // _FNET_ROWS,),
        in_specs=[pl.BlockSpec((_FNET_ROWS, FNET_DIM), lambda b: (b, 0)),
                  _const_spec((FNET_DIM, FNET_DIM)), _const_spec((FNET_DIM, FNET_DIM)),
                  _const_spec((seq, 2 * seq))],
        out_specs=pl.BlockSpec((_FNET_ROWS, FNET_DIM), lambda b: (b, 0)),
        compiler_params=_cparams(("arbitrary",), VMEM_LIMIT),
        name="fnet",
    )(fin, cbd, sbd, cs)


def _out_proj_kernel(x_ref, ys_ref, ym_ref, yf_ref, mod_ref, w_ref, g_ref, b_ref, wr_ref,
                     x1_ref, u2p_ref, lg_ref):
    y = jnp.concatenate([ys_ref[...].astype(BF16), ym_ref[...].astype(BF16),
                         yf_ref[...].astype(BF16)], axis=1)
    mix = _dot(y, w_ref[...])
    x1 = _ln(DN_ALPHA * x_ref[...] + mod_ref[2:3, :] * mix) * g_ref[...] + b_ref[...]
    x1_ref[...] = x1
    u2 = _ln(x1) * (1.0 + mod_ref[4:5, :]) + mod_ref[3:4, :]
    lg_ref[...] = _dot_nt(wr_ref[...], u2.astype(BF16))
    half = D_MODEL // 2
    words = pltpu.pack_elementwise([u2[:, :half], u2[:, half:]], packed_dtype=BF16)
    for j in range(PACK_SLAB):
        u2p_ref[pl.ds(j, _TM, stride=PACK_SLAB), :] = words[:, j * LANES:(j + 1) * LANES]


def _out_proj(x, yssd, ymla, yfft, mod_all, mod_rows, lp, seq, layer):
    t = x.shape[0]
    return pl.pallas_call(
        _out_proj_kernel,
        out_shape=[jax.ShapeDtypeStruct((t, D_MODEL), F32),
                   jax.ShapeDtypeStruct((t * PACK_SLAB, LANES), U32),
                   jax.ShapeDtypeStruct((N_EXPERTS, t), F32)],
        grid=(t // _TM,),
        in_specs=[_row_spec(D_MODEL), _row_spec(SSD_DIM), _row_spec(MLA_DIM), _row_spec(FNET_DIM),
                  _mod_spec(mod_rows, t, seq, layer), _layer_spec((D_MODEL, D_MODEL), layer),
                  _layer_spec((1, D_MODEL), layer), _layer_spec((1, D_MODEL), layer),
                  _layer_spec((N_EXPERTS, D_MODEL), layer)],
        out_specs=[_row_spec(D_MODEL),
                   pl.BlockSpec((_TM * PACK_SLAB, LANES), lambda i: (i, 0)),
                   pl.BlockSpec((N_EXPERTS, _TM), lambda i: (0, i))],
        compiler_params=_cparams(("arbitrary",), VMEM_LIMIT),
        name="out_proj",
    )(x, yssd, ymla, yfft, mod_all, lp["w_out"], lp["ln1_g"], lp["ln1_b"], lp["wrt"])


_RANK_SPLIT = 64.0


def _cumsum_tokens(x, triu, below):
    within = _dot(x.astype(BF16), triu)
    totals = jnp.broadcast_to(within[:, LANES - 1:LANES], within.shape)
    return within + _dot(below, totals.astype(BF16))


def _route_kernel(nblk, cap, lg_ref, idx_ref, aff_ref, p_ref, sel_ref):
    lg = lg_ref[...]
    e = jnp.exp(lg - jnp.max(lg, axis=0, keepdims=True))
    aff = e / jnp.sum(e, axis=0, keepdims=True)
    aff_ref[...] = aff
    capf = float(cap)

    def count(mask):
        s = jnp.sum(jnp.where(mask, 1.0, 0.0), axis=2, keepdims=True)
        return jnp.sum(s, axis=1, keepdims=True)

    def search(i, lo):
        cand = lo | jnp.left_shift(jnp.int32(1), 30 - i)
        return jnp.where(count(aff >= pltpu.bitcast(cand, F32)) >= capf, cand, lo)

    thr_bits = lax.fori_loop(0, 31, search, jnp.zeros((N_EXPERTS, 1, 1), I32))
    thr = pltpu.bitcast(thr_bits, F32)
    need = capf - count(aff > thr)

    rows = N_EXPERTS * nblk
    ri = lax.broadcasted_iota(I32, (LANES, LANES), 0)
    ci = lax.broadcasted_iota(I32, (LANES, LANES), 1)
    triu = jnp.where(ri <= ci, 1.0, 0.0).astype(BF16)
    rr = lax.broadcasted_iota(I32, (rows, rows), 0)
    rc = lax.broadcasted_iota(I32, (rows, rows), 1)
    below = jnp.where((rc < rr) & (rc >= (rr // nblk) * nblk), 1.0, 0.0).astype(BF16)
    flat = lambda v: v.reshape(rows, LANES)
    gt = flat(jnp.where(aff > thr, 1.0, 0.0))
    eq = flat(jnp.where(aff == thr, 1.0, 0.0))
    need_rows = flat(jnp.broadcast_to(need, aff.shape))
    eq_rank = _cumsum_tokens(eq, triu, below) - eq
    sel = gt + eq * jnp.where(eq_rank < need_rows, 1.0, 0.0)
    rank = _cumsum_tokens(sel, triu, below)
    sel_ref[...] = sel.reshape(N_EXPERTS, nblk, LANES)
    p_ref[...] = rank.reshape(N_EXPERTS, nblk, LANES)

    ones8 = jnp.ones((SUBLANES, LANES), BF16)
    bi = lax.broadcasted_iota(I32, (nblk, nblk), 0)
    bj = lax.broadcasted_iota(I32, (nblk, nblk), 1)
    triu_b = jnp.where(bi <= bj, 1.0, 0.0).astype(BF16)
    blk_ids = lax.broadcasted_iota(I32, (SUBLANES, nblk), 1).astype(F32).astype(BF16)
    r_col = lax.broadcasted_iota(I32, (cap, 1), 0).astype(F32)

    def per_expert(ex, carry):
        pe = p_ref[ex]
        tot_row = _dot_nt(ones8, sel_ref[ex].astype(BF16))[0:1, :]
        upto = _dot(jnp.broadcast_to(tot_row, (SUBLANES, nblk)).astype(BF16), triu_b)[0:1, :]
        onehot = jnp.where((upto - tot_row <= r_col) & (r_col < upto), 1.0, 0.0).astype(BF16)
        p_hi = jnp.floor(pe * (1.0 / _RANK_SPLIT))
        p_lo = pe - p_hi * _RANK_SPLIT
        ranks = _dot(onehot, p_hi.astype(BF16)) * _RANK_SPLIT + _dot(onehot, p_lo.astype(BF16))
        inside = jnp.where(ranks <= r_col, 1.0, 0.0).astype(BF16)
        cnt = _dot_nt(ones8, inside)[0:1, :] + float(LANES) * _dot_nt(blk_ids, onehot)[0:1, :]
        idx_ref[ex] = cnt.astype(I32)
        return carry

    lax.fori_loop(0, N_EXPERTS, per_expert, 0)


def _route(logits, cap):
    t = logits.shape[1]
    nblk = t // LANES
    idx, aff = pl.pallas_call(
        functools.partial(_route_kernel, nblk, cap),
        out_shape=[jax.ShapeDtypeStruct((N_EXPERTS, 1, cap), I32),
                   jax.ShapeDtypeStruct((N_EXPERTS, nblk, LANES), F32)],
        scratch_shapes=[pltpu.VMEM((N_EXPERTS, nblk, LANES), F32),
                        pltpu.VMEM((N_EXPERTS, nblk, LANES), F32)],
        compiler_params=_cparams(None, VMEM_LIMIT),
        name="route",
    )(logits.reshape(N_EXPERTS, nblk, LANES))
    return idx.reshape(N_EXPERTS, cap), aff.reshape(N_EXPERTS, t)


_UP_TF = 512
_UNROLL = 16


def _gather_rows(tile_ref, streams, ex, step, n_steps):
    row0 = 0
    for src_ref, idx_ref, cap in streams:
        cnt = cap // n_steps
        for u in range(cnt):
            r = step * cnt + u
            src = pl.multiple_of(idx_ref[ex * cap + r] * PACK_SLAB, PACK_SLAB)
            dst = pl.multiple_of((row0 + r) * PACK_SLAB, PACK_SLAB)
            tile_ref[pl.ds(dst, PACK_SLAB), :] = src_ref[pl.ds(src, PACK_SLAB), :]
        row0 += cap


def _moe_up_kernel(caps, ua_ref, ub_ref, ia_ref, ib_ref, wg_ref, wu_ref, h_ref, tile_ref, xe_ref):
    ex = pl.program_id(0)
    ff = pl.program_id(1)
    n_steps = EXPERT_FF // _UP_TF
    m = sum(caps)
    streams = ((ua_ref, ia_ref, caps[0]), (ub_ref, ib_ref, caps[1]))

    @pl.when((ex == 0) & (ff == 0))
    def _first_rows():
        def body(i, carry):
            row0 = 0
            for src_ref, idx_ref, cap in streams:
                @pl.when(i < cap // _UNROLL)
                def _(src_ref=src_ref, idx_ref=idx_ref, row0=row0):
                    for u in range(_UNROLL):
                        r = i * _UNROLL + u
                        src = pl.multiple_of(idx_ref[r] * PACK_SLAB, PACK_SLAB)
                        dst = pl.multiple_of((row0 + r) * PACK_SLAB, PACK_SLAB)
                        tile_ref[pl.ds(dst, PACK_SLAB), :] = src_ref[pl.ds(src, PACK_SLAB), :]
                row0 += cap
            return carry

        lax.fori_loop(0, max(caps) // _UNROLL, body, 0)

    @pl.when(ff == 0)
    def _unpack():
        half = D_MODEL // 2
        for j in range(PACK_SLAB):
            w = tile_ref[pl.ds(j, m, stride=PACK_SLAB), :]
            for k in range(2):
                v = pltpu.unpack_elementwise(w, index=k, packed_dtype=BF16, unpacked_dtype=F32)
                lo = k * half + j * LANES
                xe_ref[:, lo:lo + LANES] = v.astype(BF16)

    nxt = jnp.minimum(ex + 1, N_EXPERTS - 1)
    _gather_rows(tile_ref, streams, nxt, ff, n_steps)
    x = xe_ref[...]
    g = _dot(x, wg_ref[...].astype(BF16))
    u = _dot(x, wu_ref[...].astype(BF16))
    h_ref[...] = (_silu(g) * u).astype(BF16)


def _moe_up(u2p_a, u2p_b, idx_a, idx_b, w_gate, w_up, layer):
    caps = (idx_a.shape[1], idx_b.shape[1])
    m = sum(caps)
    smem = pl.BlockSpec(memory_space=pltpu.SMEM)
    w_spec = pl.BlockSpec((None, None, D_MODEL, _UP_TF), lambda e, f: (layer, e, 0, f))
    return pl.pallas_call(
        functools.partial(_moe_up_kernel, caps),
        out_shape=jax.ShapeDtypeStruct((N_EXPERTS, m, EXPERT_FF), BF16),
        grid=(N_EXPERTS, EXPERT_FF // _UP_TF),
        in_specs=[pl.BlockSpec(u2p_a.shape, lambda e, f: (0, 0), pipeline_mode=pl.Buffered(1)),
                  pl.BlockSpec(u2p_b.shape, lambda e, f: (0, 0), pipeline_mode=pl.Buffered(1)),
                  smem, smem, w_spec, w_spec],
        out_specs=pl.BlockSpec((None, m, _UP_TF), lambda e, f: (e, 0, f)),
        scratch_shapes=[pltpu.VMEM((m * PACK_SLAB, LANES), U32), pltpu.VMEM((m, D_MODEL), BF16)],
        compiler_params=_cparams(("arbitrary", "arbitrary"), VMEM_LIMIT),
        name="moe_up",
    )(u2p_a, u2p_b, idx_a.reshape(-1), idx_b.reshape(-1), w_gate, w_up)


_RMW = 16
_DOWN_ACC_LIMIT = 16 * 1024 * 1024


def _scatter_rows(acc_ref, slab_ref, idx_ref, aff_ref, idx0, src0, scale, rows):
    for b0 in range(0, rows, _RMW):
        dsts, vals = [], []
        for u in range(b0, b0 + _RMW):
            tok = idx_ref[idx0 + u]
            dst = pl.multiple_of(tok * ROW_SLAB, ROW_SLAB)
            src = pl.multiple_of(src0 + u * ROW_SLAB, ROW_SLAB)
            vals.append(acc_ref[pl.ds(dst, ROW_SLAB), :]
                        + (aff_ref[tok] * scale) * slab_ref[pl.ds(src, ROW_SLAB), :])
            dsts.append(dst)
        for dst, val in zip(dsts, vals):
            acc_ref[pl.ds(dst, ROW_SLAB), :] = val


def _moe_down_kernel(cap, tn, h_ref, wd_ref, idx_ref, affp_ref, aff_ref, out_ref, acc_ref,
                     slab_a_ref, slab_b_ref):
    ex = pl.program_id(0)
    nn = pl.program_id(1)
    n_steps = D_MODEL // tn
    n_half = tn // LANES
    part = cap // n_steps
    slabs = (slab_a_ref, slab_b_ref)

    @pl.when((ex == 0) & (nn == 0))
    def _zero():
        acc_ref[...] = jnp.zeros(acc_ref.shape, F32)
        slab_b_ref[...] = jnp.zeros(slab_b_ref.shape, F32)

    prev = jnp.maximum(ex - 1, 0)
    live = jnp.where(ex > 0, 1.0, 0.0)
    row0 = nn * part

    def stage(slab_prev_ref, slab_next_ref):
        _scatter_rows(acc_ref, slab_prev_ref, idx_ref, affp_ref, prev * cap + row0,
                      row0 * ROW_SLAB, live, part)
        ye = _dot(h_ref[...], wd_ref[...].astype(BF16))
        for j in range(n_half):
            slab_next_ref[pl.ds(nn * n_half + j, cap, stride=ROW_SLAB), :] = (
                ye[:, j * LANES:(j + 1) * LANES])

    for parity in range(2):
        @pl.when(ex % 2 == parity)
        def _(parity=parity):
            stage(slabs[1 - parity], slabs[parity])

    @pl.when((ex == N_EXPERTS - 1) & (nn == n_steps - 1))
    def _tail():
        def body(i, carry):
            _scatter_rows(acc_ref, slabs[(N_EXPERTS - 1) % 2], idx_ref, aff_ref,
                          ex * cap + i * _RMW, i * (_RMW * ROW_SLAB), 1.0, _RMW)
            return carry

        lax.fori_loop(0, cap // _RMW, body, 0)
        pltpu.sync_copy(acc_ref, out_ref)


def _moe_down(h, w_down, idx, aff, row_block, layer):
    cap = idx.shape[1]
    t = aff.shape[1]
    aff3 = aff[:, None, :]
    tn = D_MODEL if t * D_MODEL * 4 <= _DOWN_ACC_LIMIT else D_MODEL // 2
    gate_spec = lambda shift: pl.BlockSpec(
        (None, None, t), lambda e, n: (jnp.maximum(e - shift, 0), 0, 0), memory_space=pltpu.SMEM)
    return pl.pallas_call(
        functools.partial(_moe_down_kernel, cap, tn),
        out_shape=jax.ShapeDtypeStruct((t * ROW_SLAB, LANES), F32),
        grid=(N_EXPERTS, D_MODEL // tn),
        in_specs=[pl.BlockSpec((None, cap, EXPERT_FF), lambda e, n: (e, row_block, 0)),
                  pl.BlockSpec((None, None, EXPERT_FF, tn), lambda e, n: (layer, e, 0, n)),
                  pl.BlockSpec(memory_space=pltpu.SMEM), gate_spec(1), gate_spec(0)],
        out_specs=pl.BlockSpec(memory_space=pl.ANY),
        scratch_shapes=[pltpu.VMEM((t * ROW_SLAB, LANES), F32),
                        pltpu.VMEM((cap * ROW_SLAB, LANES), F32),
                        pltpu.VMEM((cap * ROW_SLAB, LANES), F32)],
        compiler_params=_cparams(("arbitrary", "arbitrary"), VMEM_LIMIT),
        name="moe_down",
    )(h, w_down, idx.reshape(-1), aff3, aff3)


def _swap_pairs(w):
    s = w.shape
    return jnp.flip(w.reshape(s[:-1] + (s[-1] // 2, 2)), axis=-1).reshape(s)


def _pack_w_in(w_in):
    o_dt, o_cq, o_kr, o_fin = 1280, 1296, 1680, 1712
    kr = w_in[:, :, o_kr:o_fin]
    pad = jnp.zeros(w_in.shape[:2] + (_IN_KR4 - _IN_SMALL - 2 * SSD_HEADS,), w_in.dtype)
    parts = [w_in[:, :, :o_dt], w_in[:, :, o_cq:o_kr], w_in[:, :, o_fin:], w_in[:, :, o_dt:o_cq], pad,
             jnp.tile(kr, (1, 1, MLA_HEADS)), jnp.tile(_swap_pairs(kr), (1, 1, MLA_HEADS))]
    return jnp.concatenate(parts, axis=2).astype(BF16)


def _pack_w_uq(w_uq):
    w = w_uq.reshape(w_uq.shape[:2] + (MLA_HEADS, QK_NOPE + ROPE_DIM))
    flat = lambda v: v.reshape(v.shape[:2] + (-1,))
    rope = w[..., QK_NOPE:]
    return jnp.concatenate([flat(w[..., :QK_NOPE]), flat(rope), flat(_swap_pairs(rope))],
                           axis=2).astype(BF16)


def _pack_w_ukv(w_ukv):
    w = w_ukv.reshape(w_ukv.shape[:2] + (MLA_HEADS, QK_NOPE + V_DIM))
    flat = lambda v: v.reshape(v.shape[:2] + (-1,))
    return jnp.concatenate([flat(w[..., :QK_NOPE]), flat(w[..., QK_NOPE:])], axis=2).astype(BF16)


def _rope_tables(seq):
    rows = seq // GRID_W
    r, col = jnp.meshgrid(jnp.arange(rows, dtype=F32), jnp.arange(GRID_W, dtype=F32), indexing="ij")
    pairs = ROPE_DIM // 4
    inv = ROPE_BASE ** (-jnp.arange(pairs, dtype=F32) / pairs)
    ang = jnp.concatenate([r.reshape(-1, 1) * inv, col.reshape(-1, 1) * inv], axis=-1)
    cos = jnp.repeat(jnp.cos(ang), 2, axis=1)
    sin = jnp.repeat(jnp.sin(ang), 2, axis=1) * jnp.tile(jnp.array([-1.0, 1.0], F32), ROPE_DIM // 2)
    return jnp.tile(cos, (1, MLA_HEADS)), jnp.tile(sin, (1, MLA_HEADS))


def _pad_lanes(v):
    return jnp.pad(v, ((0, 0), (0, LANES - v.shape[1])))[:, None, :]


def kernel(x_prompt, x_sample, cache_ckv, cache_krope, state_ssm, c, c_ctx,
           w_in, conv_w, conv_b, dt_bias, a_log, d_skip, ssd_norm_g,
           q_norm_g, w_uq, kv_norm_g, w_ukv, w_out, w_mod, b_mod,
           ln1_g, ln1_b, ln2_g, ln2_b, w_router, w_gate, w_up, w_down):
    nb_c, seq_c, _ = x_prompt.shape
    nb_l, seq_l, _ = x_sample.shape
    t_c, t_l = nb_c * seq_c, nb_l * seq_l
    caps = (CAPACITY_FACTOR * t_c // N_EXPERTS, CAPACITY_FACTOR * t_l // N_EXPERTS)

    w_in_p = _pack_w_in(w_in)
    wq = _pack_w_uq(w_uq)
    wkv = _pack_w_ukv(w_ukv)
    row = lambda v: v[:, None, :]
    lp = dict(conv_w=conv_w, conv_b=row(conv_b), dt_bias=_pad_lanes(dt_bias.reshape(DEPTH, -1)),
              a_log=_pad_lanes(a_log.reshape(DEPTH, -1)),
              d_skip=row(jnp.repeat(d_skip, SSD_HEADDIM, axis=1)), ssd_norm_g=row(ssd_norm_g),
              q_norm_g=row(q_norm_g), wq=wq, kv_norm_g=row(kv_norm_g), wkv=wkv,
              w_out=w_out.astype(BF16), ln1_g=row(ln1_g), ln1_b=row(ln1_b),
              wrt=jnp.swapaxes(w_router, 1, 2).astype(BF16))
    ln2 = (row(ln2_g), row(ln2_b))
    cache_kr4 = jnp.tile(cache_krope, (1, 1, 1, MLA_HEADS))
    cos4, sin4 = _rope_tables(seq_l)

    cond = jnp.zeros((SUBLANES, D_MODEL), F32).at[0].set(c_ctx).at[1:1 + nb_l].set(c)
    mod_all = _modulation(cond, w_mod, b_mod).reshape(DEPTH, SUBLANES, 6, D_MODEL)

    streams = [
        dict(x=x_prompt.reshape(t_c, D_MODEL), nb=nb_c, seq=seq_c, cap=caps[0], mod_rows=(0, 1)),
        dict(x=x_sample.reshape(t_l, D_MODEL), nb=nb_l, seq=seq_l, cap=caps[1], mod_rows=(1, nb_l)),
    ]
    prev = [None, None]
    ssm_hist = jnp.zeros((nb_c, DEPTH) + state_ssm.shape[2:], F32)
    kv_hist = (jnp.zeros((nb_c, DEPTH, seq_c, KV_LORA), F32),
               jnp.zeros((nb_c, DEPTH, seq_c, ROPE_DIM), F32))
    for l in range(DEPTH):
        routed = []
        for si, st in enumerate(streams):
            nb, seq, mod_rows = st["nb"], st["seq"], st["mod_rows"]
            x, z, xbc, cq, ckv, fin, small, kr = _in_proj(
                st["x"], prev[si], mod_all, mod_rows, w_in_p, seq, l)
            if si == 0:
                yssd, ssm_hist = _ssd(z, xbc, small, lp, nb, seq, None, ssm_hist, l)
                ymla, *kv_hist = _mla(cq, ckv, kr, lp, nb, seq, None, kv_hist, l)
            else:
                yssd, _ = _ssd(z, xbc, small, lp, nb, seq, state_ssm, None, l)
                (ymla,) = _mla(cq, ckv, kr, lp, nb, seq, (cache_ckv, cache_kr4, cos4, sin4), None, l)
            yfft = _fnet(fin, nb, seq)
            x1, u2p, logits = _out_proj(x, yssd, ymla, yfft, mod_all, mod_rows, lp, seq, l)
            idx, aff = _route(logits, st["cap"])
            st["x"] = x1
            routed.append((u2p, idx, aff))
        h = _moe_up(routed[0][0], routed[1][0], routed[0][1], routed[1][1], w_gate, w_up, l)
        for si in range(2):
            moe = _moe_down(h, w_down, routed[si][1], routed[si][2], 0 if si == 0 else 2, l)
            prev[si] = (moe,) + ln2
    last = DEPTH - 1
    y_p, y_s = [_final_norm(st["x"], *prev[si], mod_all, st["mod_rows"], st["seq"], last)
                for si, st in enumerate(streams)]
    y_p, y_s = y_p.reshape(x_prompt.shape), y_s.reshape(x_sample.shape)
    return (y_p, y_s, kv_hist[0], kv_hist[1], ssm_hist)
```

```python
import functools

import jax
import jax.numpy as jnp
import numpy as np
from jax import lax
from jax.experimental import pallas as pl
from jax.experimental.pallas import tpu as pltpu

F32 = jnp.float32
BF16 = jnp.bfloat16
I32 = jnp.int32
U32 = jnp.uint32

D_MODEL = 1024
DEPTH = 4
GRID_W = 64
SSD_HEADS = 8
SSD_HEADDIM = 64
SSD_DIM = SSD_HEADS * SSD_HEADDIM
SSD_STATE = 64
SSD_CONV = 5
SSD_CHUNK = 128
SSD_CONV_DIM = SSD_DIM + 4 * SSD_STATE
MLA_HEADS = 4
Q_LORA = 256
KV_LORA = 128
QK_NOPE = 64
ROPE_DIM = 32
V_DIM = 64
MLA_DIM = MLA_HEADS * V_DIM
MLA_SCALE = (QK_NOPE + ROPE_DIM) ** -0.5
ROPE_BASE = 10000.0
Q_TILE = 256
FNET_GW = 64
FNET_DIM = 256
N_EXPERTS = 16
EXPERT_FF = 1024
CAPACITY_FACTOR = 2
DN_ALPHA = (2 * DEPTH) ** 0.25
EPS = 1e-5
_LOG2E = 1.4426950408889634

LANES = 128
SUBLANES = 8
ROW_SLAB = D_MODEL // LANES
PACK_SLAB = ROW_SLAB // 2
VMEM_LIMIT = 60 * 1024 * 1024

_IN_Z = 0
_IN_XBC = 512
_IN_CQ = 1280
_IN_CKV = 1536
_IN_FIN = 1664
_IN_SMALL = 1920
_IN_KR4 = 2048
IN_PAD = 2304
_IN_WIDTHS = (512, 768, 256, 128, 256, 128, 256)


def _cparams(sem, vmem=None):
    return pltpu.CompilerParams(dimension_semantics=sem, vmem_limit_bytes=vmem)


def _ln(x):
    mu = jnp.mean(x, axis=-1, keepdims=True)
    xc = x - mu
    var = jnp.mean(xc * xc, axis=-1, keepdims=True)
    return xc * lax.rsqrt(var + EPS)


def _rms(x):
    return x * lax.rsqrt(jnp.mean(x * x, axis=-1, keepdims=True) + EPS)


def _silu(x):
    return x * (1.0 / (1.0 + jnp.exp(-x)))


def _dot(a, b):
    return jnp.dot(a, b, preferred_element_type=F32)


def _dot_nt(a, b):
    return lax.dot_general(a, b, (((1,), (1,)), ((), ())), preferred_element_type=F32)


def _split3(x):
    hi = x.astype(BF16)
    r1 = x - hi.astype(F32)
    mid = r1.astype(BF16)
    lo = (r1 - mid.astype(F32)).astype(BF16)
    return hi, mid, lo


def _const_spec(shape):
    return pl.BlockSpec(shape, lambda *_: (0,) * len(shape))


def _layer_spec(shape, layer):
    return pl.BlockSpec((None,) + shape, lambda *_: (layer,) + (0,) * len(shape))


_MOD_TN = 1536


def _mod_kernel(c_ref, w_ref, b_ref, o_ref):
    a = _silu(c_ref[...]).astype(BF16)
    o_ref[...] = _dot(a, w_ref[...].astype(BF16)) + b_ref[...]


def _modulation(cond, w_mod, b_mod):
    n = 6 * D_MODEL
    return pl.pallas_call(
        _mod_kernel,
        out_shape=jax.ShapeDtypeStruct((DEPTH, SUBLANES, n), F32),
        grid=(DEPTH, n // _MOD_TN),
        in_specs=[
            pl.BlockSpec((SUBLANES, D_MODEL), lambda l, j: (0, 0)),
            pl.BlockSpec((None, D_MODEL, _MOD_TN), lambda l, j: (l, 0, j)),
            pl.BlockSpec((None, 1, _MOD_TN), lambda l, j: (l, 0, j)),
        ],
        out_specs=pl.BlockSpec((None, SUBLANES, _MOD_TN), lambda l, j: (l, 0, j)),
        compiler_params=_cparams(("arbitrary", "arbitrary")),
        name="modulation",
    )(cond, w_mod, b_mod.reshape(DEPTH, 1, n))


_TM = 512


def _read_rows(slab_ref, tm):
    return jnp.concatenate(
        [slab_ref[pl.ds(j, tm, stride=ROW_SLAB), :] for j in range(ROW_SLAB)], axis=-1)


def _post_norm2(x_ref, moe_ref, pmod_ref, g_ref, b_ref):
    moe = _read_rows(moe_ref, _TM)
    return _ln(DN_ALPHA * x_ref[...] + pmod_ref[5:6, :] * moe) * g_ref[...] + b_ref[...]


def _in_proj_kernel(first, *refs):
    if first:
        x_ref, mod_ref, w_ref = refs[:3]
        outs = refs[3:]
        x = x_ref[...]
    else:
        x_ref, moe_ref, pmod_ref, g_ref, b_ref, mod_ref, w_ref, xo_ref = refs[:8]
        outs = refs[8:]
        x = _post_norm2(x_ref, moe_ref, pmod_ref, g_ref, b_ref)
        xo_ref[...] = x
    u = _ln(x) * (1.0 + mod_ref[1:2, :]) + mod_ref[0:1, :]
    p = _dot(u.astype(BF16), w_ref[...])
    off = 0
    for o_ref, w in zip(outs, _IN_WIDTHS):
        o_ref[...] = p[:, off:off + w]
        off += w


def _mod_spec(mod_rows, t, seq, layer):
    row0, count = mod_rows
    per = seq // _TM if count > 1 else t // _TM
    return pl.BlockSpec((None, None, 6, D_MODEL), lambda i: (layer, row0 + i // per, 0, 0))


def _row_spec(w):
    return pl.BlockSpec((_TM, w), lambda i: (i, 0))


_SLAB_SPEC = pl.BlockSpec((_TM * ROW_SLAB, LANES), lambda i: (i, 0))


def _in_proj(x, prev, mod_all, mod_rows, w_in_p, seq, layer):
    t = x.shape[0]
    mod_spec = _mod_spec(mod_rows, t, seq, layer)
    outs = [jax.ShapeDtypeStruct((t, w), F32) for w in _IN_WIDTHS]
    out_specs = [_row_spec(w) for w in _IN_WIDTHS]
    w_spec = _layer_spec((D_MODEL, IN_PAD), layer)
    if prev is None:
        ins = [x, mod_all, w_in_p]
        in_specs = [_row_spec(D_MODEL), mod_spec, w_spec]
    else:
        moe_slab, g, b = prev
        ins = [x, moe_slab, mod_all, g, b, mod_all, w_in_p]
        in_specs = [_row_spec(D_MODEL), _SLAB_SPEC, _mod_spec(mod_rows, t, seq, layer - 1),
                    _layer_spec((1, D_MODEL), layer - 1), _layer_spec((1, D_MODEL), layer - 1),
                    mod_spec, w_spec]
        outs = [jax.ShapeDtypeStruct((t, D_MODEL), F32)] + outs
        out_specs = [_row_spec(D_MODEL)] + out_specs
    res = pl.pallas_call(
        functools.partial(_in_proj_kernel, prev is None),
        out_shape=outs, grid=(t // _TM,), in_specs=in_specs, out_specs=out_specs,
        compiler_params=_cparams(("arbitrary",), VMEM_LIMIT),
        name="in_proj",
    )(*ins)
    if prev is None:
        return (x,) + tuple(res)
    return tuple(res)


def _final_norm_kernel(x_ref, moe_ref, pmod_ref, g_ref, b_ref, o_ref):
    o_ref[...] = _post_norm2(x_ref, moe_ref, pmod_ref, g_ref, b_ref)


def _final_norm(x, moe_slab, g, b, mod_all, mod_rows, seq, layer):
    t = x.shape[0]
    return pl.pallas_call(
        _final_norm_kernel,
        out_shape=jax.ShapeDtypeStruct((t, D_MODEL), F32), grid=(t // _TM,),
        in_specs=[_row_spec(D_MODEL), _SLAB_SPEC, _mod_spec(mod_rows, t, seq, layer),
                  _layer_spec((1, D_MODEL), layer), _layer_spec((1, D_MODEL), layer)],
        out_specs=_row_spec(D_MODEL),
        compiler_params=_cparams(("arbitrary",)),
        name="final_norm",
    )(x, moe_slab, mod_all, g, b)


def _ssd_kernel(has_h0, write_h, seq, *refs):
    refs = list(refs)
    z_ref, xbc_ref, sm_ref, cw_ref, cb_ref, dtb_ref, alog_ref, dsk_ref, ng_ref = refs[:9]
    pos = 9
    if has_h0:
        h0_ref = refs[pos]
    pos += 1
    y_ref = refs[pos]
    pos += 1
    if write_h:
        hf_ref = refs[pos]
        pos += 1
    pad_ref, xs_ref, cm_ref, gm_ref, bt_ref, acol_ref, rows_ref, st_ref = refs[pos:]
    ck = SSD_CHUNK
    nc = seq // ck
    halo = SUBLANES
    nh = SSD_HEADS
    nh2 = 2 * nh

    pad_ref[0:halo, :] = jnp.zeros((halo, SSD_CONV_DIM), F32)
    pad_ref[halo + seq:2 * halo + seq, :] = jnp.zeros((halo, SSD_CONV_DIM), F32)
    pad_ref[halo:halo + seq, :] = xbc_ref[...]
    ri = lax.broadcasted_iota(I32, (ck, ck), 0)
    ci = lax.broadcasted_iota(I32, (ck, ck), 1)
    triu = jnp.where(ri <= ci, 1.0, 0.0).astype(BF16)
    lo_half = ci < SSD_STATE
    lo64 = lax.broadcasted_iota(I32, (SSD_STATE, LANES), 1) < SSD_STATE
    zeros_half = jnp.zeros((SSD_STATE, LANES), BF16)
    fwd_rows = lax.broadcasted_iota(I32, (nh2, ck), 0) < nh

    def prepare_chunk(c, carry):
        cs = pl.multiple_of(c * ck, ck)
        rows = pl.ds(cs, ck)
        cw_blk = 2 * LANES
        for cb in range(SSD_CONV_DIM // cw_blk):
            cols = slice(cb * cw_blk, (cb + 1) * cw_blk)
            win = pad_ref[pl.ds(cs, ck + 2 * halo), cols]
            acc = jnp.zeros((ck, cw_blk), F32) + cb_ref[:, cols]
            for k in range(SSD_CONV):
                s0 = halo + k - (SSD_CONV - 1) // 2
                acc = acc + win[s0:s0 + ck, :] * cw_ref[k:k + 1, cols]
            act = _silu(acc)
            if cb * cw_blk < SSD_DIM:
                xs_ref[rows, cols] = act
                y_ref[rows, cols] = act * dsk_ref[:, cols]
        bm = act[:, :LANES]
        cm = act[:, LANES:]
        cm_ref[rows, :] = cm
        bt_ref[rows, :] = bm.T
        cg2 = jnp.concatenate([jnp.where(lo_half, cm, 0.0), jnp.where(lo_half, 0.0, cm)], axis=0)
        gm2 = _dot_nt(cg2.astype(BF16), bm.astype(BF16))
        gm_ref[rows, 0:LANES] = gm2[:ck]
        gm_ref[rows, LANES:] = gm2[ck:]

        xr = sm_ref[rows, :] + dtb_ref[...]
        dt = jnp.maximum(xr, 0.0) + jnp.log(1.0 + jnp.exp(-jnp.abs(xr)))
        ad = jnp.where(ci < nh2, dt * -jnp.exp(alog_ref[...]),
                       jnp.where(ci < 2 * nh2, pltpu.roll(dt, nh2, axis=1), 0.0))
        adt = ad.T
        at16 = adt[0:nh2, :]
        dt16 = adt[nh2:2 * nh2, :]
        csum = _dot(jnp.concatenate(_split3(at16), axis=0), triu)
        arow = csum[0:nh2] + csum[nh2:2 * nh2] + csum[2 * nh2:3 * nh2]
        acol = jnp.concatenate([arow, jnp.zeros((ck - nh2, ck), F32)], axis=0).T
        alast = jnp.broadcast_to(arow[:, ck - 1:ck], (nh2, ck)) * _LOG2E
        arow_s = jnp.where(fwd_rows, arow, arow - at16) * _LOG2E
        ldt = jnp.log2(dt16)
        rowarg = jnp.where(fwd_rows, arow_s - ldt, arow_s + ldt)
        wrow = jnp.exp2(jnp.where(fwd_rows, alast - arow_s, arow_s))
        acol_ref[rows, :] = jnp.where(ci < nh, acol, acol - ad) * _LOG2E
        rows_ref[pl.ds(pl.multiple_of(c * (4 * nh2), 4 * nh2), 4 * nh2), :] = jnp.concatenate(
            [rowarg, wrow * dt16, jnp.exp2(alast), alast], axis=0)
        return carry

    lax.fori_loop(0, nc, prepare_chunk, 0)

    if has_h0:
        for d in range(2):
            h0 = h0_ref[d].reshape(SSD_DIM, SSD_STATE)
            h0 = jnp.concatenate([h0, jnp.zeros((SSD_DIM, LANES - SSD_STATE), F32)], axis=1)
            st_ref[d] = h0.T[:SSD_STATE, :]
    else:
        st_ref[...] = jnp.zeros(st_ref.shape, F32)

    def chunk_dir(d, c):
        cs = pl.multiple_of(c * ck, ck)
        acol = acol_ref[pl.ds(cs, ck), :]
        rws = rows_ref[pl.ds(pl.multiple_of(c * (4 * nh2), 4 * nh2), 4 * nh2), :]
        cm = cm_ref[pl.ds(cs, ck), :]
        cg = [jnp.where(lo_half, cm, 0.0), jnp.where(lo_half, 0.0, cm)]
        for q in range(nh // 2):
            g = q // 2
            gmat = gm_ref[pl.ds(cs, ck), g * LANES:(g + 1) * LANES]
            btg = bt_ref[pl.ds(cs + g * SSD_STATE, SSD_STATE), :]
            lhs_rows = []
            decl = []
            for hh in range(2):
                hi = d * nh + 2 * q + hh
                acol_b = jnp.broadcast_to(acol[:, hi:hi + 1], (ck, ck))
                rowarg = rws[hi:hi + 1, :]
                if d == 0:
                    lmat = jnp.where(ri >= ci, jnp.exp2(acol_b - rowarg), 0.0)
                    ecol = jnp.exp2(acol_b)
                else:
                    lmat = jnp.where(ci >= ri, jnp.exp2(rowarg - acol_b), 0.0)
                    ecol = jnp.exp2(rws[3 * nh2 + hi:3 * nh2 + hi + 1, :] - acol_b)
                mm = (gmat * lmat).astype(BF16)
                cs_h = (cg[g] * ecol).astype(BF16)
                bw = (btg * rws[nh2 + hi:nh2 + hi + 1, :]).astype(BF16)
                lhs_rows.append(jnp.concatenate([mm, cs_h], axis=1))
                lhs_rows.append(jnp.concatenate([bw, zeros_half], axis=1))
                decl.append(rws[2 * nh2 + hi:2 * nh2 + hi + 1, :])
            lhs = jnp.concatenate(lhs_rows, axis=0)
            cols = slice(q * LANES, (q + 1) * LANES)
            st_pair = st_ref[d, :, cols]
            stb = st_pair.astype(BF16)
            w_st = jnp.concatenate([stb, zeros_half] if g == 0 else [zeros_half, stb], axis=0)
            xs_pair = xs_ref[pl.ds(cs, ck), cols].astype(BF16)
            out = _dot(lhs, jnp.concatenate([xs_pair, w_st], axis=0))
            r2 = ck + SSD_STATE
            y_pair = jnp.where(lo_half, out[0:ck], out[r2:r2 + ck])
            st_new = jnp.where(lo64, out[ck:r2], out[r2 + ck:2 * r2])
            dec = jnp.where(lo64, decl[0], decl[1])
            st_ref[d, :, cols] = st_pair * dec + st_new
            y_ref[pl.ds(cs, ck), cols] += y_pair

    def body(c, carry):
        chunk_dir(0, c)
        chunk_dir(1, nc - 1 - c)
        return carry

    lax.fori_loop(0, nc, body, 0)

    def gate_chunk(c, carry):
        rows = pl.ds(pl.multiple_of(c * ck, ck), ck)
        y = y_ref[rows, :] * _silu(z_ref[rows, :])
        y_ref[rows, :] = _rms(y) * ng_ref[...]
        return carry

    lax.fori_loop(0, nc, gate_chunk, 0)

    if write_h:
        for d in range(2):
            st = jnp.concatenate([st_ref[d], jnp.zeros((LANES - SSD_STATE, SSD_DIM), F32)], axis=0)
            hf_ref[d] = st.T[:, :SSD_STATE].reshape(SSD_HEADS, SSD_HEADDIM, SSD_STATE)


def _ssd(z, xbc, small, lp, nb, seq, h0, hist, layer):
    has_h0 = h0 is not None
    write_h = not has_h0
    seq_spec = lambda w: pl.BlockSpec((seq, w), lambda b: (b, 0))
    ins = [z, xbc, small, lp["conv_w"], lp["conv_b"], lp["dt_bias"], lp["a_log"], lp["d_skip"],
           lp["ssd_norm_g"]]
    in_specs = [seq_spec(SSD_DIM), seq_spec(SSD_CONV_DIM), seq_spec(LANES),
                _layer_spec((SSD_CONV, SSD_CONV_DIM), layer), _layer_spec((1, SSD_CONV_DIM), layer),
                _layer_spec((1, LANES), layer), _layer_spec((1, LANES), layer),
                _layer_spec((1, SSD_DIM), layer), _layer_spec((1, SSD_DIM), layer)]
    hshape = (2, SSD_HEADS, SSD_HEADDIM, SSD_STATE)
    layer_block = pl.BlockSpec((None, None) + hshape, lambda b: (b, layer, 0, 0, 0, 0))
    outs = [jax.ShapeDtypeStruct((nb * seq, SSD_DIM), F32)]
    out_specs = [seq_spec(SSD_DIM)]
    aliases = {}
    if has_h0:
        ins.append(h0)
        in_specs.append(layer_block)
    else:
        ins.append(hist)
        in_specs.append(pl.BlockSpec(memory_space=pl.ANY))
        aliases = {len(ins) - 1: 1}
        outs.append(jax.ShapeDtypeStruct(hist.shape, F32))
        out_specs.append(layer_block)
    res = pl.pallas_call(
        functools.partial(_ssd_kernel, has_h0, write_h, seq),
        out_shape=outs, grid=(nb,), in_specs=in_specs, out_specs=out_specs,
        input_output_aliases=aliases,
        scratch_shapes=[pltpu.VMEM((seq + 2 * SUBLANES, SSD_CONV_DIM), F32),
                        pltpu.VMEM((seq, SSD_DIM), F32),
                        pltpu.VMEM((seq, LANES), F32),
                        pltpu.VMEM((seq, 2 * LANES), F32),
                        pltpu.VMEM((seq, LANES), F32),
                        pltpu.VMEM((seq, LANES), F32),
                        pltpu.VMEM((seq // SSD_CHUNK * 8 * SSD_HEADS, LANES), F32),
                        pltpu.VMEM((2, SSD_STATE, SSD_DIM), F32)],
        compiler_params=_cparams(("arbitrary",), VMEM_LIMIT),
        name="ssd",
    )(*ins)
    return res if write_h else (res[0], None)


_MLA_GROUP_ROWS = 1024


def _mla_kernel(has_ctx, seq, group, *refs):
    refs = list(refs)
    cq_ref, ckv_ref, kr_ref, qg_ref, wq_ref, kvg_ref, wkv_ref = refs[:7]
    pos = 7
    if has_ctx:
        cckv_ref, ckr_ref, cos_ref, sin_ref = refs[pos:pos + 4]
        pos += 4
    else:
        pos += 2
    y_ref = refs[pos]
    pos += 1
    if not has_ctx:
        ckvo_ref, kro_ref = refs[pos:pos + 2]

    qn = (_rms(cq_ref[...]) * qg_ref[...]).astype(BF16)
    qall = _dot(qn, wq_ref[...])
    q_nope = qall[:, :MLA_DIM]
    q_rope = qall[:, MLA_DIM:MLA_DIM + LANES]
    ckv_n = _rms(ckv_ref[...]) * kvg_ref[...]
    kr4 = kr_ref[:, :LANES]
    if has_ctx:
        q_rope = q_rope * cos_ref[...] + qall[:, MLA_DIM + LANES:] * sin_ref[...]
        kr4 = kr4 * cos_ref[...] + kr_ref[:, LANES:] * sin_ref[...]
        ckv_all = jnp.concatenate([cckv_ref[...], ckv_n], axis=0)
        kr_all = jnp.concatenate([ckr_ref[...], kr4], axis=0)
    else:
        ckvo_ref[...] = ckv_n.reshape(group, seq, KV_LORA)
        kro_ref[...] = kr4[:, :ROPE_DIM].reshape(group, seq, ROPE_DIM)
        ckv_all = ckv_n
        kr_all = kr4
    kv = _dot(ckv_all.astype(BF16), wkv_ref[...])
    krb = kr_all.astype(BF16)
    lane = lax.broadcasted_iota(I32, (Q_TILE, LANES), 1)
    lo_half = lane < QK_NOPE
    n_keys = kv.shape[0] // group
    for qt in range(group * seq // Q_TILE):
        rows = slice(qt * Q_TILE, (qt + 1) * Q_TILE)
        s0 = (qt * Q_TILE // seq) * n_keys
        keys = slice(s0, s0 + n_keys)
        qr = q_rope[rows] * MLA_SCALE
        for pair in range(MLA_HEADS // 2):
            cols = slice(pair * LANES, (pair + 1) * LANES)
            kcat = jnp.concatenate([kv[keys, cols].astype(BF16), krb[keys]], axis=1)
            vb = kv[keys, MLA_DIM + pair * LANES:MLA_DIM + (pair + 1) * LANES].astype(BF16)
            qn_pair = q_nope[rows, cols] * MLA_SCALE
            outs = []
            for hh in range(2):
                h = 2 * pair + hh
                qa = jnp.where(lo_half if hh == 0 else jnp.logical_not(lo_half), qn_pair, 0.0)
                qb = jnp.where((lane >= h * ROPE_DIM) & (lane < (h + 1) * ROPE_DIM), qr, 0.0)
                s = _dot_nt(jnp.concatenate([qa, qb], axis=1).astype(BF16), kcat)
                p = jnp.exp(s - jnp.max(s, axis=-1, keepdims=True))
                o = _dot(p.astype(BF16), vb)
                outs.append(o / jnp.sum(p, axis=-1, keepdims=True))
            y_ref[rows, cols] = jnp.where(lo_half, outs[0], outs[1])


def _mla(cq, ckv, kr, lp, nb, seq, ctx, hist, layer):
    has_ctx = ctx is not None
    group = 1 if has_ctx else _MLA_GROUP_ROWS // seq
    seq_spec = lambda w: pl.BlockSpec((group * seq, w), lambda b: (b, 0))
    ins = [cq, ckv, kr, lp["q_norm_g"], lp["wq"], lp["kv_norm_g"], lp["wkv"]]
    in_specs = [seq_spec(Q_LORA), seq_spec(KV_LORA), seq_spec(2 * LANES),
                _layer_spec((1, Q_LORA), layer), _layer_spec((Q_LORA, 2 * MLA_DIM), layer),
                _layer_spec((1, KV_LORA), layer), _layer_spec((KV_LORA, 2 * MLA_DIM), layer)]
    outs = [jax.ShapeDtypeStruct((nb * seq, MLA_DIM), F32)]
    out_specs = [seq_spec(MLA_DIM)]
    if has_ctx:
        cache_ckv, cache_kr4, cos4, sin4 = ctx
        past = cache_ckv.shape[2]
        ins += [cache_ckv, cache_kr4, cos4, sin4]
        in_specs += [pl.BlockSpec((None, None, past, KV_LORA), lambda b: (b, layer, 0, 0)),
                     pl.BlockSpec((None, None, past, LANES), lambda b: (b, layer, 0, 0)),
                     _const_spec((seq, LANES)), _const_spec((seq, LANES))]
    else:
        aliases = {len(ins): 1, len(ins) + 1: 2}
        ins += list(hist)
        in_specs += [pl.BlockSpec(memory_space=pl.ANY)] * 2
        outs += [jax.ShapeDtypeStruct(h.shape, F32) for h in hist]
        out_specs += [pl.BlockSpec((group, None, seq, KV_LORA), lambda b: (b, layer, 0, 0)),
                      pl.BlockSpec((group, None, seq, ROPE_DIM), lambda b: (b, layer, 0, 0))]
    return pl.pallas_call(
        functools.partial(_mla_kernel, has_ctx, seq, group),
        out_shape=outs, grid=(nb // group,), in_specs=in_specs, out_specs=out_specs,
        input_output_aliases={} if has_ctx else aliases,
        compiler_params=_cparams(("arbitrary",), VMEM_LIMIT),
        name="mla",
    )(*ins)


_FNET_ROWS = 1024


def _fnet_kernel(seq, g_ref, cbd_ref, sbd_ref, cs_ref, o_ref):
    gb = g_ref[...].astype(BF16)
    gc = _dot(gb, cbd_ref[...].astype(BF16)).astype(BF16)
    gs = _dot(gb, sbd_ref[...].astype(BF16)).astype(BF16)
    cs = cs_ref[...].astype(BF16)
    for i in range(_FNET_ROWS // seq):
        rows = slice(i * seq, (i + 1) * seq)
        o_ref[rows, :] = _dot(cs, jnp.concatenate([gc[rows], gs[rows]], axis=0))


def _dft_constants(seq):
    k = np.arange(FNET_GW)
    ang = 2.0 * np.pi * np.outer(k, k) / FNET_GW
    eye = np.eye(FNET_DIM // FNET_GW)
    cbd = np.kron(eye, np.cos(ang)) / np.sqrt(FNET_GW)
    sbd = np.kron(eye, np.sin(ang)) / np.sqrt(FNET_GW)
    s = np.arange(seq)
    angs = 2.0 * np.pi * np.outer(s, s) / seq
    cs = np.concatenate([np.cos(angs), -np.sin(angs)], axis=1) / np.sqrt(seq)
    return (jnp.asarray(cbd, F32), jnp.asarray(sbd, F32), jnp.asarray(cs, F32))


def _fnet(fin, nb, seq):
    cbd, sbd, cs = _dft_constants(seq)
    return pl.pallas_call(
        functools.partial(_fnet_kernel, seq),
        out_shape=jax.ShapeDtypeStruct((nb * seq, FNET_DIM), F32), grid=(nb * seq // _FNET_ROWS,),
        in_specs=[pl.BlockSpec((_FNET_ROWS, FNET_DIM), lambda b: (b, 0)),
                  _const_spec((FNET_DIM, FNET_DIM)), _const_spec((FNET_DIM, FNET_DIM)),
                  _const_spec((seq, 2 * seq))],
        out_specs=pl.BlockSpec((_FNET_ROWS, FNET_DIM), lambda b: (b, 0)),
        compiler_params=_cparams(("arbitrary",), VMEM_LIMIT),
        name="fnet",
    )(fin, cbd, sbd, cs)


def _out_proj_kernel(x_ref, ys_ref, ym_ref, yf_ref, mod_ref, w_ref, g_ref, b_ref, wr_ref,
                     x1_ref, u2p_ref, lg_ref):
    y = jnp.concatenate([ys_ref[...].astype(BF16), ym_ref[...].astype(BF16),
                         yf_ref[...].astype(BF16)], axis=1)
    mix = _dot(y, w_ref[...])
    x1 = _ln(DN_ALPHA * x_ref[...] + mod_ref[2:3, :] * mix) * g_ref[...] + b_ref[...]
    x1_ref[...] = x1
    u2 = _ln(x1) * (1.0 + mod_ref[4:5, :]) + mod_ref[3:4, :]
    lg_ref[...] = _dot_nt(wr_ref[...], u2.astype(BF16))
    half = D_MODEL // 2
    words = pltpu.pack_elementwise([u2[:, :half], u2[:, half:]], packed_dtype=BF16)
    for j in range(PACK_SLAB):
        u2p_ref[pl.ds(j, _TM, stride=PACK_SLAB), :] = words[:, j * LANES:(j + 1) * LANES]


def _out_proj(x, yssd, ymla, yfft, mod_all, mod_rows, lp, seq, layer):
    t = x.shape[0]
    return pl.pallas_call(
        _out_proj_kernel,
        out_shape=[jax.ShapeDtypeStruct((t, D_MODEL), F32),
                   jax.ShapeDtypeStruct((t * PACK_SLAB, LANES), U32),
                   jax.ShapeDtypeStruct((N_EXPERTS, t), F32)],
        grid=(t // _TM,),
        in_specs=[_row_spec(D_MODEL), _row_spec(SSD_DIM), _row_spec(MLA_DIM), _row_spec(FNET_DIM),
                  _mod_spec(mod_rows, t, seq, layer), _layer_spec((D_MODEL, D_MODEL), layer),
                  _layer_spec((1, D_MODEL), layer), _layer_spec((1, D_MODEL), layer),
                  _layer_spec((N_EXPERTS, D_MODEL), layer)],
        out_specs=[_row_spec(D_MODEL),
                   pl.BlockSpec((_TM * PACK_SLAB, LANES), lambda i: (i, 0)),
                   pl.BlockSpec((N_EXPERTS, _TM), lambda i: (0, i))],
        compiler_params=_cparams(("arbitrary",), VMEM_LIMIT),
        name="out_proj",
    )(x, yssd, ymla, yfft, mod_all, lp["w_out"], lp["ln1_g"], lp["ln1_b"], lp["wrt"])


_RANK_SPLIT = 64.0


def _cumsum_tokens(x, triu, below):
    within = _dot(x.astype(BF16), triu)
    totals = jnp.broadcast_to(within[:, LANES - 1:LANES], within.shape)
    return within + _dot(below, totals.astype(BF16))


def _route_kernel(nblk, cap, lg_ref, idx_ref, aff_ref, p_ref, sel_ref):
    lg = lg_ref[...]
    e = jnp.exp(lg - jnp.max(lg, axis=0, keepdims=True))
    aff = e / jnp.sum(e, axis=0, keepdims=True)
    aff_ref[...] = aff
    capf = float(cap)

    def count(mask):
        s = jnp.sum(jnp.where(mask, 1.0, 0.0), axis=1, keepdims=True)
        return jnp.sum(s, axis=2, keepdims=True)

    def search(i, lo):
        cand = lo | jnp.left_shift(jnp.int32(1), 30 - i)
        return jnp.where(count(aff >= pltpu.bitcast(cand, F32)) >= capf, cand, lo)

    thr_bits = lax.fori_loop(0, 31, search, jnp.zeros((N_EXPERTS, 1, 1), I32))
    thr = pltpu.bitcast(thr_bits, F32)
    need = capf - count(aff > thr)

    rows = N_EXPERTS * nblk
    ri = lax.broadcasted_iota(I32, (LANES, LANES), 0)
    ci = lax.broadcasted_iota(I32, (LANES, LANES), 1)
    triu = jnp.where(ri <= ci, 1.0, 0.0).astype(BF16)
    rr = lax.broadcasted_iota(I32, (rows, rows), 0)
    rc = lax.broadcasted_iota(I32, (rows, rows), 1)
    below = jnp.where((rc < rr) & (rc >= (rr // nblk) * nblk), 1.0, 0.0).astype(BF16)
    flat = lambda v: v.reshape(rows, LANES)
    gt = flat(jnp.where(aff > thr, 1.0, 0.0))
    eq = flat(jnp.where(aff == thr, 1.0, 0.0))
    need_rows = flat(jnp.broadcast_to(need, aff.shape))
    eq_rank = _cumsum_tokens(eq, triu, below) - eq
    sel = gt + eq * jnp.where(eq_rank < need_rows, 1.0, 0.0)
    rank = _cumsum_tokens(sel, triu, below)
    sel_ref[...] = sel.reshape(N_EXPERTS, nblk, LANES)
    p_ref[...] = rank.reshape(N_EXPERTS, nblk, LANES)

    ones8 = jnp.ones((SUBLANES, LANES), BF16)
    bi = lax.broadcasted_iota(I32, (nblk, nblk), 0)
    bj = lax.broadcasted_iota(I32, (nblk, nblk), 1)
    triu_b = jnp.where(bi <= bj, 1.0, 0.0).astype(BF16)
    blk_ids = lax.broadcasted_iota(I32, (SUBLANES, nblk), 1).astype(F32).astype(BF16)
    r_col = lax.broadcasted_iota(I32, (cap, 1), 0).astype(F32)

    def per_expert(ex, carry):
        pe = p_ref[ex]
        tot_row = _dot_nt(ones8, sel_ref[ex].astype(BF16))[0:1, :]
        upto = _dot(jnp.broadcast_to(tot_row, (SUBLANES, nblk)).astype(BF16), triu_b)[0:1, :]
        onehot = jnp.where((upto - tot_row <= r_col) & (r_col < upto), 1.0, 0.0).astype(BF16)
        p_hi = jnp.floor(pe * (1.0 / _RANK_SPLIT))
        p_lo = pe - p_hi * _RANK_SPLIT
        ranks = _dot(onehot, p_hi.astype(BF16)) * _RANK_SPLIT + _dot(onehot, p_lo.astype(BF16))
        inside = jnp.where(ranks <= r_col, 1.0, 0.0).astype(BF16)
        cnt = _dot_nt(ones8, inside)[0:1, :] + float(LANES) * _dot_nt(blk_ids, onehot)[0:1, :]
        idx_ref[ex] = cnt.astype(I32)
        return carry

    lax.fori_loop(0, N_EXPERTS, per_expert, 0)


def _route(logits, cap):
    t = logits.shape[1]
    nblk = t // LANES
    idx, aff = pl.pallas_call(
        functools.partial(_route_kernel, nblk, cap),
        out_shape=[jax.ShapeDtypeStruct((N_EXPERTS, 1, cap), I32),
                   jax.ShapeDtypeStruct((N_EXPERTS, nblk, LANES), F32)],
        scratch_shapes=[pltpu.VMEM((N_EXPERTS, nblk, LANES), F32),
                        pltpu.VMEM((N_EXPERTS, nblk, LANES), F32)],
        compiler_params=_cparams(None, VMEM_LIMIT),
        name="route",
    )(logits.reshape(N_EXPERTS, nblk, LANES))
    return idx.reshape(N_EXPERTS, cap), aff.reshape(N_EXPERTS, t)


_UP_TF = 512
_UNROLL = 16


def _gather_rows(tile_ref, streams, ex, step, n_steps):
    row0 = 0
    for src_ref, idx_ref, cap in streams:
        cnt = cap // n_steps
        for u in range(cnt):
            r = step * cnt + u
            src = pl.multiple_of(idx_ref[ex * cap + r] * PACK_SLAB, PACK_SLAB)
            dst = pl.multiple_of((row0 + r) * PACK_SLAB, PACK_SLAB)
            tile_ref[pl.ds(dst, PACK_SLAB), :] = src_ref[pl.ds(src, PACK_SLAB), :]
        row0 += cap


def _moe_up_kernel(caps, ua_ref, ub_ref, ia_ref, ib_ref, wg_ref, wu_ref, h_ref, tile_ref, xe_ref):
    ex = pl.program_id(0)
    ff = pl.program_id(1)
    n_steps = EXPERT_FF // _UP_TF
    m = sum(caps)
    streams = ((ua_ref, ia_ref, caps[0]), (ub_ref, ib_ref, caps[1]))

    @pl.when((ex == 0) & (ff == 0))
    def _first_rows():
        def body(i, carry):
            row0 = 0
            for src_ref, idx_ref, cap in streams:
                @pl.when(i < cap // _UNROLL)
                def _(src_ref=src_ref, idx_ref=idx_ref, row0=row0):
                    for u in range(_UNROLL):
                        r = i * _UNROLL + u
                        src = pl.multiple_of(idx_ref[r] * PACK_SLAB, PACK_SLAB)
                        dst = pl.multiple_of((row0 + r) * PACK_SLAB, PACK_SLAB)
                        tile_ref[pl.ds(dst, PACK_SLAB), :] = src_ref[pl.ds(src, PACK_SLAB), :]
                row0 += cap
            return carry

        lax.fori_loop(0, max(caps) // _UNROLL, body, 0)

    @pl.when(ff == 0)
    def _unpack():
        half = D_MODEL // 2
        for j in range(PACK_SLAB):
            w = tile_ref[pl.ds(j, m, stride=PACK_SLAB), :]
            for k in range(2):
                v = pltpu.unpack_elementwise(w, index=k, packed_dtype=BF16, unpacked_dtype=F32)
                lo = k * half + j * LANES
                xe_ref[:, lo:lo + LANES] = v.astype(BF16)

    nxt = jnp.minimum(ex + 1, N_EXPERTS - 1)
    _gather_rows(tile_ref, streams, nxt, ff, n_steps)
    x = xe_ref[...]
    g = _dot(x, wg_ref[...].astype(BF16))
    u = _dot(x, wu_ref[...].astype(BF16))
    h_ref[...] = (_silu(g) * u).astype(BF16)


def _moe_up(u2p_a, u2p_b, idx_a, idx_b, w_gate, w_up, layer):
    caps = (idx_a.shape[1], idx_b.shape[1])
    m = sum(caps)
    smem = pl.BlockSpec(memory_space=pltpu.SMEM)
    w_spec = pl.BlockSpec((None, None, D_MODEL, _UP_TF), lambda e, f: (layer, e, 0, f))
    return pl.pallas_call(
        functools.partial(_moe_up_kernel, caps),
        out_shape=jax.ShapeDtypeStruct((N_EXPERTS, m, EXPERT_FF), BF16),
        grid=(N_EXPERTS, EXPERT_FF // _UP_TF),
        in_specs=[pl.BlockSpec(u2p_a.shape, lambda e, f: (0, 0), pipeline_mode=pl.Buffered(1)),
                  pl.BlockSpec(u2p_b.shape, lambda e, f: (0, 0), pipeline_mode=pl.Buffered(1)),
                  smem, smem, w_spec, w_spec],
        out_specs=pl.BlockSpec((None, m, _UP_TF), lambda e, f: (e, 0, f)),
        scratch_shapes=[pltpu.VMEM((m * PACK_SLAB, LANES), U32), pltpu.VMEM((m, D_MODEL), BF16)],
        compiler_params=_cparams(("arbitrary", "arbitrary"), VMEM_LIMIT),
        name="moe_up",
    )(u2p_a, u2p_b, idx_a.reshape(-1), idx_b.reshape(-1), w_gate, w_up)


_RMW = 16
_DOWN_ACC_LIMIT = 32 * 1024 * 1024


def _scatter_rows(acc_ref, slab_ref, idx_ref, aff_ref, idx0, src0, scale, rows):
    for b0 in range(0, rows, _RMW):
        dsts, vals = [], []
        for u in range(b0, b0 + _RMW):
            tok = idx_ref[idx0 + u]
            dst = pl.multiple_of(tok * ROW_SLAB, ROW_SLAB)
            src = pl.multiple_of(src0 + u * ROW_SLAB, ROW_SLAB)
            vals.append(acc_ref[pl.ds(dst, ROW_SLAB), :]
                        + (aff_ref[tok] * scale) * slab_ref[pl.ds(src, ROW_SLAB), :])
            dsts.append(dst)
        for dst, val in zip(dsts, vals):
            acc_ref[pl.ds(dst, ROW_SLAB), :] = val


def _moe_down_kernel(cap, tn, h_ref, wd_ref, idx_ref, affp_ref, aff_ref, out_ref, acc_ref,
                     slab_a_ref, slab_b_ref):
    ex = pl.program_id(0)
    nn = pl.program_id(1)
    n_steps = D_MODEL // tn
    n_half = tn // LANES
    part = cap // n_steps
    slabs = (slab_a_ref, slab_b_ref)

    @pl.when((ex == 0) & (nn == 0))
    def _zero():
        acc_ref[...] = jnp.zeros(acc_ref.shape, F32)
        slab_b_ref[...] = jnp.zeros(slab_b_ref.shape, F32)

    prev = jnp.maximum(ex - 1, 0)
    live = jnp.where(ex > 0, 1.0, 0.0)
    row0 = nn * part

    def stage(slab_prev_ref, slab_next_ref):
        _scatter_rows(acc_ref, slab_prev_ref, idx_ref, affp_ref, prev * cap + row0,
                      row0 * ROW_SLAB, live, part)
        ye = _dot(h_ref[...], wd_ref[...].astype(BF16))
        for j in range(n_half):
            slab_next_ref[pl.ds(nn * n_half + j, cap, stride=ROW_SLAB), :] = (
                ye[:, j * LANES:(j + 1) * LANES])

    for parity in range(2):
        @pl.when(ex % 2 == parity)
        def _(parity=parity):
            stage(slabs[1 - parity], slabs[parity])

    @pl.when((ex == N_EXPERTS - 1) & (nn == n_steps - 1))
    def _tail():
        def body(i, carry):
            _scatter_rows(acc_ref, slabs[(N_EXPERTS - 1) % 2], idx_ref, aff_ref,
                          ex * cap + i * _RMW, i * (_RMW * ROW_SLAB), 1.0, _RMW)
            return carry

        lax.fori_loop(0, cap // _RMW, body, 0)
        pltpu.sync_copy(acc_ref, out_ref)


def _moe_down(h, w_down, idx, aff, row_block, layer):
    cap = idx.shape[1]
    t = aff.shape[1]
    aff3 = aff[:, None, :]
    tn = D_MODEL if t * D_MODEL * 4 <= _DOWN_ACC_LIMIT else D_MODEL // 2
    gate_spec = lambda shift: pl.BlockSpec(
        (None, None, t), lambda e, n: (jnp.maximum(e - shift, 0), 0, 0), memory_space=pltpu.SMEM)
    return pl.pallas_call(
        functools.partial(_moe_down_kernel, cap, tn),
        out_shape=jax.ShapeDtypeStruct((t * ROW_SLAB, LANES), F32),
        grid=(N_EXPERTS, D_MODEL // tn),
        in_specs=[pl.BlockSpec((None, cap, EXPERT_FF), lambda e, n: (e, row_block, 0)),
                  pl.BlockSpec((None, None, EXPERT_FF, tn), lambda e, n: (layer, e, 0, n)),
                  pl.BlockSpec(memory_space=pltpu.SMEM), gate_spec(1), gate_spec(0)],
        out_specs=pl.BlockSpec(memory_space=pl.ANY),
        scratch_shapes=[pltpu.VMEM((t * ROW_SLAB, LANES), F32),
                        pltpu.VMEM((cap * ROW_SLAB, LANES), F32),
                        pltpu.VMEM((cap * ROW_SLAB, LANES), F32)],
        compiler_params=_cparams(("arbitrary", "arbitrary"), VMEM_LIMIT),
        name="moe_down",
    )(h, w_down, idx.reshape(-1), aff3, aff3)


def _swap_pairs(w):
    s = w.shape
    return jnp.flip(w.reshape(s[:-1] + (s[-1] // 2, 2)), axis=-1).reshape(s)


def _pack_w_in(w_in):
    o_dt, o_cq, o_kr, o_fin = 1280, 1296, 1680, 1712
    kr = w_in[:, :, o_kr:o_fin]
    pad = jnp.zeros(w_in.shape[:2] + (_IN_KR4 - _IN_SMALL - 2 * SSD_HEADS,), w_in.dtype)
    parts = [w_in[:, :, :o_dt], w_in[:, :, o_cq:o_kr], w_in[:, :, o_fin:], w_in[:, :, o_dt:o_cq], pad,
             jnp.tile(kr, (1, 1, MLA_HEADS)), jnp.tile(_swap_pairs(kr), (1, 1, MLA_HEADS))]
    return jnp.concatenate(parts, axis=2).astype(BF16)


def _pack_w_uq(w_uq):
    w = w_uq.reshape(w_uq.shape[:2] + (MLA_HEADS, QK_NOPE + ROPE_DIM))
    flat = lambda v: v.reshape(v.shape[:2] + (-1,))
    rope = w[..., QK_NOPE:]
    return jnp.concatenate([flat(w[..., :QK_NOPE]), flat(rope), flat(_swap_pairs(rope))],
                           axis=2).astype(BF16)


def _pack_w_ukv(w_ukv):
    w = w_ukv.reshape(w_ukv.shape[:2] + (MLA_HEADS, QK_NOPE + V_DIM))
    flat = lambda v: v.reshape(v.shape[:2] + (-1,))
    return jnp.concatenate([flat(w[..., :QK_NOPE]), flat(w[..., QK_NOPE:])], axis=2).astype(BF16)


def _rope_tables(seq):
    rows = seq // GRID_W
    r, col = jnp.meshgrid(jnp.arange(rows, dtype=F32), jnp.arange(GRID_W, dtype=F32), indexing="ij")
    pairs = ROPE_DIM // 4
    inv = ROPE_BASE ** (-jnp.arange(pairs, dtype=F32) / pairs)
    ang = jnp.concatenate([r.reshape(-1, 1) * inv, col.reshape(-1, 1) * inv], axis=-1)
    cos = jnp.repeat(jnp.cos(ang), 2, axis=1)
    sin = jnp.repeat(jnp.sin(ang), 2, axis=1) * jnp.tile(jnp.array([-1.0, 1.0], F32), ROPE_DIM // 2)
    return jnp.tile(cos, (1, MLA_HEADS)), jnp.tile(sin, (1, MLA_HEADS))


def _pad_lanes(v):
    return jnp.pad(v, ((0, 0), (0, LANES - v.shape[1])))[:, None, :]


def kernel(x_prompt, x_sample, cache_ckv, cache_krope, state_ssm, c, c_ctx,
           w_in, conv_w, conv_b, dt_bias, a_log, d_skip, ssd_norm_g,
           q_norm_g, w_uq, kv_norm_g, w_ukv, w_out, w_mod, b_mod,
           ln1_g, ln1_b, ln2_g, ln2_b, w_router, w_gate, w_up, w_down):
    nb_c, seq_c, _ = x_prompt.shape
    nb_l, seq_l, _ = x_sample.shape
    t_c, t_l = nb_c * seq_c, nb_l * seq_l
    caps = (CAPACITY_FACTOR * t_c // N_EXPERTS, CAPACITY_FACTOR * t_l // N_EXPERTS)

    w_in_p = _pack_w_in(w_in)
    wq = _pack_w_uq(w_uq)
    wkv = _pack_w_ukv(w_ukv)
    row = lambda v: v[:, None, :]
    lp = dict(conv_w=conv_w, conv_b=row(conv_b), dt_bias=_pad_lanes(dt_bias.reshape(DEPTH, -1)),
              a_log=_pad_lanes(a_log.reshape(DEPTH, -1)),
              d_skip=row(jnp.repeat(d_skip, SSD_HEADDIM, axis=1)), ssd_norm_g=row(ssd_norm_g),
              q_norm_g=row(q_norm_g), wq=wq, kv_norm_g=row(kv_norm_g), wkv=wkv,
              w_out=w_out.astype(BF16), ln1_g=row(ln1_g), ln1_b=row(ln1_b),
              wrt=jnp.swapaxes(w_router, 1, 2).astype(BF16))
    ln2 = (row(ln2_g), row(ln2_b))
    cache_kr4 = jnp.tile(cache_krope, (1, 1, 1, MLA_HEADS))
    cos4, sin4 = _rope_tables(seq_l)

    cond = jnp.zeros((SUBLANES, D_MODEL), F32).at[0].set(c_ctx).at[1:1 + nb_l].set(c)
    mod_all = _modulation(cond, w_mod, b_mod).reshape(DEPTH, SUBLANES, 6, D_MODEL)

    streams = [
        dict(x=x_prompt.reshape(t_c, D_MODEL), nb=nb_c, seq=seq_c, cap=caps[0], mod_rows=(0, 1)),
        dict(x=x_sample.reshape(t_l, D_MODEL), nb=nb_l, seq=seq_l, cap=caps[1], mod_rows=(1, nb_l)),
    ]
    prev = [None, None]
    ssm_hist = jnp.zeros((nb_c, DEPTH) + state_ssm.shape[2:], F32)
    kv_hist = (jnp.zeros((nb_c, DEPTH, seq_c, KV_LORA), F32),
               jnp.zeros((nb_c, DEPTH, seq_c, ROPE_DIM), F32))
    for l in range(DEPTH):
        routed = []
        for si, st in enumerate(streams):
            nb, seq, mod_rows = st["nb"], st["seq"], st["mod_rows"]
            x, z, xbc, cq, ckv, fin, small, kr = _in_proj(
                st["x"], prev[si], mod_all, mod_rows, w_in_p, seq, l)
            if si == 0:
                yssd, ssm_hist = _ssd(z, xbc, small, lp, nb, seq, None, ssm_hist, l)
                ymla, *kv_hist = _mla(cq, ckv, kr, lp, nb, seq, None, kv_hist, l)
            else:
                yssd, _ = _ssd(z, xbc, small, lp, nb, seq, state_ssm, None, l)
                (ymla,) = _mla(cq, ckv, kr, lp, nb, seq, (cache_ckv, cache_kr4, cos4, sin4), None, l)
            yfft = _fnet(fin, nb, seq)
            x1, u2p, logits = _out_proj(x, yssd, ymla, yfft, mod_all, mod_rows, lp, seq, l)
            idx, aff = _route(logits, st["cap"])
            st["x"] = x1
            routed.append((u2p, idx, aff))
        h = _moe_up(routed[0][0], routed[1][0], routed[0][1], routed[1][1], w_gate, w_up, l)
        for si in range(2):
            moe = _moe_down(h, w_down, routed[si][1], routed[si][2], 0 if si == 0 else 2, l)
            prev[si] = (moe,) + ln2
    last = DEPTH - 1
    y_p, y_s = [_final_norm(st["x"], *prev[si], mod_all, st["mod_rows"], st["seq"], last)
                for si, st in enumerate(streams)]
    y_p, y_s = y_p.reshape(x_prompt.shape), y_s.reshape(x_sample.shape)
    return (y_p, y_s, kv_hist[0], kv_hist[1], ssm_hist)
```

```python
import functools

import jax
import jax.numpy as jnp
import numpy as np
from jax import lax
from jax.experimental import pallas as pl
from jax.experimental.pallas import tpu as pltpu

F32 = jnp.float32
BF16 = jnp.bfloat16
I32 = jnp.int32
U32 = jnp.uint32

D_MODEL = 1024
DEPTH = 4
GRID_W = 64
SSD_HEADS = 8
SSD_HEADDIM = 64
SSD_DIM = SSD_HEADS * SSD_HEADDIM
SSD_STATE = 64
SSD_CONV = 5
SSD_CHUNK = 128
SSD_CONV_DIM = SSD_DIM + 4 * SSD_STATE
MLA_HEADS = 4
Q_LORA = 256
KV_LORA = 128
QK_NOPE = 64
ROPE_DIM = 32
V_DIM = 64
MLA_DIM = MLA_HEADS * V_DIM
MLA_SCALE = (QK_NOPE + ROPE_DIM) ** -0.5
ROPE_BASE = 10000.0
Q_TILE = 256
FNET_GW = 64
FNET_DIM = 256
N_EXPERTS = 16
EXPERT_FF = 1024
CAPACITY_FACTOR = 2
DN_ALPHA = (2 * DEPTH) ** 0.25
EPS = 1e-5
_LOG2E = 1.4426950408889634

LANES = 128
SUBLANES = 8
ROW_SLAB = D_MODEL // LANES
PACK_SLAB = ROW_SLAB // 2
VMEM_LIMIT = 60 * 1024 * 1024

_IN_XBC = (0, 768)
_IN_CQ = (768, 256)
_IN_Z = (1024, 512)
_IN_FIN = (1536, 256)
_IN_CKV = (1792, 128)
_IN_SMALL = (1920, 128)
_IN_KR4 = (2048, 256)
IN_PAD = 2304


def _segment_spec(rows, segment):
    start, width = segment
    return pl.BlockSpec((rows, width), lambda b: (b, start // width))


def _cparams(sem, vmem=None):
    return pltpu.CompilerParams(dimension_semantics=sem, vmem_limit_bytes=vmem)


def _ln(x):
    mu = jnp.mean(x, axis=-1, keepdims=True)
    xc = x - mu
    var = jnp.mean(xc * xc, axis=-1, keepdims=True)
    return xc * lax.rsqrt(var + EPS)


def _rms(x):
    return x * lax.rsqrt(jnp.mean(x * x, axis=-1, keepdims=True) + EPS)


def _silu(x):
    return x * (1.0 / (1.0 + jnp.exp(-x)))


def _dot(a, b):
    return jnp.dot(a, b, preferred_element_type=F32)


def _dot_nt(a, b):
    return lax.dot_general(a, b, (((1,), (1,)), ((), ())), preferred_element_type=F32)


def _split3(x):
    hi = x.astype(BF16)
    r1 = x - hi.astype(F32)
    mid = r1.astype(BF16)
    lo = (r1 - mid.astype(F32)).astype(BF16)
    return hi, mid, lo


def _const_spec(shape):
    return pl.BlockSpec(shape, lambda *_: (0,) * len(shape))


def _layer_spec(shape, layer):
    return pl.BlockSpec((None,) + shape, lambda *_: (layer,) + (0,) * len(shape))


_MOD_TN = 1536


def _mod_kernel(c_ref, w_ref, b_ref, o_ref):
    a = _silu(c_ref[...]).astype(BF16)
    o_ref[...] = _dot(a, w_ref[...].astype(BF16)) + b_ref[...]


def _modulation(cond, w_mod, b_mod):
    n = 6 * D_MODEL
    return pl.pallas_call(
        _mod_kernel,
        out_shape=jax.ShapeDtypeStruct((DEPTH, SUBLANES, n), F32),
        grid=(DEPTH, n // _MOD_TN),
        in_specs=[
            pl.BlockSpec((SUBLANES, D_MODEL), lambda l, j: (0, 0)),
            pl.BlockSpec((None, D_MODEL, _MOD_TN), lambda l, j: (l, 0, j)),
            pl.BlockSpec((None, 1, _MOD_TN), lambda l, j: (l, 0, j)),
        ],
        out_specs=pl.BlockSpec((None, SUBLANES, _MOD_TN), lambda l, j: (l, 0, j)),
        compiler_params=_cparams(("arbitrary", "arbitrary")),
        name="modulation",
    )(cond, w_mod, b_mod.reshape(DEPTH, 1, n))


_TM = 512


def _read_rows(slab_ref, tm):
    return jnp.concatenate(
        [slab_ref[pl.ds(j, tm, stride=ROW_SLAB), :] for j in range(ROW_SLAB)], axis=-1)


def _post_norm2(x_ref, moe_ref, pmod_ref, g_ref, b_ref):
    moe = _read_rows(moe_ref, _TM)
    return _ln(DN_ALPHA * x_ref[...] + pmod_ref[5:6, :] * moe) * g_ref[...] + b_ref[...]


def _in_proj_kernel(first, *refs):
    if first:
        x_ref, mod_ref, w_ref, p_ref = refs
        x = x_ref[...]
    else:
        x_ref, moe_ref, pmod_ref, g_ref, b_ref, mod_ref, w_ref, xo_ref, p_ref = refs
        x = _post_norm2(x_ref, moe_ref, pmod_ref, g_ref, b_ref)
        xo_ref[...] = x
    u = _ln(x) * (1.0 + mod_ref[1:2, :]) + mod_ref[0:1, :]
    p_ref[...] = _dot(u.astype(BF16), w_ref[...])


def _mod_spec(mod_rows, t, seq, layer):
    row0, count = mod_rows
    per = seq // _TM if count > 1 else t // _TM
    return pl.BlockSpec((None, None, 6, D_MODEL), lambda i: (layer, row0 + i // per, 0, 0))


def _row_spec(w):
    return pl.BlockSpec((_TM, w), lambda i: (i, 0))


_SLAB_SPEC = pl.BlockSpec((_TM * ROW_SLAB, LANES), lambda i: (i, 0))


def _in_proj(x, prev, mod_all, mod_rows, w_in_p, seq, layer):
    t = x.shape[0]
    mod_spec = _mod_spec(mod_rows, t, seq, layer)
    outs = [jax.ShapeDtypeStruct((t, IN_PAD), F32)]
    out_specs = [_row_spec(IN_PAD)]
    w_spec = _layer_spec((D_MODEL, IN_PAD), layer)
    if prev is None:
        ins = [x, mod_all, w_in_p]
        in_specs = [_row_spec(D_MODEL), mod_spec, w_spec]
    else:
        moe_slab, g, b = prev
        ins = [x, moe_slab, mod_all, g, b, mod_all, w_in_p]
        in_specs = [_row_spec(D_MODEL), _SLAB_SPEC, _mod_spec(mod_rows, t, seq, layer - 1),
                    _layer_spec((1, D_MODEL), layer - 1), _layer_spec((1, D_MODEL), layer - 1),
                    mod_spec, w_spec]
        outs = [jax.ShapeDtypeStruct((t, D_MODEL), F32)] + outs
        out_specs = [_row_spec(D_MODEL)] + out_specs
    res = pl.pallas_call(
        functools.partial(_in_proj_kernel, prev is None),
        out_shape=outs, grid=(t // _TM,), in_specs=in_specs, out_specs=out_specs,
        compiler_params=_cparams(("arbitrary",), VMEM_LIMIT),
        name="in_proj",
    )(*ins)
    if prev is None:
        return (x,) + tuple(res)
    return tuple(res)


def _final_norm_kernel(x_ref, moe_ref, pmod_ref, g_ref, b_ref, o_ref):
    o_ref[...] = _post_norm2(x_ref, moe_ref, pmod_ref, g_ref, b_ref)


def _final_norm(x, moe_slab, g, b, mod_all, mod_rows, seq, layer):
    t = x.shape[0]
    return pl.pallas_call(
        _final_norm_kernel,
        out_shape=jax.ShapeDtypeStruct((t, D_MODEL), F32), grid=(t // _TM,),
        in_specs=[_row_spec(D_MODEL), _SLAB_SPEC, _mod_spec(mod_rows, t, seq, layer),
                  _layer_spec((1, D_MODEL), layer), _layer_spec((1, D_MODEL), layer)],
        out_specs=_row_spec(D_MODEL),
        compiler_params=_cparams(("arbitrary",)),
        name="final_norm",
    )(x, moe_slab, mod_all, g, b)


def _ssd_kernel(has_h0, write_h, seq, *refs):
    refs = list(refs)
    z_ref, xbc_ref, sm_ref, cw_ref, cb_ref, dtb_ref, alog_ref, dsk_ref, ng_ref = refs[:9]
    pos = 9
    if has_h0:
        h0_ref = refs[pos]
    pos += 1
    y_ref = refs[pos]
    pos += 1
    if write_h:
        hf_ref = refs[pos]
        pos += 1
    pad_ref, xs_ref, cm_ref, gm_ref, bt_ref, acol_ref, rows_ref, st_ref = refs[pos:]
    ck = SSD_CHUNK
    nc = seq // ck
    halo = SUBLANES
    nh = SSD_HEADS
    nh2 = 2 * nh

    pad_ref[0:halo, :] = jnp.zeros((halo, SSD_CONV_DIM), F32)
    pad_ref[halo + seq:2 * halo + seq, :] = jnp.zeros((halo, SSD_CONV_DIM), F32)
    pad_ref[halo:halo + seq, :] = xbc_ref[...]
    ri = lax.broadcasted_iota(I32, (ck, ck), 0)
    ci = lax.broadcasted_iota(I32, (ck, ck), 1)
    triu = jnp.where(ri <= ci, 1.0, 0.0).astype(BF16)
    lo_half = ci < SSD_STATE
    lo64 = lax.broadcasted_iota(I32, (SSD_STATE, LANES), 1) < SSD_STATE
    zeros_half = jnp.zeros((SSD_STATE, LANES), BF16)
    fwd_rows = lax.broadcasted_iota(I32, (nh2, ck), 0) < nh

    def prepare_chunk(c, carry):
        cs = pl.multiple_of(c * ck, ck)
        rows = pl.ds(cs, ck)
        cw_blk = 2 * LANES
        for cb in range(SSD_CONV_DIM // cw_blk):
            cols = slice(cb * cw_blk, (cb + 1) * cw_blk)
            win = pad_ref[pl.ds(cs, ck + 2 * halo), cols]
            acc = jnp.zeros((ck, cw_blk), F32) + cb_ref[:, cols]
            for k in range(SSD_CONV):
                s0 = halo + k - (SSD_CONV - 1) // 2
                acc = acc + win[s0:s0 + ck, :] * cw_ref[k:k + 1, cols]
            act = _silu(acc)
            if cb * cw_blk < SSD_DIM:
                xs_ref[rows, cols] = act
                y_ref[rows, cols] = act * dsk_ref[:, cols]
        bm = act[:, :LANES]
        cm = act[:, LANES:]
        cm_ref[rows, :] = cm
        bt_ref[rows, :] = bm.T
        cg2 = jnp.concatenate([jnp.where(lo_half, cm, 0.0), jnp.where(lo_half, 0.0, cm)], axis=0)
        gm2 = _dot_nt(cg2.astype(BF16), bm.astype(BF16))
        gm_ref[rows, 0:LANES] = gm2[:ck]
        gm_ref[rows, LANES:] = gm2[ck:]

        xr = sm_ref[rows, :] + dtb_ref[...]
        dt = jnp.maximum(xr, 0.0) + jnp.log(1.0 + jnp.exp(-jnp.abs(xr)))
        ad = jnp.where(ci < nh2, dt * -jnp.exp(alog_ref[...]),
                       jnp.where(ci < 2 * nh2, pltpu.roll(dt, nh2, axis=1), 0.0))
        adt = ad.T
        at16 = adt[0:nh2, :]
        dt16 = adt[nh2:2 * nh2, :]
        csum = _dot(jnp.concatenate(_split3(at16), axis=0), triu)
        arow = csum[0:nh2] + csum[nh2:2 * nh2] + csum[2 * nh2:3 * nh2]
        acol = jnp.concatenate([arow, jnp.zeros((ck - nh2, ck), F32)], axis=0).T
        alast = jnp.broadcast_to(arow[:, ck - 1:ck], (nh2, ck)) * _LOG2E
        arow_s = jnp.where(fwd_rows, arow, arow - at16) * _LOG2E
        ldt = jnp.log2(dt16)
        rowarg = jnp.where(fwd_rows, arow_s - ldt, arow_s + ldt)
        wrow = jnp.exp2(jnp.where(fwd_rows, alast - arow_s, arow_s))
        acol_ref[rows, :] = jnp.where(ci < nh, acol, acol - ad) * _LOG2E
        rows_ref[pl.ds(pl.multiple_of(c * (4 * nh2), 4 * nh2), 4 * nh2), :] = jnp.concatenate(
            [rowarg, wrow * dt16, jnp.exp2(alast), alast], axis=0)
        return carry

    lax.fori_loop(0, nc, prepare_chunk, 0)

    if has_h0:
        for d in range(2):
            h0 = h0_ref[d].reshape(SSD_DIM, SSD_STATE)
            h0 = jnp.concatenate([h0, jnp.zeros((SSD_DIM, LANES - SSD_STATE), F32)], axis=1)
            st_ref[d] = h0.T[:SSD_STATE, :]
    else:
        st_ref[...] = jnp.zeros(st_ref.shape, F32)

    def chunk_dir(d, c):
        cs = pl.multiple_of(c * ck, ck)
        acol = acol_ref[pl.ds(cs, ck), :]
        rws = rows_ref[pl.ds(pl.multiple_of(c * (4 * nh2), 4 * nh2), 4 * nh2), :]
        cm = cm_ref[pl.ds(cs, ck), :]
        cg = [jnp.where(lo_half, cm, 0.0), jnp.where(lo_half, 0.0, cm)]
        for q in range(nh // 2):
            g = q // 2
            gmat = gm_ref[pl.ds(cs, ck), g * LANES:(g + 1) * LANES]
            btg = bt_ref[pl.ds(cs + g * SSD_STATE, SSD_STATE), :]
            lhs_rows = []
            decl = []
            for hh in range(2):
                hi = d * nh + 2 * q + hh
                acol_b = jnp.broadcast_to(acol[:, hi:hi + 1], (ck, ck))
                rowarg = rws[hi:hi + 1, :]
                if d == 0:
                    lmat = jnp.where(ri >= ci, jnp.exp2(acol_b - rowarg), 0.0)
                    ecol = jnp.exp2(acol_b)
                else:
                    lmat = jnp.where(ci >= ri, jnp.exp2(rowarg - acol_b), 0.0)
                    ecol = jnp.exp2(rws[3 * nh2 + hi:3 * nh2 + hi + 1, :] - acol_b)
                mm = (gmat * lmat).astype(BF16)
                cs_h = (cg[g] * ecol).astype(BF16)
                bw = (btg * rws[nh2 + hi:nh2 + hi + 1, :]).astype(BF16)
                lhs_rows.append(jnp.concatenate([mm, cs_h], axis=1))
                lhs_rows.append(jnp.concatenate([bw, zeros_half], axis=1))
                decl.append(rws[2 * nh2 + hi:2 * nh2 + hi + 1, :])
            lhs = jnp.concatenate(lhs_rows, axis=0)
            cols = slice(q * LANES, (q + 1) * LANES)
            st_pair = st_ref[d, :, cols]
            stb = st_pair.astype(BF16)
            w_st = jnp.concatenate([stb, zeros_half] if g == 0 else [zeros_half, stb], axis=0)
            xs_pair = xs_ref[pl.ds(cs, ck), cols].astype(BF16)
            out = _dot(lhs, jnp.concatenate([xs_pair, w_st], axis=0))
            r2 = ck + SSD_STATE
            y_pair = jnp.where(lo_half, out[0:ck], out[r2:r2 + ck])
            st_new = jnp.where(lo64, out[ck:r2], out[r2 + ck:2 * r2])
            dec = jnp.where(lo64, decl[0], decl[1])
            st_ref[d, :, cols] = st_pair * dec + st_new
            y_ref[pl.ds(cs, ck), cols] += y_pair

    def body(c, carry):
        chunk_dir(0, c)
        chunk_dir(1, nc - 1 - c)
        return carry

    lax.fori_loop(0, nc, body, 0)

    def gate_chunk(c, carry):
        rows = pl.ds(pl.multiple_of(c * ck, ck), ck)
        y = y_ref[rows, :] * _silu(z_ref[rows, :])
        y_ref[rows, :] = _rms(y) * ng_ref[...]
        return carry

    lax.fori_loop(0, nc, gate_chunk, 0)

    if write_h:
        for d in range(2):
            st = jnp.concatenate([st_ref[d], jnp.zeros((LANES - SSD_STATE, SSD_DIM), F32)], axis=0)
            hf_ref[d] = st.T[:, :SSD_STATE].reshape(SSD_HEADS, SSD_HEADDIM, SSD_STATE)


def _ssd(proj, lp, nb, seq, h0, hist, layer):
    has_h0 = h0 is not None
    write_h = not has_h0
    seq_spec = lambda w: pl.BlockSpec((seq, w), lambda b: (b, 0))
    ins = [proj, proj, proj, lp["conv_w"], lp["conv_b"], lp["dt_bias"], lp["a_log"], lp["d_skip"],
           lp["ssd_norm_g"]]
    in_specs = [_segment_spec(seq, _IN_Z), _segment_spec(seq, _IN_XBC), _segment_spec(seq, _IN_SMALL),
                _layer_spec((SSD_CONV, SSD_CONV_DIM), layer), _layer_spec((1, SSD_CONV_DIM), layer),
                _layer_spec((1, LANES), layer), _layer_spec((1, LANES), layer),
                _layer_spec((1, SSD_DIM), layer), _layer_spec((1, SSD_DIM), layer)]
    hshape = (2, SSD_HEADS, SSD_HEADDIM, SSD_STATE)
    layer_block = pl.BlockSpec((None, None) + hshape, lambda b: (b, layer, 0, 0, 0, 0))
    outs = [jax.ShapeDtypeStruct((nb * seq, SSD_DIM), F32)]
    out_specs = [seq_spec(SSD_DIM)]
    aliases = {}
    if has_h0:
        ins.append(h0)
        in_specs.append(layer_block)
    else:
        ins.append(hist)
        in_specs.append(pl.BlockSpec(memory_space=pl.ANY))
        aliases = {len(ins) - 1: 1}
        outs.append(jax.ShapeDtypeStruct(hist.shape, F32))
        out_specs.append(layer_block)
    res = pl.pallas_call(
        functools.partial(_ssd_kernel, has_h0, write_h, seq),
        out_shape=outs, grid=(nb,), in_specs=in_specs, out_specs=out_specs,
        input_output_aliases=aliases,
        scratch_shapes=[pltpu.VMEM((seq + 2 * SUBLANES, SSD_CONV_DIM), F32),
                        pltpu.VMEM((seq, SSD_DIM), F32),
                        pltpu.VMEM((seq, LANES), F32),
                        pltpu.VMEM((seq, 2 * LANES), F32),
                        pltpu.VMEM((seq, LANES), F32),
                        pltpu.VMEM((seq, LANES), F32),
                        pltpu.VMEM((seq // SSD_CHUNK * 8 * SSD_HEADS, LANES), F32),
                        pltpu.VMEM((2, SSD_STATE, SSD_DIM), F32)],
        compiler_params=_cparams(("arbitrary",), VMEM_LIMIT),
        name="ssd",
    )(*ins)
    return res if write_h else (res[0], None)


_MLA_GROUP_ROWS = 1024


def _mla_kernel(has_ctx, seq, group, *refs):
    refs = list(refs)
    cq_ref, ckv_ref, kr_ref, qg_ref, wq_ref, kvg_ref, wkv_ref = refs[:7]
    pos = 7
    if has_ctx:
        cckv_ref, ckr_ref, cos_ref, sin_ref = refs[pos:pos + 4]
        pos += 4
    else:
        pos += 2
    y_ref = refs[pos]
    pos += 1
    if not has_ctx:
        ckvo_ref, kro_ref = refs[pos:pos + 2]

    qn = (_rms(cq_ref[...]) * qg_ref[...]).astype(BF16)
    qall = _dot(qn, wq_ref[...])
    q_nope = qall[:, :MLA_DIM]
    q_rope = qall[:, MLA_DIM:MLA_DIM + LANES]
    ckv_n = _rms(ckv_ref[...]) * kvg_ref[...]
    kr4 = kr_ref[:, :LANES]
    if has_ctx:
        q_rope = q_rope * cos_ref[...] + qall[:, MLA_DIM + LANES:] * sin_ref[...]
        kr4 = kr4 * cos_ref[...] + kr_ref[:, LANES:] * sin_ref[...]
        ckv_all = jnp.concatenate([cckv_ref[...], ckv_n], axis=0)
        kr_all = jnp.concatenate([ckr_ref[...], kr4], axis=0)
    else:
        ckvo_ref[...] = ckv_n.reshape(group, seq, KV_LORA)
        kro_ref[...] = kr4[:, :ROPE_DIM].reshape(group, seq, ROPE_DIM)
        ckv_all = ckv_n
        kr_all = kr4
    kv = _dot(ckv_all.astype(BF16), wkv_ref[...])
    krb = kr_all.astype(BF16)
    lane = lax.broadcasted_iota(I32, (Q_TILE, LANES), 1)
    lo_half = lane < QK_NOPE
    n_keys = kv.shape[0] // group
    for qt in range(group * seq // Q_TILE):
        rows = slice(qt * Q_TILE, (qt + 1) * Q_TILE)
        s0 = (qt * Q_TILE // seq) * n_keys
        keys = slice(s0, s0 + n_keys)
        qr = q_rope[rows] * MLA_SCALE
        for pair in range(MLA_HEADS // 2):
            cols = slice(pair * LANES, (pair + 1) * LANES)
            kcat = jnp.concatenate([kv[keys, cols].astype(BF16), krb[keys]], axis=1)
            vb = kv[keys, MLA_DIM + pair * LANES:MLA_DIM + (pair + 1) * LANES].astype(BF16)
            qn_pair = q_nope[rows, cols] * MLA_SCALE
            outs = []
            for hh in range(2):
                h = 2 * pair + hh
                qa = jnp.where(lo_half if hh == 0 else jnp.logical_not(lo_half), qn_pair, 0.0)
                qb = jnp.where((lane >= h * ROPE_DIM) & (lane < (h + 1) * ROPE_DIM), qr, 0.0)
                s = _dot_nt(jnp.concatenate([qa, qb], axis=1).astype(BF16), kcat)
                p = jnp.exp(s - jnp.max(s, axis=-1, keepdims=True))
                o = _dot(p.astype(BF16), vb)
                outs.append(o / jnp.sum(p, axis=-1, keepdims=True))
            y_ref[rows, cols] = jnp.where(lo_half, outs[0], outs[1])


def _mla(proj, lp, nb, seq, ctx, hist, layer):
    has_ctx = ctx is not None
    group = 1 if has_ctx else _MLA_GROUP_ROWS // seq
    seq_spec = lambda w: pl.BlockSpec((group * seq, w), lambda b: (b, 0))
    ins = [proj, proj, proj, lp["q_norm_g"], lp["wq"], lp["kv_norm_g"], lp["wkv"]]
    rows = group * seq
    in_specs = [_segment_spec(rows, _IN_CQ), _segment_spec(rows, _IN_CKV), _segment_spec(rows, _IN_KR4),
                _layer_spec((1, Q_LORA), layer), _layer_spec((Q_LORA, 2 * MLA_DIM), layer),
                _layer_spec((1, KV_LORA), layer), _layer_spec((KV_LORA, 2 * MLA_DIM), layer)]
    outs = [jax.ShapeDtypeStruct((nb * seq, MLA_DIM), F32)]
    out_specs = [seq_spec(MLA_DIM)]
    if has_ctx:
        cache_ckv, cache_kr4, cos4, sin4 = ctx
        past = cache_ckv.shape[2]
        ins += [cache_ckv, cache_kr4, cos4, sin4]
        in_specs += [pl.BlockSpec((None, None, past, KV_LORA), lambda b: (b, layer, 0, 0)),
                     pl.BlockSpec((None, None, past, LANES), lambda b: (b, layer, 0, 0)),
                     _const_spec((seq, LANES)), _const_spec((seq, LANES))]
    else:
        aliases = {len(ins): 1, len(ins) + 1: 2}
        ins += list(hist)
        in_specs += [pl.BlockSpec(memory_space=pl.ANY)] * 2
        outs += [jax.ShapeDtypeStruct(h.shape, F32) for h in hist]
        out_specs += [pl.BlockSpec((group, None, seq, KV_LORA), lambda b: (b, layer, 0, 0)),
                      pl.BlockSpec((group, None, seq, ROPE_DIM), lambda b: (b, layer, 0, 0))]
    return pl.pallas_call(
        functools.partial(_mla_kernel, has_ctx, seq, group),
        out_shape=outs, grid=(nb // group,), in_specs=in_specs, out_specs=out_specs,
        input_output_aliases={} if has_ctx else aliases,
        compiler_params=_cparams(("arbitrary",), VMEM_LIMIT),
        name="mla",
    )(*ins)


_FNET_ROWS = 1024


def _fnet_kernel(seq, g_ref, cbd_ref, sbd_ref, cs_ref, o_ref):
    gb = g_ref[...].astype(BF16)
    gc = _dot(gb, cbd_ref[...].astype(BF16)).astype(BF16)
    gs = _dot(gb, sbd_ref[...].astype(BF16)).astype(BF16)
    cs = cs_ref[...].astype(BF16)
    for i in range(_FNET_ROWS // seq):
        rows = slice(i * seq, (i + 1) * seq)
        o_ref[rows, :] = _dot(cs, jnp.concatenate([gc[rows], gs[rows]], axis=0))


def _dft_constants(seq):
    k = np.arange(FNET_GW)
    ang = 2.0 * np.pi * np.outer(k, k) / FNET_GW
    eye = np.eye(FNET_DIM // FNET_GW)
    cbd = np.kron(eye, np.cos(ang)) / np.sqrt(FNET_GW)
    sbd = np.kron(eye, np.sin(ang)) / np.sqrt(FNET_GW)
    s = np.arange(seq)
    angs = 2.0 * np.pi * np.outer(s, s) / seq
    cs = np.concatenate([np.cos(angs), -np.sin(angs)], axis=1) / np.sqrt(seq)
    return (jnp.asarray(cbd, F32), jnp.asarray(sbd, F32), jnp.asarray(cs, F32))


def _fnet(proj, nb, seq):
    cbd, sbd, cs = _dft_constants(seq)
    return pl.pallas_call(
        functools.partial(_fnet_kernel, seq),
        out_shape=jax.ShapeDtypeStruct((nb * seq, FNET_DIM), F32), grid=(nb * seq // _FNET_ROWS,),
        in_specs=[_segment_spec(_FNET_ROWS, _IN_FIN),
                  _const_spec((FNET_DIM, FNET_DIM)), _const_spec((FNET_DIM, FNET_DIM)),
                  _const_spec((seq, 2 * seq))],
        out_specs=pl.BlockSpec((_FNET_ROWS, FNET_DIM), lambda b: (b, 0)),
        compiler_params=_cparams(("arbitrary",), VMEM_LIMIT),
        name="fnet",
    )(proj, cbd, sbd, cs)


def _out_proj_kernel(x_ref, ys_ref, ym_ref, yf_ref, mod_ref, w_ref, g_ref, b_ref, wr_ref,
                     x1_ref, u2p_ref, lg_ref):
    y = jnp.concatenate([ys_ref[...].astype(BF16), ym_ref[...].astype(BF16),
                         yf_ref[...].astype(BF16)], axis=1)
    mix = _dot(y, w_ref[...])
    x1 = _ln(DN_ALPHA * x_ref[...] + mod_ref[2:3, :] * mix) * g_ref[...] + b_ref[...]
    x1_ref[...] = x1
    u2 = _ln(x1) * (1.0 + mod_ref[4:5, :]) + mod_ref[3:4, :]
    lg_ref[...] = _dot_nt(wr_ref[...], u2.astype(BF16))
    half = D_MODEL // 2
    words = pltpu.pack_elementwise([u2[:, :half], u2[:, half:]], packed_dtype=BF16)
    for j in range(PACK_SLAB):
        u2p_ref[pl.ds(j, _TM, stride=PACK_SLAB), :] = words[:, j * LANES:(j + 1) * LANES]


def _out_proj(x, yssd, ymla, yfft, mod_all, mod_rows, lp, seq, layer):
    t = x.shape[0]
    return pl.pallas_call(
        _out_proj_kernel,
        out_shape=[jax.ShapeDtypeStruct((t, D_MODEL), F32),
                   jax.ShapeDtypeStruct((t * PACK_SLAB, LANES), U32),
                   jax.ShapeDtypeStruct((N_EXPERTS, t), F32)],
        grid=(t // _TM,),
        in_specs=[_row_spec(D_MODEL), _row_spec(SSD_DIM), _row_spec(MLA_DIM), _row_spec(FNET_DIM),
                  _mod_spec(mod_rows, t, seq, layer), _layer_spec((D_MODEL, D_MODEL), layer),
                  _layer_spec((1, D_MODEL), layer), _layer_spec((1, D_MODEL), layer),
                  _layer_spec((N_EXPERTS, D_MODEL), layer)],
        out_specs=[_row_spec(D_MODEL),
                   pl.BlockSpec((_TM * PACK_SLAB, LANES), lambda i: (i, 0)),
                   pl.BlockSpec((N_EXPERTS, _TM), lambda i: (0, i))],
        compiler_params=_cparams(("arbitrary",), VMEM_LIMIT),
        name="out_proj",
    )(x, yssd, ymla, yfft, mod_all, lp["w_out"], lp["ln1_g"], lp["ln1_b"], lp["wrt"])


_RANK_SPLIT = 64.0


def _cumsum_tokens(x, triu, below):
    within = _dot(x.astype(BF16), triu)
    totals = jnp.broadcast_to(within[:, LANES - 1:LANES], within.shape)
    return within + _dot(below, totals.astype(BF16))


def _route_kernel(nblk, cap, lg_ref, idx_ref, aff_ref, p_ref, sel_ref):
    lg = lg_ref[...]
    e = jnp.exp(lg - jnp.max(lg, axis=0, keepdims=True))
    aff = e / jnp.sum(e, axis=0, keepdims=True)
    aff_ref[...] = aff
    capf = float(cap)

    def count(mask):
        s = jnp.sum(jnp.where(mask, 1.0, 0.0), axis=1, keepdims=True)
        return jnp.sum(s, axis=2, keepdims=True)

    def search(i, lo):
        cand = lo | jnp.left_shift(jnp.int32(1), 30 - i)
        return jnp.where(count(aff >= pltpu.bitcast(cand, F32)) >= capf, cand, lo)

    thr_bits = lax.fori_loop(0, 31, search, jnp.zeros((N_EXPERTS, 1, 1), I32))
    thr = pltpu.bitcast(thr_bits, F32)
    need = capf - count(aff > thr)

    rows = N_EXPERTS * nblk
    ri = lax.broadcasted_iota(I32, (LANES, LANES), 0)
    ci = lax.broadcasted_iota(I32, (LANES, LANES), 1)
    triu = jnp.where(ri <= ci, 1.0, 0.0).astype(BF16)
    rr = lax.broadcasted_iota(I32, (rows, rows), 0)
    rc = lax.broadcasted_iota(I32, (rows, rows), 1)
    below = jnp.where((rc < rr) & (rc >= (rr // nblk) * nblk), 1.0, 0.0).astype(BF16)
    flat = lambda v: v.reshape(rows, LANES)
    gt = flat(jnp.where(aff > thr, 1.0, 0.0))
    eq = flat(jnp.where(aff == thr, 1.0, 0.0))
    need_rows = flat(jnp.broadcast_to(need, aff.shape))
    eq_rank = _cumsum_tokens(eq, triu, below) - eq
    sel = gt + eq * jnp.where(eq_rank < need_rows, 1.0, 0.0)
    rank = _cumsum_tokens(sel, triu, below)
    sel_ref[...] = sel.reshape(N_EXPERTS, nblk, LANES)
    p_ref[...] = rank.reshape(N_EXPERTS, nblk, LANES)

    ones8 = jnp.ones((SUBLANES, LANES), BF16)
    bi = lax.broadcasted_iota(I32, (nblk, nblk), 0)
    bj = lax.broadcasted_iota(I32, (nblk, nblk), 1)
    triu_b = jnp.where(bi <= bj, 1.0, 0.0).astype(BF16)
    blk_ids = lax.broadcasted_iota(I32, (SUBLANES, nblk), 1).astype(F32).astype(BF16)
    r_col = lax.broadcasted_iota(I32, (cap, 1), 0).astype(F32)

    def per_expert(ex, carry):
        pe = p_ref[ex]
        tot_row = _dot_nt(ones8, sel_ref[ex].astype(BF16))[0:1, :]
        upto = _dot(jnp.broadcast_to(tot_row, (SUBLANES, nblk)).astype(BF16), triu_b)[0:1, :]
        onehot = jnp.where((upto - tot_row <= r_col) & (r_col < upto), 1.0, 0.0).astype(BF16)
        p_hi = jnp.floor(pe * (1.0 / _RANK_SPLIT))
        p_lo = pe - p_hi * _RANK_SPLIT
        ranks = _dot(onehot, p_hi.astype(BF16)) * _RANK_SPLIT + _dot(onehot, p_lo.astype(BF16))
        inside = jnp.where(ranks <= r_col, 1.0, 0.0).astype(BF16)
        cnt = _dot_nt(ones8, inside)[0:1, :] + float(LANES) * _dot_nt(blk_ids, onehot)[0:1, :]
        idx_ref[ex] = cnt.astype(I32)
        return carry

    lax.fori_loop(0, N_EXPERTS, per_expert, 0)


def _route(logits, cap):
    t = logits.shape[1]
    nblk = t // LANES
    idx, aff = pl.pallas_call(
        functools.partial(_route_kernel, nblk, cap),
        out_shape=[jax.ShapeDtypeStruct((N_EXPERTS, 1, cap), I32),
                   jax.ShapeDtypeStruct((N_EXPERTS, nblk, LANES), F32)],
        scratch_shapes=[pltpu.VMEM((N_EXPERTS, nblk, LANES), F32),
                        pltpu.VMEM((N_EXPERTS, nblk, LANES), F32)],
        compiler_params=_cparams(None, VMEM_LIMIT),
        name="route",
    )(logits.reshape(N_EXPERTS, nblk, LANES))
    return idx.reshape(N_EXPERTS, cap), aff.reshape(N_EXPERTS, t)


_UP_TF = 512
_UNROLL = 16


def _gather_rows(tile_ref, streams, ex, step, n_steps):
    row0 = 0
    for src_ref, idx_ref, cap in streams:
        cnt = cap // n_steps
        for u in range(cnt):
            r = step * cnt + u
            src = pl.multiple_of(idx_ref[ex * cap + r] * PACK_SLAB, PACK_SLAB)
            dst = pl.multiple_of((row0 + r) * PACK_SLAB, PACK_SLAB)
            tile_ref[pl.ds(dst, PACK_SLAB), :] = src_ref[pl.ds(src, PACK_SLAB), :]
        row0 += cap


def _moe_up_kernel(caps, ua_ref, ub_ref, ia_ref, ib_ref, wg_ref, wu_ref, h_ref, tile_ref, xe_ref):
    ex = pl.program_id(0)
    ff = pl.program_id(1)
    n_steps = EXPERT_FF // _UP_TF
    m = sum(caps)
    streams = ((ua_ref, ia_ref, caps[0]), (ub_ref, ib_ref, caps[1]))

    @pl.when((ex == 0) & (ff == 0))
    def _first_rows():
        def body(i, carry):
            row0 = 0
            for src_ref, idx_ref, cap in streams:
                @pl.when(i < cap // _UNROLL)
                def _(src_ref=src_ref, idx_ref=idx_ref, row0=row0):
                    for u in range(_UNROLL):
                        r = i * _UNROLL + u
                        src = pl.multiple_of(idx_ref[r] * PACK_SLAB, PACK_SLAB)
                        dst = pl.multiple_of((row0 + r) * PACK_SLAB, PACK_SLAB)
                        tile_ref[pl.ds(dst, PACK_SLAB), :] = src_ref[pl.ds(src, PACK_SLAB), :]
                row0 += cap
            return carry

        lax.fori_loop(0, max(caps) // _UNROLL, body, 0)

    @pl.when(ff == 0)
    def _unpack():
        half = D_MODEL // 2
        for j in range(PACK_SLAB):
            w = tile_ref[pl.ds(j, m, stride=PACK_SLAB), :]
            for k in range(2):
                v = pltpu.unpack_elementwise(w, index=k, packed_dtype=BF16, unpacked_dtype=F32)
                lo = k * half + j * LANES
                xe_ref[:, lo:lo + LANES] = v.astype(BF16)

    nxt = jnp.minimum(ex + 1, N_EXPERTS - 1)
    _gather_rows(tile_ref, streams, nxt, ff, n_steps)
    x = xe_ref[...]
    g = _dot(x, wg_ref[...].astype(BF16))
    u = _dot(x, wu_ref[...].astype(BF16))
    h_ref[...] = (_silu(g) * u).astype(BF16)


def _moe_up(u2p_a, u2p_b, idx_a, idx_b, w_gate, w_up, layer):
    caps = (idx_a.shape[1], idx_b.shape[1])
    m = sum(caps)
    smem = pl.BlockSpec(memory_space=pltpu.SMEM)
    w_spec = pl.BlockSpec((None, None, D_MODEL, _UP_TF), lambda e, f: (layer, e, 0, f))
    return pl.pallas_call(
        functools.partial(_moe_up_kernel, caps),
        out_shape=jax.ShapeDtypeStruct((N_EXPERTS, m, EXPERT_FF), BF16),
        grid=(N_EXPERTS, EXPERT_FF // _UP_TF),
        in_specs=[pl.BlockSpec(u2p_a.shape, lambda e, f: (0, 0), pipeline_mode=pl.Buffered(1)),
                  pl.BlockSpec(u2p_b.shape, lambda e, f: (0, 0), pipeline_mode=pl.Buffered(1)),
                  smem, smem, w_spec, w_spec],
        out_specs=pl.BlockSpec((None, m, _UP_TF), lambda e, f: (e, 0, f)),
        scratch_shapes=[pltpu.VMEM((m * PACK_SLAB, LANES), U32), pltpu.VMEM((m, D_MODEL), BF16)],
        compiler_params=_cparams(("arbitrary", "arbitrary"), VMEM_LIMIT),
        name="moe_up",
    )(u2p_a, u2p_b, idx_a.reshape(-1), idx_b.reshape(-1), w_gate, w_up)


_RMW = 16
_DOWN_ACC_LIMIT = 32 * 1024 * 1024


def _scatter_rows(acc_ref, slab_ref, idx_ref, aff_ref, idx0, src0, scale, rows):
    for b0 in range(0, rows, _RMW):
        dsts, vals = [], []
        for u in range(b0, b0 + _RMW):
            tok = idx_ref[idx0 + u]
            dst = pl.multiple_of(tok * ROW_SLAB, ROW_SLAB)
            src = pl.multiple_of(src0 + u * ROW_SLAB, ROW_SLAB)
            vals.append(acc_ref[pl.ds(dst, ROW_SLAB), :]
                        + (aff_ref[tok] * scale) * slab_ref[pl.ds(src, ROW_SLAB), :])
            dsts.append(dst)
        for dst, val in zip(dsts, vals):
            acc_ref[pl.ds(dst, ROW_SLAB), :] = val


def _moe_down_kernel(cap, tn, h_ref, wd_ref, idx_ref, affp_ref, aff_ref, out_ref, acc_ref,
                     slab_a_ref, slab_b_ref):
    ex = pl.program_id(0)
    nn = pl.program_id(1)
    n_steps = D_MODEL // tn
    n_half = tn // LANES
    part = cap // n_steps
    slabs = (slab_a_ref, slab_b_ref)

    @pl.when((ex == 0) & (nn == 0))
    def _zero():
        acc_ref[...] = jnp.zeros(acc_ref.shape, F32)
        slab_b_ref[...] = jnp.zeros(slab_b_ref.shape, F32)

    prev = jnp.maximum(ex - 1, 0)
    live = jnp.where(ex > 0, 1.0, 0.0)
    row0 = nn * part

    def stage(slab_prev_ref, slab_next_ref):
        _scatter_rows(acc_ref, slab_prev_ref, idx_ref, affp_ref, prev * cap + row0,
                      row0 * ROW_SLAB, live, part)
        ye = _dot(h_ref[...], wd_ref[...].astype(BF16))
        for j in range(n_half):
            slab_next_ref[pl.ds(nn * n_half + j, cap, stride=ROW_SLAB), :] = (
                ye[:, j * LANES:(j + 1) * LANES])

    for parity in range(2):
        @pl.when(ex % 2 == parity)
        def _(parity=parity):
            stage(slabs[1 - parity], slabs[parity])

    @pl.when((ex == N_EXPERTS - 1) & (nn == n_steps - 1))
    def _tail():
        def body(i, carry):
            _scatter_rows(acc_ref, slabs[(N_EXPERTS - 1) % 2], idx_ref, aff_ref,
                          ex * cap + i * _RMW, i * (_RMW * ROW_SLAB), 1.0, _RMW)
            return carry

        lax.fori_loop(0, cap // _RMW, body, 0)
        pltpu.sync_copy(acc_ref, out_ref)


def _moe_down(h, w_down, idx, aff, row_block, layer):
    cap = idx.shape[1]
    t = aff.shape[1]
    aff3 = aff[:, None, :]
    tn = D_MODEL if t * D_MODEL * 4 <= _DOWN_ACC_LIMIT else D_MODEL // 2
    gate_spec = lambda shift: pl.BlockSpec(
        (None, None, t), lambda e, n: (jnp.maximum(e - shift, 0), 0, 0), memory_space=pltpu.SMEM)
    return pl.pallas_call(
        functools.partial(_moe_down_kernel, cap, tn),
        out_shape=jax.ShapeDtypeStruct((t * ROW_SLAB, LANES), F32),
        grid=(N_EXPERTS, D_MODEL // tn),
        in_specs=[pl.BlockSpec((None, cap, EXPERT_FF), lambda e, n: (e, row_block, 0)),
                  pl.BlockSpec((None, None, EXPERT_FF, tn), lambda e, n: (layer, e, 0, n)),
                  pl.BlockSpec(memory_space=pltpu.SMEM), gate_spec(1), gate_spec(0)],
        out_specs=pl.BlockSpec(memory_space=pl.ANY),
        scratch_shapes=[pltpu.VMEM((t * ROW_SLAB, LANES), F32),
                        pltpu.VMEM((cap * ROW_SLAB, LANES), F32),
                        pltpu.VMEM((cap * ROW_SLAB, LANES), F32)],
        compiler_params=_cparams(("arbitrary", "arbitrary"), VMEM_LIMIT),
        name="moe_down",
    )(h, w_down, idx.reshape(-1), aff3, aff3)


def _swap_pairs(w):
    s = w.shape
    return jnp.flip(w.reshape(s[:-1] + (s[-1] // 2, 2)), axis=-1).reshape(s)


def _pack_w_in(w_in):
    o_xbc, o_dt, o_cq, o_ckv, o_kr, o_fin = 512, 1280, 1296, 1552, 1680, 1712
    kr = w_in[:, :, o_kr:o_fin]
    pad = jnp.zeros(w_in.shape[:2] + (_IN_SMALL[1] - 2 * SSD_HEADS,), w_in.dtype)
    parts = [w_in[:, :, o_xbc:o_dt], w_in[:, :, o_cq:o_ckv], w_in[:, :, :o_xbc], w_in[:, :, o_fin:],
             w_in[:, :, o_ckv:o_kr], w_in[:, :, o_dt:o_cq], pad,
             jnp.tile(kr, (1, 1, MLA_HEADS)), jnp.tile(_swap_pairs(kr), (1, 1, MLA_HEADS))]
    return jnp.concatenate(parts, axis=2).astype(BF16)


def _pack_w_uq(w_uq):
    w = w_uq.reshape(w_uq.shape[:2] + (MLA_HEADS, QK_NOPE + ROPE_DIM))
    flat = lambda v: v.reshape(v.shape[:2] + (-1,))
    rope = w[..., QK_NOPE:]
    return jnp.concatenate([flat(w[..., :QK_NOPE]), flat(rope), flat(_swap_pairs(rope))],
                           axis=2).astype(BF16)


def _pack_w_ukv(w_ukv):
    w = w_ukv.reshape(w_ukv.shape[:2] + (MLA_HEADS, QK_NOPE + V_DIM))
    flat = lambda v: v.reshape(v.shape[:2] + (-1,))
    return jnp.concatenate([flat(w[..., :QK_NOPE]), flat(w[..., QK_NOPE:])], axis=2).astype(BF16)


def _rope_tables(seq):
    rows = seq // GRID_W
    r, col = jnp.meshgrid(jnp.arange(rows, dtype=F32), jnp.arange(GRID_W, dtype=F32), indexing="ij")
    pairs = ROPE_DIM // 4
    inv = ROPE_BASE ** (-jnp.arange(pairs, dtype=F32) / pairs)
    ang = jnp.concatenate([r.reshape(-1, 1) * inv, col.reshape(-1, 1) * inv], axis=-1)
    cos = jnp.repeat(jnp.cos(ang), 2, axis=1)
    sin = jnp.repeat(jnp.sin(ang), 2, axis=1) * jnp.tile(jnp.array([-1.0, 1.0], F32), ROPE_DIM // 2)
    return jnp.tile(cos, (1, MLA_HEADS)), jnp.tile(sin, (1, MLA_HEADS))


def _pad_lanes(v):
    return jnp.pad(v, ((0, 0), (0, LANES - v.shape[1])))[:, None, :]


def kernel(x_prompt, x_sample, cache_ckv, cache_krope, state_ssm, c, c_ctx,
           w_in, conv_w, conv_b, dt_bias, a_log, d_skip, ssd_norm_g,
           q_norm_g, w_uq, kv_norm_g, w_ukv, w_out, w_mod, b_mod,
           ln1_g, ln1_b, ln2_g, ln2_b, w_router, w_gate, w_up, w_down):
    nb_c, seq_c, _ = x_prompt.shape
    nb_l, seq_l, _ = x_sample.shape
    t_c, t_l = nb_c * seq_c, nb_l * seq_l
    caps = (CAPACITY_FACTOR * t_c // N_EXPERTS, CAPACITY_FACTOR * t_l // N_EXPERTS)

    w_in_p = _pack_w_in(w_in)
    wq = _pack_w_uq(w_uq)
    wkv = _pack_w_ukv(w_ukv)
    row = lambda v: v[:, None, :]
    lp = dict(conv_w=conv_w, conv_b=row(conv_b), dt_bias=_pad_lanes(dt_bias.reshape(DEPTH, -1)),
              a_log=_pad_lanes(a_log.reshape(DEPTH, -1)),
              d_skip=row(jnp.repeat(d_skip, SSD_HEADDIM, axis=1)), ssd_norm_g=row(ssd_norm_g),
              q_norm_g=row(q_norm_g), wq=wq, kv_norm_g=row(kv_norm_g), wkv=wkv,
              w_out=w_out.astype(BF16), ln1_g=row(ln1_g), ln1_b=row(ln1_b),
              wrt=jnp.swapaxes(w_router, 1, 2).astype(BF16))
    ln2 = (row(ln2_g), row(ln2_b))
    cache_kr4 = jnp.tile(cache_krope, (1, 1, 1, MLA_HEADS))
    cos4, sin4 = _rope_tables(seq_l)

    cond = jnp.zeros((SUBLANES, D_MODEL), F32).at[0].set(c_ctx).at[1:1 + nb_l].set(c)
    mod_all = _modulation(cond, w_mod, b_mod).reshape(DEPTH, SUBLANES, 6, D_MODEL)

    streams = [
        dict(x=x_prompt.reshape(t_c, D_MODEL), nb=nb_c, seq=seq_c, cap=caps[0], mod_rows=(0, 1)),
        dict(x=x_sample.reshape(t_l, D_MODEL), nb=nb_l, seq=seq_l, cap=caps[1], mod_rows=(1, nb_l)),
    ]
    prev = [None, None]
    ssm_hist = jnp.zeros((nb_c, DEPTH) + state_ssm.shape[2:], F32)
    kv_hist = (jnp.zeros((nb_c, DEPTH, seq_c, KV_LORA), F32),
               jnp.zeros((nb_c, DEPTH, seq_c, ROPE_DIM), F32))
    for l in range(DEPTH):
        routed = []
        for si, st in enumerate(streams):
            nb, seq, mod_rows = st["nb"], st["seq"], st["mod_rows"]
            x, proj = _in_proj(st["x"], prev[si], mod_all, mod_rows, w_in_p, seq, l)
            if si == 0:
                yssd, ssm_hist = _ssd(proj, lp, nb, seq, None, ssm_hist, l)
                ymla, *kv_hist = _mla(proj, lp, nb, seq, None, kv_hist, l)
            else:
                yssd, _ = _ssd(proj, lp, nb, seq, state_ssm, None, l)
                (ymla,) = _mla(proj, lp, nb, seq, (cache_ckv, cache_kr4, cos4, sin4), None, l)
            yfft = _fnet(proj, nb, seq)
            x1, u2p, logits = _out_proj(x, yssd, ymla, yfft, mod_all, mod_rows, lp, seq, l)
            idx, aff = _route(logits, st["cap"])
            st["x"] = x1
            routed.append((u2p, idx, aff))
        h = _moe_up(routed[0][0], routed[1][0], routed[0][1], routed[1][1], w_gate, w_up, l)
        for si in range(2):
            moe = _moe_down(h, w_down, routed[si][1], routed[si][2], 0 if si == 0 else 2, l)
            prev[si] = (moe,) + ln2
    last = DEPTH - 1
    y_p, y_s = [_final_norm(st["x"], *prev[si], mod_all, st["mod_rows"], st["seq"], last)
                for si, st in enumerate(streams)]
    y_p, y_s = y_p.reshape(x_prompt.shape), y_s.reshape(x_sample.shape)
    return (y_p, y_s, kv_hist[0], kv_hist[1], ssm_hist)
```

```python
import functools

import jax
import jax.numpy as jnp
import numpy as np
from jax import lax
from jax.experimental import pallas as pl
from jax.experimental.pallas import tpu as pltpu

F32 = jnp.float32
BF16 = jnp.bfloat16
I32 = jnp.int32
U32 = jnp.uint32

D_MODEL = 1024
DEPTH = 4
GRID_W = 64
SSD_HEADS = 8
SSD_HEADDIM = 64
SSD_DIM = SSD_HEADS * SSD_HEADDIM
SSD_STATE = 64
SSD_CONV = 5
SSD_CHUNK = 128
SSD_CONV_DIM = SSD_DIM + 4 * SSD_STATE
MLA_HEADS = 4
Q_LORA = 256
KV_LORA = 128
QK_NOPE = 64
ROPE_DIM = 32
V_DIM = 64
MLA_DIM = MLA_HEADS * V_DIM
MLA_SCALE = (QK_NOPE + ROPE_DIM) ** -0.5
ROPE_BASE = 10000.0
Q_TILE = 256
FNET_GW = 64
FNET_DIM = 256
N_EXPERTS = 16
EXPERT_FF = 1024
CAPACITY_FACTOR = 2
DN_ALPHA = (2 * DEPTH) ** 0.25
EPS = 1e-5
_LOG2E = 1.4426950408889634

LANES = 128
SUBLANES = 8
ROW_SLAB = D_MODEL // LANES
PACK_SLAB = ROW_SLAB // 2
VMEM_LIMIT = 60 * 1024 * 1024

_IN_XBC = (0, 768)
_IN_CQ = (768, 256)
_IN_Z = (1024, 512)
_IN_FIN = (1536, 256)
_IN_CKV = (1792, 128)
_IN_SMALL = (1920, 128)
_IN_KR4 = (2048, 256)
IN_PAD = 2304


def _segment_spec(rows, segment):
    start, width = segment
    return pl.BlockSpec((rows, width), lambda b: (b, start // width))


def _cparams(sem, vmem=None):
    return pltpu.CompilerParams(dimension_semantics=sem, vmem_limit_bytes=vmem)


def _ln(x):
    mu = jnp.mean(x, axis=-1, keepdims=True)
    xc = x - mu
    var = jnp.mean(xc * xc, axis=-1, keepdims=True)
    return xc * lax.rsqrt(var + EPS)


def _rms(x):
    return x * lax.rsqrt(jnp.mean(x * x, axis=-1, keepdims=True) + EPS)


def _silu(x):
    return x * (1.0 / (1.0 + jnp.exp(-x)))


def _dot(a, b):
    return jnp.dot(a, b, preferred_element_type=F32)


def _dot_nt(a, b):
    return lax.dot_general(a, b, (((1,), (1,)), ((), ())), preferred_element_type=F32)


def _split3(x):
    hi = x.astype(BF16)
    r1 = x - hi.astype(F32)
    mid = r1.astype(BF16)
    lo = (r1 - mid.astype(F32)).astype(BF16)
    return hi, mid, lo


def _const_spec(shape):
    return pl.BlockSpec(shape, lambda *_: (0,) * len(shape))


def _layer_spec(shape, layer):
    return pl.BlockSpec((None,) + shape, lambda *_: (layer,) + (0,) * len(shape))


_MOD_TN = 1536


def _mod_kernel(c_ref, w_ref, b_ref, o_ref):
    a = _silu(c_ref[...]).astype(BF16)
    o_ref[...] = _dot(a, w_ref[...].astype(BF16)) + b_ref[...]


def _modulation(cond, w_mod, b_mod):
    n = 6 * D_MODEL
    return pl.pallas_call(
        _mod_kernel,
        out_shape=jax.ShapeDtypeStruct((DEPTH, SUBLANES, n), F32),
        grid=(DEPTH, n // _MOD_TN),
        in_specs=[
            pl.BlockSpec((SUBLANES, D_MODEL), lambda l, j: (0, 0)),
            pl.BlockSpec((None, D_MODEL, _MOD_TN), lambda l, j: (l, 0, j)),
            pl.BlockSpec((None, 1, _MOD_TN), lambda l, j: (l, 0, j)),
        ],
        out_specs=pl.BlockSpec((None, SUBLANES, _MOD_TN), lambda l, j: (l, 0, j)),
        compiler_params=_cparams(("arbitrary", "arbitrary")),
        name="modulation",
    )(cond, w_mod, b_mod.reshape(DEPTH, 1, n))


_TM = 512


def _read_rows(slab_ref, tm):
    return jnp.concatenate(
        [slab_ref[pl.ds(j, tm, stride=ROW_SLAB), :] for j in range(ROW_SLAB)], axis=-1)


def _post_norm2(x_ref, moe_ref, pmod_ref, g_ref, b_ref):
    moe = _read_rows(moe_ref, _TM)
    return _ln(DN_ALPHA * x_ref[...] + pmod_ref[5:6, :] * moe) * g_ref[...] + b_ref[...]


def _in_proj_kernel(first, *refs):
    if first:
        x_ref, mod_ref, w_ref, p_ref = refs
        x = x_ref[...]
    else:
        x_ref, moe_ref, pmod_ref, g_ref, b_ref, mod_ref, w_ref, xo_ref, p_ref = refs
        x = _post_norm2(x_ref, moe_ref, pmod_ref, g_ref, b_ref)
        xo_ref[...] = x
    u = _ln(x) * (1.0 + mod_ref[1:2, :]) + mod_ref[0:1, :]
    p_ref[...] = _dot(u.astype(BF16), w_ref[...])


def _mod_spec(mod_rows, t, seq, layer):
    row0, count = mod_rows
    per = seq // _TM if count > 1 else t // _TM
    return pl.BlockSpec((None, None, 6, D_MODEL), lambda i: (layer, row0 + i // per, 0, 0))


def _row_spec(w):
    return pl.BlockSpec((_TM, w), lambda i: (i, 0))


_SLAB_SPEC = pl.BlockSpec((_TM * ROW_SLAB, LANES), lambda i: (i, 0))


def _in_proj(x, prev, mod_all, mod_rows, w_in_p, seq, layer):
    t = x.shape[0]
    mod_spec = _mod_spec(mod_rows, t, seq, layer)
    outs = [jax.ShapeDtypeStruct((t, IN_PAD), F32)]
    out_specs = [_row_spec(IN_PAD)]
    w_spec = _layer_spec((D_MODEL, IN_PAD), layer)
    if prev is None:
        ins = [x, mod_all, w_in_p]
        in_specs = [_row_spec(D_MODEL), mod_spec, w_spec]
    else:
        moe_slab, g, b = prev
        ins = [x, moe_slab, mod_all, g, b, mod_all, w_in_p]
        in_specs = [_row_spec(D_MODEL), _SLAB_SPEC, _mod_spec(mod_rows, t, seq, layer - 1),
                    _layer_spec((1, D_MODEL), layer - 1), _layer_spec((1, D_MODEL), layer - 1),
                    mod_spec, w_spec]
        outs = [jax.ShapeDtypeStruct((t, D_MODEL), F32)] + outs
        out_specs = [_row_spec(D_MODEL)] + out_specs
    res = pl.pallas_call(
        functools.partial(_in_proj_kernel, prev is None),
        out_shape=outs, grid=(t // _TM,), in_specs=in_specs, out_specs=out_specs,
        compiler_params=_cparams(("arbitrary",), VMEM_LIMIT),
        name="in_proj",
    )(*ins)
    if prev is None:
        return (x,) + tuple(res)
    return tuple(res)


def _final_norm_kernel(x_ref, moe_ref, pmod_ref, g_ref, b_ref, o_ref):
    o_ref[...] = _post_norm2(x_ref, moe_ref, pmod_ref, g_ref, b_ref)


def _final_norm(x, moe_slab, g, b, mod_all, mod_rows, seq, layer):
    t = x.shape[0]
    return pl.pallas_call(
        _final_norm_kernel,
        out_shape=jax.ShapeDtypeStruct((t, D_MODEL), F32), grid=(t // _TM,),
        in_specs=[_row_spec(D_MODEL), _SLAB_SPEC, _mod_spec(mod_rows, t, seq, layer),
                  _layer_spec((1, D_MODEL), layer), _layer_spec((1, D_MODEL), layer)],
        out_specs=_row_spec(D_MODEL),
        compiler_params=_cparams(("arbitrary",)),
        name="final_norm",
    )(x, moe_slab, mod_all, g, b)


def _ssd_kernel(has_h0, write_h, seq, *refs):
    refs = list(refs)
    z_ref, xbc_ref, sm_ref, cw_ref, cb_ref, dtb_ref, alog_ref, dsk_ref, ng_ref = refs[:9]
    pos = 9
    if has_h0:
        h0_ref = refs[pos]
    pos += 1
    y_ref = refs[pos]
    pos += 1
    if write_h:
        hf_ref = refs[pos]
        pos += 1
    pad_ref, xs_ref, cm_ref, gm_ref, bt_ref, acol_ref, rows_ref, st_ref = refs[pos:]
    ck = SSD_CHUNK
    nc = seq // ck
    halo = SUBLANES
    nh = SSD_HEADS
    nh2 = 2 * nh

    pad_ref[0:halo, :] = jnp.zeros((halo, SSD_CONV_DIM), F32)
    pad_ref[halo + seq:2 * halo + seq, :] = jnp.zeros((halo, SSD_CONV_DIM), F32)
    pad_ref[halo:halo + seq, :] = xbc_ref[...]
    ri = lax.broadcasted_iota(I32, (ck, ck), 0)
    ci = lax.broadcasted_iota(I32, (ck, ck), 1)
    triu = jnp.where(ri <= ci, 1.0, 0.0).astype(BF16)
    lo_half = ci < SSD_STATE
    lo64 = lax.broadcasted_iota(I32, (SSD_STATE, LANES), 1) < SSD_STATE
    zeros_half = jnp.zeros((SSD_STATE, LANES), BF16)
    fwd_rows = lax.broadcasted_iota(I32, (nh2, ck), 0) < nh

    def prepare_chunk(c, carry):
        cs = pl.multiple_of(c * ck, ck)
        rows = pl.ds(cs, ck)
        cw_blk = 2 * LANES
        for cb in range(SSD_CONV_DIM // cw_blk):
            cols = slice(cb * cw_blk, (cb + 1) * cw_blk)
            win = pad_ref[pl.ds(cs, ck + 2 * halo), cols]
            acc = jnp.zeros((ck, cw_blk), F32) + cb_ref[:, cols]
            for k in range(SSD_CONV):
                s0 = halo + k - (SSD_CONV - 1) // 2
                acc = acc + win[s0:s0 + ck, :] * cw_ref[k:k + 1, cols]
            act = _silu(acc)
            if cb * cw_blk < SSD_DIM:
                xs_ref[rows, cols] = act
                y_ref[rows, cols] = act * dsk_ref[:, cols]
        bm = act[:, :LANES]
        cm = act[:, LANES:]
        cm_ref[rows, :] = cm
        bt_ref[rows, :] = bm.T
        cg2 = jnp.concatenate([jnp.where(lo_half, cm, 0.0), jnp.where(lo_half, 0.0, cm)], axis=0)
        gm2 = _dot_nt(cg2.astype(BF16), bm.astype(BF16))
        gm_ref[rows, 0:LANES] = gm2[:ck]
        gm_ref[rows, LANES:] = gm2[ck:]

        xr = sm_ref[rows, :] + dtb_ref[...]
        dt = jnp.maximum(xr, 0.0) + jnp.log(1.0 + jnp.exp(-jnp.abs(xr)))
        ad = jnp.where(ci < nh2, dt * -jnp.exp(alog_ref[...]),
                       jnp.where(ci < 2 * nh2, pltpu.roll(dt, nh2, axis=1), 0.0))
        adt = ad.T
        at16 = adt[0:nh2, :]
        dt16 = adt[nh2:2 * nh2, :]
        csum = _dot(jnp.concatenate(_split3(at16), axis=0), triu)
        arow = csum[0:nh2] + csum[nh2:2 * nh2] + csum[2 * nh2:3 * nh2]
        acol = jnp.concatenate([arow, jnp.zeros((ck - nh2, ck), F32)], axis=0).T
        alast = jnp.broadcast_to(arow[:, ck - 1:ck], (nh2, ck)) * _LOG2E
        arow_s = jnp.where(fwd_rows, arow, arow - at16) * _LOG2E
        ldt = jnp.log2(dt16)
        rowarg = jnp.where(fwd_rows, arow_s - ldt, arow_s + ldt)
        wrow = jnp.exp2(jnp.where(fwd_rows, alast - arow_s, arow_s))
        acol_ref[rows, :] = jnp.where(ci < nh, acol, acol - ad) * _LOG2E
        rows_ref[pl.ds(pl.multiple_of(c * (4 * nh2), 4 * nh2), 4 * nh2), :] = jnp.concatenate(
            [rowarg, wrow * dt16, jnp.exp2(alast), alast], axis=0)
        return carry

    lax.fori_loop(0, nc, prepare_chunk, 0)

    if has_h0:
        for d in range(2):
            h0 = h0_ref[d].reshape(SSD_DIM, SSD_STATE)
            h0 = jnp.concatenate([h0, jnp.zeros((SSD_DIM, LANES - SSD_STATE), F32)], axis=1)
            st_ref[d] = h0.T[:SSD_STATE, :]
    else:
        st_ref[...] = jnp.zeros(st_ref.shape, F32)

    def chunk_dir(d, c):
        cs = pl.multiple_of(c * ck, ck)
        acol = acol_ref[pl.ds(cs, ck), :]
        rws = rows_ref[pl.ds(pl.multiple_of(c * (4 * nh2), 4 * nh2), 4 * nh2), :]
        cm = cm_ref[pl.ds(cs, ck), :]
        cg = [jnp.where(lo_half, cm, 0.0), jnp.where(lo_half, 0.0, cm)]
        for q in range(nh // 2):
            g = q // 2
            gmat = gm_ref[pl.ds(cs, ck), g * LANES:(g + 1) * LANES]
            btg = bt_ref[pl.ds(cs + g * SSD_STATE, SSD_STATE), :]
            lhs_rows = []
            decl = []
            for hh in range(2):
                hi = d * nh + 2 * q + hh
                acol_b = jnp.broadcast_to(acol[:, hi:hi + 1], (ck, ck))
                rowarg = rws[hi:hi + 1, :]
                if d == 0:
                    lmat = jnp.where(ri >= ci, jnp.exp2(acol_b - rowarg), 0.0)
                    ecol = jnp.exp2(acol_b)
                else:
                    lmat = jnp.where(ci >= ri, jnp.exp2(rowarg - acol_b), 0.0)
                    ecol = jnp.exp2(rws[3 * nh2 + hi:3 * nh2 + hi + 1, :] - acol_b)
                mm = (gmat * lmat).astype(BF16)
                cs_h = (cg[g] * ecol).astype(BF16)
                bw = (btg * rws[nh2 + hi:nh2 + hi + 1, :]).astype(BF16)
                lhs_rows.append(jnp.concatenate([mm, cs_h], axis=1))
                lhs_rows.append(jnp.concatenate([bw, zeros_half], axis=1))
                decl.append(rws[2 * nh2 + hi:2 * nh2 + hi + 1, :])
            lhs = jnp.concatenate(lhs_rows, axis=0)
            cols = slice(q * LANES, (q + 1) * LANES)
            st_pair = st_ref[d, :, cols]
            stb = st_pair.astype(BF16)
            w_st = jnp.concatenate([stb, zeros_half] if g == 0 else [zeros_half, stb], axis=0)
            xs_pair = xs_ref[pl.ds(cs, ck), cols].astype(BF16)
            out = _dot(lhs, jnp.concatenate([xs_pair, w_st], axis=0))
            r2 = ck + SSD_STATE
            y_pair = jnp.where(lo_half, out[0:ck], out[r2:r2 + ck])
            st_new = jnp.where(lo64, out[ck:r2], out[r2 + ck:2 * r2])
            dec = jnp.where(lo64, decl[0], decl[1])
            st_ref[d, :, cols] = st_pair * dec + st_new
            y_ref[pl.ds(cs, ck), cols] += y_pair

    def body(c, carry):
        chunk_dir(0, c)
        chunk_dir(1, nc - 1 - c)
        return carry

    lax.fori_loop(0, nc, body, 0)

    def gate_chunk(c, carry):
        rows = pl.ds(pl.multiple_of(c * ck, ck), ck)
        y = y_ref[rows, :] * _silu(z_ref[rows, :])
        y_ref[rows, :] = _rms(y) * ng_ref[...]
        return carry

    lax.fori_loop(0, nc, gate_chunk, 0)

    if write_h:
        for d in range(2):
            st = jnp.concatenate([st_ref[d], jnp.zeros((LANES - SSD_STATE, SSD_DIM), F32)], axis=0)
            hf_ref[d] = st.T[:, :SSD_STATE].reshape(SSD_HEADS, SSD_HEADDIM, SSD_STATE)


def _ssd(proj, lp, nb, seq, h0, hist, layer):
    has_h0 = h0 is not None
    write_h = not has_h0
    seq_spec = lambda w: pl.BlockSpec((seq, w), lambda b: (b, 0))
    ins = [proj, proj, proj, lp["conv_w"], lp["conv_b"], lp["dt_bias"], lp["a_log"], lp["d_skip"],
           lp["ssd_norm_g"]]
    in_specs = [_segment_spec(seq, _IN_Z), _segment_spec(seq, _IN_XBC), _segment_spec(seq, _IN_SMALL),
                _layer_spec((SSD_CONV, SSD_CONV_DIM), layer), _layer_spec((1, SSD_CONV_DIM), layer),
                _layer_spec((1, LANES), layer), _layer_spec((1, LANES), layer),
                _layer_spec((1, SSD_DIM), layer), _layer_spec((1, SSD_DIM), layer)]
    hshape = (2, SSD_HEADS, SSD_HEADDIM, SSD_STATE)
    layer_block = pl.BlockSpec((None, None) + hshape, lambda b: (b, layer, 0, 0, 0, 0))
    outs = [jax.ShapeDtypeStruct((nb * seq, SSD_DIM), F32)]
    out_specs = [seq_spec(SSD_DIM)]
    aliases = {}
    if has_h0:
        ins.append(h0)
        in_specs.append(layer_block)
    else:
        ins.append(hist)
        in_specs.append(pl.BlockSpec(memory_space=pl.ANY))
        aliases = {len(ins) - 1: 1}
        outs.append(jax.ShapeDtypeStruct(hist.shape, F32))
        out_specs.append(layer_block)
    res = pl.pallas_call(
        functools.partial(_ssd_kernel, has_h0, write_h, seq),
        out_shape=outs, grid=(nb,), in_specs=in_specs, out_specs=out_specs,
        input_output_aliases=aliases,
        scratch_shapes=[pltpu.VMEM((seq + 2 * SUBLANES, SSD_CONV_DIM), F32),
                        pltpu.VMEM((seq, SSD_DIM), F32),
                        pltpu.VMEM((seq, LANES), F32),
                        pltpu.VMEM((seq, 2 * LANES), F32),
                        pltpu.VMEM((seq, LANES), F32),
                        pltpu.VMEM((seq, LANES), F32),
                        pltpu.VMEM((seq // SSD_CHUNK * 8 * SSD_HEADS, LANES), F32),
                        pltpu.VMEM((2, SSD_STATE, SSD_DIM), F32)],
        compiler_params=_cparams(("arbitrary",), VMEM_LIMIT),
        name="ssd",
    )(*ins)
    return res if write_h else (res[0], None)


_MLA_GROUP_ROWS = 1024


def _mla_kernel(has_ctx, seq, group, *refs):
    refs = list(refs)
    cq_ref, ckv_ref, kr_ref, qg_ref, wq_ref, kvg_ref, wkv_ref = refs[:7]
    pos = 7
    if has_ctx:
        cckv_ref, ckr_ref, cos_ref, sin_ref = refs[pos:pos + 4]
        pos += 4
    else:
        pos += 2
    y_ref = refs[pos]
    pos += 1
    if not has_ctx:
        ckvo_ref, kro_ref = refs[pos:pos + 2]

    qn = (_rms(cq_ref[...]) * qg_ref[...]).astype(BF16)
    qall = _dot(qn, wq_ref[...])
    q_nope = qall[:, :MLA_DIM]
    q_rope = qall[:, MLA_DIM:MLA_DIM + LANES]
    ckv_n = _rms(ckv_ref[...]) * kvg_ref[...]
    kr4 = kr_ref[:, :LANES]
    if has_ctx:
        q_rope = q_rope * cos_ref[...] + qall[:, MLA_DIM + LANES:] * sin_ref[...]
        kr4 = kr4 * cos_ref[...] + kr_ref[:, LANES:] * sin_ref[...]
        ckv_all = jnp.concatenate([cckv_ref[...], ckv_n], axis=0)
        kr_all = jnp.concatenate([ckr_ref[...], kr4], axis=0)
    else:
        ckvo_ref[...] = ckv_n.reshape(group, seq, KV_LORA)
        kro_ref[...] = kr4[:, :ROPE_DIM].reshape(group, seq, ROPE_DIM)
        ckv_all = ckv_n
        kr_all = kr4
    kv = _dot(ckv_all.astype(BF16), wkv_ref[...])
    krb = kr_all.astype(BF16)
    lane = lax.broadcasted_iota(I32, (Q_TILE, LANES), 1)
    lo_half = lane < QK_NOPE
    n_keys = kv.shape[0] // group
    for qt in range(group * seq // Q_TILE):
        rows = slice(qt * Q_TILE, (qt + 1) * Q_TILE)
        s0 = (qt * Q_TILE // seq) * n_keys
        keys = slice(s0, s0 + n_keys)
        qr = q_rope[rows] * MLA_SCALE
        for pair in range(MLA_HEADS // 2):
            cols = slice(pair * LANES, (pair + 1) * LANES)
            kcat = jnp.concatenate([kv[keys, cols].astype(BF16), krb[keys]], axis=1)
            vb = kv[keys, MLA_DIM + pair * LANES:MLA_DIM + (pair + 1) * LANES].astype(BF16)
            qn_pair = q_nope[rows, cols] * MLA_SCALE
            outs = []
            for hh in range(2):
                h = 2 * pair + hh
                qa = jnp.where(lo_half if hh == 0 else jnp.logical_not(lo_half), qn_pair, 0.0)
                qb = jnp.where((lane >= h * ROPE_DIM) & (lane < (h + 1) * ROPE_DIM), qr, 0.0)
                s = _dot_nt(jnp.concatenate([qa, qb], axis=1).astype(BF16), kcat)
                p = jnp.exp(s - jnp.max(s, axis=-1, keepdims=True))
                o = _dot(p.astype(BF16), vb)
                outs.append(o / jnp.sum(p, axis=-1, keepdims=True))
            y_ref[rows, cols] = jnp.where(lo_half, outs[0], outs[1])


def _mla(proj, lp, nb, seq, ctx, hist, layer):
    has_ctx = ctx is not None
    group = 1 if has_ctx else _MLA_GROUP_ROWS // seq
    seq_spec = lambda w: pl.BlockSpec((group * seq, w), lambda b: (b, 0))
    ins = [proj, proj, proj, lp["q_norm_g"], lp["wq"], lp["kv_norm_g"], lp["wkv"]]
    rows = group * seq
    in_specs = [_segment_spec(rows, _IN_CQ), _segment_spec(rows, _IN_CKV), _segment_spec(rows, _IN_KR4),
                _layer_spec((1, Q_LORA), layer), _layer_spec((Q_LORA, 2 * MLA_DIM), layer),
                _layer_spec((1, KV_LORA), layer), _layer_spec((KV_LORA, 2 * MLA_DIM), layer)]
    outs = [jax.ShapeDtypeStruct((nb * seq, MLA_DIM), F32)]
    out_specs = [seq_spec(MLA_DIM)]
    if has_ctx:
        cache_ckv, cache_kr4, cos4, sin4 = ctx
        past = cache_ckv.shape[2]
        ins += [cache_ckv, cache_kr4, cos4, sin4]
        in_specs += [pl.BlockSpec((None, None, past, KV_LORA), lambda b: (b, layer, 0, 0)),
                     pl.BlockSpec((None, None, past, LANES), lambda b: (b, layer, 0, 0)),
                     _const_spec((seq, LANES)), _const_spec((seq, LANES))]
    else:
        aliases = {len(ins): 1, len(ins) + 1: 2}
        ins += list(hist)
        in_specs += [pl.BlockSpec(memory_space=pl.ANY)] * 2
        outs += [jax.ShapeDtypeStruct(h.shape, F32) for h in hist]
        out_specs += [pl.BlockSpec((group, None, seq, KV_LORA), lambda b: (b, layer, 0, 0)),
                      pl.BlockSpec((group, None, seq, ROPE_DIM), lambda b: (b, layer, 0, 0))]
    return pl.pallas_call(
        functools.partial(_mla_kernel, has_ctx, seq, group),
        out_shape=outs, grid=(nb // group,), in_specs=in_specs, out_specs=out_specs,
        input_output_aliases={} if has_ctx else aliases,
        compiler_params=_cparams(("arbitrary",), VMEM_LIMIT),
        name="mla",
    )(*ins)


_FNET_ROWS = 1024


def _fnet_kernel(seq, g_ref, cbd_ref, sbd_ref, cs_ref, o_ref):
    gb = g_ref[...].astype(BF16)
    gc = _dot(gb, cbd_ref[...].astype(BF16)).astype(BF16)
    gs = _dot(gb, sbd_ref[...].astype(BF16)).astype(BF16)
    cs = cs_ref[...].astype(BF16)
    for i in range(_FNET_ROWS // seq):
        rows = slice(i * seq, (i + 1) * seq)
        o_ref[rows, :] = _dot(cs, jnp.concatenate([gc[rows], gs[rows]], axis=0))


def _dft_constants(seq):
    k = np.arange(FNET_GW)
    ang = 2.0 * np.pi * np.outer(k, k) / FNET_GW
    eye = np.eye(FNET_DIM // FNET_GW)
    cbd = np.kron(eye, np.cos(ang)) / np.sqrt(FNET_GW)
    sbd = np.kron(eye, np.sin(ang)) / np.sqrt(FNET_GW)
    s = np.arange(seq)
    angs = 2.0 * np.pi * np.outer(s, s) / seq
    cs = np.concatenate([np.cos(angs), -np.sin(angs)], axis=1) / np.sqrt(seq)
    return (jnp.asarray(cbd, F32), jnp.asarray(sbd, F32), jnp.asarray(cs, F32))


def _fnet(proj, nb, seq):
    cbd, sbd, cs = _dft_constants(seq)
    return pl.pallas_call(
        functools.partial(_fnet_kernel, seq),
        out_shape=jax.ShapeDtypeStruct((nb * seq, FNET_DIM), F32), grid=(nb * seq // _FNET_ROWS,),
        in_specs=[_segment_spec(_FNET_ROWS, _IN_FIN),
                  _const_spec((FNET_DIM, FNET_DIM)), _const_spec((FNET_DIM, FNET_DIM)),
                  _const_spec((seq, 2 * seq))],
        out_specs=pl.BlockSpec((_FNET_ROWS, FNET_DIM), lambda b: (b, 0)),
        compiler_params=_cparams(("arbitrary",), VMEM_LIMIT),
        name="fnet",
    )(proj, cbd, sbd, cs)


def _out_proj_kernel(x_ref, ys_ref, ym_ref, yf_ref, mod_ref, w_ref, g_ref, b_ref, wr_ref,
                     x1_ref, u2p_ref, lg_ref):
    y = jnp.concatenate([ys_ref[...].astype(BF16), ym_ref[...].astype(BF16),
                         yf_ref[...].astype(BF16)], axis=1)
    mix = _dot(y, w_ref[...])
    x1 = _ln(DN_ALPHA * x_ref[...] + mod_ref[2:3, :] * mix) * g_ref[...] + b_ref[...]
    x1_ref[...] = x1
    u2 = _ln(x1) * (1.0 + mod_ref[4:5, :]) + mod_ref[3:4, :]
    lg_ref[...] = _dot_nt(wr_ref[...], u2.astype(BF16))
    half = D_MODEL // 2
    words = pltpu.pack_elementwise([u2[:, :half], u2[:, half:]], packed_dtype=BF16)
    for j in range(PACK_SLAB):
        u2p_ref[pl.ds(j, _TM, stride=PACK_SLAB), :] = words[:, j * LANES:(j + 1) * LANES]


def _out_proj(x, yssd, ymla, yfft, mod_all, mod_rows, lp, seq, layer):
    t = x.shape[0]
    return pl.pallas_call(
        _out_proj_kernel,
        out_shape=[jax.ShapeDtypeStruct((t, D_MODEL), F32),
                   jax.ShapeDtypeStruct((t * PACK_SLAB, LANES), U32),
                   jax.ShapeDtypeStruct((N_EXPERTS, t), F32)],
        grid=(t // _TM,),
        in_specs=[_row_spec(D_MODEL), _row_spec(SSD_DIM), _row_spec(MLA_DIM), _row_spec(FNET_DIM),
                  _mod_spec(mod_rows, t, seq, layer), _layer_spec((D_MODEL, D_MODEL), layer),
                  _layer_spec((1, D_MODEL), layer), _layer_spec((1, D_MODEL), layer),
                  _layer_spec((N_EXPERTS, D_MODEL), layer)],
        out_specs=[_row_spec(D_MODEL),
                   pl.BlockSpec((_TM * PACK_SLAB, LANES), lambda i: (i, 0)),
                   pl.BlockSpec((N_EXPERTS, _TM), lambda i: (0, i))],
        compiler_params=_cparams(("arbitrary",), VMEM_LIMIT),
        name="out_proj",
    )(x, yssd, ymla, yfft, mod_all, lp["w_out"], lp["ln1_g"], lp["ln1_b"], lp["wrt"])


_RANK_SPLIT = 64.0


def _cumsum_tokens(x, triu, below):
    within = _dot(x.astype(BF16), triu)
    totals = jnp.broadcast_to(within[:, LANES - 1:LANES], within.shape)
    return within + _dot(below, totals.astype(BF16))


def _route_kernel(nblk, cap, lg_ref, idx_ref, aff_ref, p_ref, sel_ref):
    lg = lg_ref[...]
    e = jnp.exp(lg - jnp.max(lg, axis=0, keepdims=True))
    aff = e / jnp.sum(e, axis=0, keepdims=True)
    aff_ref[...] = aff
    capf = float(cap)

    def count(mask):
        s = jnp.sum(jnp.where(mask, 1.0, 0.0), axis=1, keepdims=True)
        return jnp.sum(s, axis=2, keepdims=True)

    def search(i, lo):
        cand = lo | jnp.left_shift(jnp.int32(1), 30 - i)
        return jnp.where(count(aff >= pltpu.bitcast(cand, F32)) >= capf, cand, lo)

    thr_bits = lax.fori_loop(0, 31, search, jnp.zeros((N_EXPERTS, 1, 1), I32))
    thr = pltpu.bitcast(thr_bits, F32)
    need = capf - count(aff > thr)

    rows = N_EXPERTS * nblk
    ri = lax.broadcasted_iota(I32, (LANES, LANES), 0)
    ci = lax.broadcasted_iota(I32, (LANES, LANES), 1)
    triu = jnp.where(ri <= ci, 1.0, 0.0).astype(BF16)
    rr = lax.broadcasted_iota(I32, (rows, rows), 0)
    rc = lax.broadcasted_iota(I32, (rows, rows), 1)
    below = jnp.where((rc < rr) & (rc >= (rr // nblk) * nblk), 1.0, 0.0).astype(BF16)
    flat = lambda v: v.reshape(rows, LANES)
    gt = flat(jnp.where(aff > thr, 1.0, 0.0))
    eq = flat(jnp.where(aff == thr, 1.0, 0.0))
    need_rows = flat(jnp.broadcast_to(need, aff.shape))
    eq_rank = _cumsum_tokens(eq, triu, below) - eq
    sel = gt + eq * jnp.where(eq_rank < need_rows, 1.0, 0.0)
    rank = _cumsum_tokens(sel, triu, below)
    sel_ref[...] = sel.reshape(N_EXPERTS, nblk, LANES)
    p_ref[...] = rank.reshape(N_EXPERTS, nblk, LANES)

    ones8 = jnp.ones((SUBLANES, LANES), BF16)
    bi = lax.broadcasted_iota(I32, (nblk, nblk), 0)
    bj = lax.broadcasted_iota(I32, (nblk, nblk), 1)
    triu_b = jnp.where(bi <= bj, 1.0, 0.0).astype(BF16)
    blk_ids = lax.broadcasted_iota(I32, (SUBLANES, nblk), 1).astype(F32).astype(BF16)
    r_col = lax.broadcasted_iota(I32, (cap, 1), 0).astype(F32)

    def per_expert(ex, carry):
        pe = p_ref[ex]
        tot_row = _dot_nt(ones8, sel_ref[ex].astype(BF16))[0:1, :]
        upto = _dot(jnp.broadcast_to(tot_row, (SUBLANES, nblk)).astype(BF16), triu_b)[0:1, :]
        onehot = jnp.where((upto - tot_row <= r_col) & (r_col < upto), 1.0, 0.0).astype(BF16)
        p_hi = jnp.floor(pe * (1.0 / _RANK_SPLIT))
        p_lo = pe - p_hi * _RANK_SPLIT
        ranks = _dot(onehot, p_hi.astype(BF16)) * _RANK_SPLIT + _dot(onehot, p_lo.astype(BF16))
        inside = jnp.where(ranks <= r_col, 1.0, 0.0).astype(BF16)
        cnt = _dot_nt(ones8, inside)[0:1, :] + float(LANES) * _dot_nt(blk_ids, onehot)[0:1, :]
        idx_ref[ex] = cnt.astype(I32)
        return carry

    lax.fori_loop(0, N_EXPERTS, per_expert, 0)


def _route(logits, cap):
    t = logits.shape[1]
    nblk = t // LANES
    idx, aff = pl.pallas_call(
        functools.partial(_route_kernel, nblk, cap),
        out_shape=[jax.ShapeDtypeStruct((N_EXPERTS, 1, cap), I32),
                   jax.ShapeDtypeStruct((N_EXPERTS, nblk, LANES), F32)],
        scratch_shapes=[pltpu.VMEM((N_EXPERTS, nblk, LANES), F32),
                        pltpu.VMEM((N_EXPERTS, nblk, LANES), F32)],
        compiler_params=_cparams(None, VMEM_LIMIT),
        name="route",
    )(logits.reshape(N_EXPERTS, nblk, LANES))
    return idx.reshape(N_EXPERTS, cap), aff.reshape(N_EXPERTS, t)


_UP_TF = 512
_UNROLL = 16


def _gather_rows(tile_ref, streams, ex, step, n_steps):
    row0 = 0
    for src_ref, idx_ref, cap in streams:
        cnt = cap // n_steps
        for u in range(cnt):
            r = step * cnt + u
            src = pl.multiple_of(idx_ref[ex * cap + r] * PACK_SLAB, PACK_SLAB)
            dst = pl.multiple_of((row0 + r) * PACK_SLAB, PACK_SLAB)
            tile_ref[pl.ds(dst, PACK_SLAB), :] = src_ref[pl.ds(src, PACK_SLAB), :]
        row0 += cap


def _moe_up_kernel(caps, ua_ref, ub_ref, ia_ref, ib_ref, wg_ref, wu_ref, h_ref, tile_ref, xe_ref):
    ex = pl.program_id(0)
    ff = pl.program_id(1)
    n_steps = EXPERT_FF // _UP_TF
    m = sum(caps)
    streams = ((ua_ref, ia_ref, caps[0]), (ub_ref, ib_ref, caps[1]))

    @pl.when((ex == 0) & (ff == 0))
    def _first_rows():
        def body(i, carry):
            row0 = 0
            for src_ref, idx_ref, cap in streams:
                @pl.when(i < cap // _UNROLL)
                def _(src_ref=src_ref, idx_ref=idx_ref, row0=row0):
                    for u in range(_UNROLL):
                        r = i * _UNROLL + u
                        src = pl.multiple_of(idx_ref[r] * PACK_SLAB, PACK_SLAB)
                        dst = pl.multiple_of((row0 + r) * PACK_SLAB, PACK_SLAB)
                        tile_ref[pl.ds(dst, PACK_SLAB), :] = src_ref[pl.ds(src, PACK_SLAB), :]
                row0 += cap
            return carry

        lax.fori_loop(0, max(caps) // _UNROLL, body, 0)

    @pl.when(ff == 0)
    def _unpack():
        half = D_MODEL // 2
        for j in range(PACK_SLAB):
            w = tile_ref[pl.ds(j, m, stride=PACK_SLAB), :]
            for k in range(2):
                v = pltpu.unpack_elementwise(w, index=k, packed_dtype=BF16, unpacked_dtype=F32)
                lo = k * half + j * LANES
                xe_ref[:, lo:lo + LANES] = v.astype(BF16)

    nxt = jnp.minimum(ex + 1, N_EXPERTS - 1)
    _gather_rows(tile_ref, streams, nxt, ff, n_steps)
    x = xe_ref[...]
    g = _dot(x, wg_ref[...].astype(BF16))
    u = _dot(x, wu_ref[...].astype(BF16))
    h_ref[...] = (_silu(g) * u).astype(BF16)


def _moe_up(u2p_a, u2p_b, idx_a, idx_b, w_gate, w_up, layer):
    caps = (idx_a.shape[1], idx_b.shape[1])
    m = sum(caps)
    smem = pl.BlockSpec(memory_space=pltpu.SMEM)
    w_spec = pl.BlockSpec((None, None, D_MODEL, _UP_TF), lambda e, f: (layer, e, 0, f))
    return pl.pallas_call(
        functools.partial(_moe_up_kernel, caps),
        out_shape=jax.ShapeDtypeStruct((N_EXPERTS, m, EXPERT_FF), BF16),
        grid=(N_EXPERTS, EXPERT_FF // _UP_TF),
        in_specs=[pl.BlockSpec(u2p_a.shape, lambda e, f: (0, 0), pipeline_mode=pl.Buffered(1)),
                  pl.BlockSpec(u2p_b.shape, lambda e, f: (0, 0), pipeline_mode=pl.Buffered(1)),
                  smem, smem, w_spec, w_spec],
        out_specs=pl.BlockSpec((None, m, _UP_TF), lambda e, f: (e, 0, f)),
        scratch_shapes=[pltpu.VMEM((m * PACK_SLAB, LANES), U32), pltpu.VMEM((m, D_MODEL), BF16)],
        compiler_params=_cparams(("arbitrary", "arbitrary"), VMEM_LIMIT),
        name="moe_up",
    )(u2p_a, u2p_b, idx_a.reshape(-1), idx_b.reshape(-1), w_gate, w_up)


_RMW = 8
_DOWN_ACC_LIMIT = 32 * 1024 * 1024


def _scatter_rows(acc_ref, slab_ref, idx_ref, aff_ref, idx0, src0, scale, rows):
    for b0 in range(0, rows, _RMW):
        dsts, vals = [], []
        for u in range(b0, b0 + _RMW):
            tok = idx_ref[idx0 + u]
            dst = pl.multiple_of(tok * ROW_SLAB, ROW_SLAB)
            src = pl.multiple_of(src0 + u * ROW_SLAB, ROW_SLAB)
            vals.append(acc_ref[pl.ds(dst, ROW_SLAB), :]
                        + (aff_ref[tok] * scale) * slab_ref[pl.ds(src, ROW_SLAB), :])
            dsts.append(dst)
        for dst, val in zip(dsts, vals):
            acc_ref[pl.ds(dst, ROW_SLAB), :] = val


def _moe_down_kernel(cap, tn, h_ref, wd_ref, idx_ref, affp_ref, aff_ref, out_ref, acc_ref,
                     slab_a_ref, slab_b_ref):
    ex = pl.program_id(0)
    nn = pl.program_id(1)
    n_steps = D_MODEL // tn
    n_half = tn // LANES
    part = cap // n_steps
    slabs = (slab_a_ref, slab_b_ref)

    @pl.when((ex == 0) & (nn == 0))
    def _zero():
        acc_ref[...] = jnp.zeros(acc_ref.shape, F32)
        slab_b_ref[...] = jnp.zeros(slab_b_ref.shape, F32)

    prev = jnp.maximum(ex - 1, 0)
    live = jnp.where(ex > 0, 1.0, 0.0)
    row0 = nn * part

    def stage(slab_prev_ref, slab_next_ref):
        _scatter_rows(acc_ref, slab_prev_ref, idx_ref, affp_ref, prev * cap + row0,
                      row0 * ROW_SLAB, live, part)
        ye = _dot(h_ref[...], wd_ref[...].astype(BF16))
        for j in range(n_half):
            slab_next_ref[pl.ds(nn * n_half + j, cap, stride=ROW_SLAB), :] = (
                ye[:, j * LANES:(j + 1) * LANES])

    for parity in range(2):
        @pl.when(ex % 2 == parity)
        def _(parity=parity):
            stage(slabs[1 - parity], slabs[parity])

    @pl.when((ex == N_EXPERTS - 1) & (nn == n_steps - 1))
    def _tail():
        def body(i, carry):
            _scatter_rows(acc_ref, slabs[(N_EXPERTS - 1) % 2], idx_ref, aff_ref,
                          ex * cap + i * _RMW, i * (_RMW * ROW_SLAB), 1.0, _RMW)
            return carry

        lax.fori_loop(0, cap // _RMW, body, 0)
        pltpu.sync_copy(acc_ref, out_ref)


def _moe_down(h, w_down, idx, aff, row_block, layer):
    cap = idx.shape[1]
    t = aff.shape[1]
    aff3 = aff[:, None, :]
    tn = D_MODEL if t * D_MODEL * 4 <= _DOWN_ACC_LIMIT else D_MODEL // 2
    gate_spec = lambda shift: pl.BlockSpec(
        (None, None, t), lambda e, n: (jnp.maximum(e - shift, 0), 0, 0), memory_space=pltpu.SMEM)
    return pl.pallas_call(
        functools.partial(_moe_down_kernel, cap, tn),
        out_shape=jax.ShapeDtypeStruct((t * ROW_SLAB, LANES), F32),
        grid=(N_EXPERTS, D_MODEL // tn),
        in_specs=[pl.BlockSpec((None, cap, EXPERT_FF), lambda e, n: (e, row_block, 0)),
                  pl.BlockSpec((None, None, EXPERT_FF, tn), lambda e, n: (layer, e, 0, n)),
                  pl.BlockSpec(memory_space=pltpu.SMEM), gate_spec(1), gate_spec(0)],
        out_specs=pl.BlockSpec(memory_space=pl.ANY),
        scratch_shapes=[pltpu.VMEM((t * ROW_SLAB, LANES), F32),
                        pltpu.VMEM((cap * ROW_SLAB, LANES), F32),
                        pltpu.VMEM((cap * ROW_SLAB, LANES), F32)],
        compiler_params=_cparams(("arbitrary", "arbitrary"), VMEM_LIMIT),
        name="moe_down",
    )(h, w_down, idx.reshape(-1), aff3, aff3)


def _swap_pairs(w):
    s = w.shape
    return jnp.flip(w.reshape(s[:-1] + (s[-1] // 2, 2)), axis=-1).reshape(s)


def _pack_w_in(w_in):
    o_xbc, o_dt, o_cq, o_ckv, o_kr, o_fin = 512, 1280, 1296, 1552, 1680, 1712
    kr = w_in[:, :, o_kr:o_fin]
    pad = jnp.zeros(w_in.shape[:2] + (_IN_SMALL[1] - 2 * SSD_HEADS,), w_in.dtype)
    parts = [w_in[:, :, o_xbc:o_dt], w_in[:, :, o_cq:o_ckv], w_in[:, :, :o_xbc], w_in[:, :, o_fin:],
             w_in[:, :, o_ckv:o_kr], w_in[:, :, o_dt:o_cq], pad,
             jnp.tile(kr, (1, 1, MLA_HEADS)), jnp.tile(_swap_pairs(kr), (1, 1, MLA_HEADS))]
    return jnp.concatenate(parts, axis=2).astype(BF16)


def _pack_w_uq(w_uq):
    w = w_uq.reshape(w_uq.shape[:2] + (MLA_HEADS, QK_NOPE + ROPE_DIM))
    flat = lambda v: v.reshape(v.shape[:2] + (-1,))
    rope = w[..., QK_NOPE:]
    return jnp.concatenate([flat(w[..., :QK_NOPE]), flat(rope), flat(_swap_pairs(rope))],
                           axis=2).astype(BF16)


def _pack_w_ukv(w_ukv):
    w = w_ukv.reshape(w_ukv.shape[:2] + (MLA_HEADS, QK_NOPE + V_DIM))
    flat = lambda v: v.reshape(v.shape[:2] + (-1,))
    return jnp.concatenate([flat(w[..., :QK_NOPE]), flat(w[..., QK_NOPE:])], axis=2).astype(BF16)


def _rope_tables(seq):
    rows = seq // GRID_W
    r, col = jnp.meshgrid(jnp.arange(rows, dtype=F32), jnp.arange(GRID_W, dtype=F32), indexing="ij")
    pairs = ROPE_DIM // 4
    inv = ROPE_BASE ** (-jnp.arange(pairs, dtype=F32) / pairs)
    ang = jnp.concatenate([r.reshape(-1, 1) * inv, col.reshape(-1, 1) * inv], axis=-1)
    cos = jnp.repeat(jnp.cos(ang), 2, axis=1)
    sin = jnp.repeat(jnp.sin(ang), 2, axis=1) * jnp.tile(jnp.array([-1.0, 1.0], F32), ROPE_DIM // 2)
    return jnp.tile(cos, (1, MLA_HEADS)), jnp.tile(sin, (1, MLA_HEADS))


def _pad_lanes(v):
    return jnp.pad(v, ((0, 0), (0, LANES - v.shape[1])))[:, None, :]


def kernel(x_prompt, x_sample, cache_ckv, cache_krope, state_ssm, c, c_ctx,
           w_in, conv_w, conv_b, dt_bias, a_log, d_skip, ssd_norm_g,
           q_norm_g, w_uq, kv_norm_g, w_ukv, w_out, w_mod, b_mod,
           ln1_g, ln1_b, ln2_g, ln2_b, w_router, w_gate, w_up, w_down):
    nb_c, seq_c, _ = x_prompt.shape
    nb_l, seq_l, _ = x_sample.shape
    t_c, t_l = nb_c * seq_c, nb_l * seq_l
    caps = (CAPACITY_FACTOR * t_c // N_EXPERTS, CAPACITY_FACTOR * t_l // N_EXPERTS)

    w_in_p = _pack_w_in(w_in)
    wq = _pack_w_uq(w_uq)
    wkv = _pack_w_ukv(w_ukv)
    row = lambda v: v[:, None, :]
    lp = dict(conv_w=conv_w, conv_b=row(conv_b), dt_bias=_pad_lanes(dt_bias.reshape(DEPTH, -1)),
              a_log=_pad_lanes(a_log.reshape(DEPTH, -1)),
              d_skip=row(jnp.repeat(d_skip, SSD_HEADDIM, axis=1)), ssd_norm_g=row(ssd_norm_g),
              q_norm_g=row(q_norm_g), wq=wq, kv_norm_g=row(kv_norm_g), wkv=wkv,
              w_out=w_out.astype(BF16), ln1_g=row(ln1_g), ln1_b=row(ln1_b),
              wrt=jnp.swapaxes(w_router, 1, 2).astype(BF16))
    ln2 = (row(ln2_g), row(ln2_b))
    cache_kr4 = jnp.tile(cache_krope, (1, 1, 1, MLA_HEADS))
    cos4, sin4 = _rope_tables(seq_l)

    cond = jnp.zeros((SUBLANES, D_MODEL), F32).at[0].set(c_ctx).at[1:1 + nb_l].set(c)
    mod_all = _modulation(cond, w_mod, b_mod).reshape(DEPTH, SUBLANES, 6, D_MODEL)

    streams = [
        dict(x=x_prompt.reshape(t_c, D_MODEL), nb=nb_c, seq=seq_c, cap=caps[0], mod_rows=(0, 1)),
        dict(x=x_sample.reshape(t_l, D_MODEL), nb=nb_l, seq=seq_l, cap=caps[1], mod_rows=(1, nb_l)),
    ]
    prev = [None, None]
    ssm_hist = jnp.zeros((nb_c, DEPTH) + state_ssm.shape[2:], F32)
    kv_hist = (jnp.zeros((nb_c, DEPTH, seq_c, KV_LORA), F32),
               jnp.zeros((nb_c, DEPTH, seq_c, ROPE_DIM), F32))
    for l in range(DEPTH):
        routed = []
        for si, st in enumerate(streams):
            nb, seq, mod_rows = st["nb"], st["seq"], st["mod_rows"]
            x, proj = _in_proj(st["x"], prev[si], mod_all, mod_rows, w_in_p, seq, l)
            if si == 0:
                yssd, ssm_hist = _ssd(proj, lp, nb, seq, None, ssm_hist, l)
                ymla, *kv_hist = _mla(proj, lp, nb, seq, None, kv_hist, l)
            else:
                yssd, _ = _ssd(proj, lp, nb, seq, state_ssm, None, l)
                (ymla,) = _mla(proj, lp, nb, seq, (cache_ckv, cache_kr4, cos4, sin4), None, l)
            yfft = _fnet(proj, nb, seq)
            x1, u2p, logits = _out_proj(x, yssd, ymla, yfft, mod_all, mod_rows, lp, seq, l)
            idx, aff = _route(logits, st["cap"])
            st["x"] = x1
            routed.append((u2p, idx, aff))
        h = _moe_up(routed[0][0], routed[1][0], routed[0][1], routed[1][1], w_gate, w_up, l)
        for si in range(2):
            moe = _moe_down(h, w_down, routed[si][1], routed[si][2], 0 if si == 0 else 2, l)
            prev[si] = (moe,) + ln2
    last = DEPTH - 1
    y_p, y_s = [_final_norm(st["x"], *prev[si], mod_all, st["mod_rows"], st["seq"], last)
                for si, st in enumerate(streams)]
    y_p, y_s = y_p.reshape(x_prompt.shape), y_s.reshape(x_sample.shape)
    return (y_p, y_s, kv_hist[0], kv_hist[1], ssm_hist)
```

```python
import functools

import jax
import jax.numpy as jnp
import numpy as np
from jax import lax
from jax.experimental import pallas as pl
from jax.experimental.pallas import tpu as pltpu

F32 = jnp.float32
BF16 = jnp.bfloat16
I32 = jnp.int32
U32 = jnp.uint32

D_MODEL = 1024
DEPTH = 4
GRID_W = 64
SSD_HEADS = 8
SSD_HEADDIM = 64
SSD_DIM = SSD_HEADS * SSD_HEADDIM
SSD_STATE = 64
SSD_CONV = 5
SSD_CHUNK = 128
SSD_CONV_DIM = SSD_DIM + 4 * SSD_STATE
MLA_HEADS = 4
Q_LORA = 256
KV_LORA = 128
QK_NOPE = 64
ROPE_DIM = 32
V_DIM = 64
MLA_DIM = MLA_HEADS * V_DIM
MLA_SCALE = (QK_NOPE + ROPE_DIM) ** -0.5
ROPE_BASE = 10000.0
Q_TILE = 256
FNET_GW = 64
FNET_DIM = 256
N_EXPERTS = 16
EXPERT_FF = 1024
CAPACITY_FACTOR = 2
DN_ALPHA = (2 * DEPTH) ** 0.25
EPS = 1e-5
_LOG2E = 1.4426950408889634

LANES = 128
SUBLANES = 8
ROW_SLAB = D_MODEL // LANES
PACK_SLAB = ROW_SLAB // 2
VMEM_LIMIT = 60 * 1024 * 1024

_IN_XBC = (0, 768)
_IN_CQ = (768, 256)
_IN_Z = (1024, 512)
_IN_FIN = (1536, 256)
_IN_CKV = (1792, 128)
_IN_SMALL = (1920, 128)
_IN_KR4 = (2048, 256)
IN_PAD = 2304


def _segment_spec(rows, segment):
    start, width = segment
    return pl.BlockSpec((rows, width), lambda b: (b, start // width))


def _cparams(sem, vmem=None):
    return pltpu.CompilerParams(dimension_semantics=sem, vmem_limit_bytes=vmem)


def _ln(x):
    mu = jnp.mean(x, axis=-1, keepdims=True)
    xc = x - mu
    var = jnp.mean(xc * xc, axis=-1, keepdims=True)
    return xc * lax.rsqrt(var + EPS)


def _rms(x):
    return x * lax.rsqrt(jnp.mean(x * x, axis=-1, keepdims=True) + EPS)


def _silu(x):
    return x * (1.0 / (1.0 + jnp.exp(-x)))


def _dot(a, b):
    return jnp.dot(a, b, preferred_element_type=F32)


def _dot_nt(a, b):
    return lax.dot_general(a, b, (((1,), (1,)), ((), ())), preferred_element_type=F32)


def _split3(x):
    hi = x.astype(BF16)
    r1 = x - hi.astype(F32)
    mid = r1.astype(BF16)
    lo = (r1 - mid.astype(F32)).astype(BF16)
    return hi, mid, lo


def _const_spec(shape):
    return pl.BlockSpec(shape, lambda *_: (0,) * len(shape))


def _layer_spec(shape, layer):
    return pl.BlockSpec((None,) + shape, lambda *_: (layer,) + (0,) * len(shape))


_MOD_TN = 1536


def _mod_kernel(c_ref, w_ref, b_ref, o_ref):
    a = _silu(c_ref[...]).astype(BF16)
    o_ref[...] = _dot(a, w_ref[...].astype(BF16)) + b_ref[...]


def _modulation(cond, w_mod, b_mod):
    n = 6 * D_MODEL
    return pl.pallas_call(
        _mod_kernel,
        out_shape=jax.ShapeDtypeStruct((DEPTH, SUBLANES, n), F32),
        grid=(DEPTH, n // _MOD_TN),
        in_specs=[
            pl.BlockSpec((SUBLANES, D_MODEL), lambda l, j: (0, 0)),
            pl.BlockSpec((None, D_MODEL, _MOD_TN), lambda l, j: (l, 0, j)),
            pl.BlockSpec((None, 1, _MOD_TN), lambda l, j: (l, 0, j)),
        ],
        out_specs=pl.BlockSpec((None, SUBLANES, _MOD_TN), lambda l, j: (l, 0, j)),
        compiler_params=_cparams(("arbitrary", "arbitrary")),
        name="modulation",
    )(cond, w_mod, b_mod.reshape(DEPTH, 1, n))


_TM = 512


def _read_rows(slab_ref, tm):
    return jnp.concatenate(
        [slab_ref[pl.ds(j, tm, stride=ROW_SLAB), :] for j in range(ROW_SLAB)], axis=-1)


def _post_norm2(x_ref, moe_ref, pmod_ref, g_ref, b_ref):
    moe = _read_rows(moe_ref, _TM)
    return _ln(DN_ALPHA * x_ref[...] + pmod_ref[5:6, :] * moe) * g_ref[...] + b_ref[...]


def _in_proj_kernel(first, *refs):
    if first:
        x_ref, mod_ref, w_ref, p_ref = refs
        x = x_ref[...]
    else:
        x_ref, moe_ref, pmod_ref, g_ref, b_ref, mod_ref, w_ref, xo_ref, p_ref = refs
        x = _post_norm2(x_ref, moe_ref, pmod_ref, g_ref, b_ref)
        xo_ref[...] = x
    u = _ln(x) * (1.0 + mod_ref[1:2, :]) + mod_ref[0:1, :]
    p_ref[...] = _dot(u.astype(BF16), w_ref[...])


def _mod_spec(mod_rows, t, seq, layer):
    row0, count = mod_rows
    per = seq // _TM if count > 1 else t // _TM
    return pl.BlockSpec((None, None, 6, D_MODEL), lambda i: (layer, row0 + i // per, 0, 0))


def _row_spec(w):
    return pl.BlockSpec((_TM, w), lambda i: (i, 0))


_SLAB_SPEC = pl.BlockSpec((_TM * ROW_SLAB, LANES), lambda i: (i, 0))


def _in_proj(x, prev, mod_all, mod_rows, w_in_p, seq, layer):
    t = x.shape[0]
    mod_spec = _mod_spec(mod_rows, t, seq, layer)
    outs = [jax.ShapeDtypeStruct((t, IN_PAD), F32)]
    out_specs = [_row_spec(IN_PAD)]
    w_spec = _layer_spec((D_MODEL, IN_PAD), layer)
    if prev is None:
        ins = [x, mod_all, w_in_p]
        in_specs = [_row_spec(D_MODEL), mod_spec, w_spec]
    else:
        moe_slab, g, b = prev
        ins = [x, moe_slab, mod_all, g, b, mod_all, w_in_p]
        in_specs = [_row_spec(D_MODEL), _SLAB_SPEC, _mod_spec(mod_rows, t, seq, layer - 1),
                    _layer_spec((1, D_MODEL), layer - 1), _layer_spec((1, D_MODEL), layer - 1),
                    mod_spec, w_spec]
        outs = [jax.ShapeDtypeStruct((t, D_MODEL), F32)] + outs
        out_specs = [_row_spec(D_MODEL)] + out_specs
    res = pl.pallas_call(
        functools.partial(_in_proj_kernel, prev is None),
        out_shape=outs, grid=(t // _TM,), in_specs=in_specs, out_specs=out_specs,
        compiler_params=_cparams(("arbitrary",), VMEM_LIMIT),
        name="in_proj",
    )(*ins)
    if prev is None:
        return (x,) + tuple(res)
    return tuple(res)


def _final_norm_kernel(x_ref, moe_ref, pmod_ref, g_ref, b_ref, o_ref):
    o_ref[...] = _post_norm2(x_ref, moe_ref, pmod_ref, g_ref, b_ref)


def _final_norm(x, moe_slab, g, b, mod_all, mod_rows, seq, layer):
    t = x.shape[0]
    return pl.pallas_call(
        _final_norm_kernel,
        out_shape=jax.ShapeDtypeStruct((t, D_MODEL), F32), grid=(t // _TM,),
        in_specs=[_row_spec(D_MODEL), _SLAB_SPEC, _mod_spec(mod_rows, t, seq, layer),
                  _layer_spec((1, D_MODEL), layer), _layer_spec((1, D_MODEL), layer)],
        out_specs=_row_spec(D_MODEL),
        compiler_params=_cparams(("arbitrary",)),
        name="final_norm",
    )(x, moe_slab, mod_all, g, b)


def _ssd_kernel(has_h0, write_h, seq, *refs):
    refs = list(refs)
    z_ref, xbc_ref, sm_ref, cw_ref, cb_ref, dtb_ref, alog_ref, dsk_ref, ng_ref = refs[:9]
    pos = 9
    if has_h0:
        h0_ref = refs[pos]
    pos += 1
    y_ref = refs[pos]
    pos += 1
    if write_h:
        hf_ref = refs[pos]
        pos += 1
    pad_ref, xs_ref, cm_ref, gm_ref, bt_ref, acol_ref, rows_ref, st_ref = refs[pos:]
    ck = SSD_CHUNK
    nc = seq // ck
    halo = SUBLANES
    nh = SSD_HEADS
    nh2 = 2 * nh

    pad_ref[0:halo, :] = jnp.zeros((halo, SSD_CONV_DIM), F32)
    pad_ref[halo + seq:2 * halo + seq, :] = jnp.zeros((halo, SSD_CONV_DIM), F32)
    pad_ref[halo:halo + seq, :] = xbc_ref[...]
    ri = lax.broadcasted_iota(I32, (ck, ck), 0)
    ci = lax.broadcasted_iota(I32, (ck, ck), 1)
    triu = jnp.where(ri <= ci, 1.0, 0.0).astype(BF16)
    lo_half = ci < SSD_STATE
    lo64 = lax.broadcasted_iota(I32, (SSD_STATE, LANES), 1) < SSD_STATE
    zeros_half = jnp.zeros((SSD_STATE, LANES), BF16)
    fwd_rows = lax.broadcasted_iota(I32, (nh2, ck), 0) < nh

    def prepare_chunk(c, carry):
        cs = pl.multiple_of(c * ck, ck)
        rows = pl.ds(cs, ck)
        cw_blk = 2 * LANES
        for cb in range(SSD_CONV_DIM // cw_blk):
            cols = slice(cb * cw_blk, (cb + 1) * cw_blk)
            win = pad_ref[pl.ds(cs, ck + 2 * halo), cols]
            acc = jnp.zeros((ck, cw_blk), F32) + cb_ref[:, cols]
            for k in range(SSD_CONV):
                s0 = halo + k - (SSD_CONV - 1) // 2
                acc = acc + win[s0:s0 + ck, :] * cw_ref[k:k + 1, cols]
            act = _silu(acc)
            if cb * cw_blk < SSD_DIM:
                xs_ref[rows, cols] = act
                y_ref[rows, cols] = act * dsk_ref[:, cols]
        bm = act[:, :LANES]
        cm = act[:, LANES:]
        cm_ref[rows, :] = cm
        bt_ref[rows, :] = bm.T
        cg2 = jnp.concatenate([jnp.where(lo_half, cm, 0.0), jnp.where(lo_half, 0.0, cm)], axis=0)
        gm2 = _dot_nt(cg2.astype(BF16), bm.astype(BF16))
        gm_ref[rows, 0:LANES] = gm2[:ck]
        gm_ref[rows, LANES:] = gm2[ck:]

        xr = sm_ref[rows, :] + dtb_ref[...]
        dt = jnp.maximum(xr, 0.0) + jnp.log(1.0 + jnp.exp(-jnp.abs(xr)))
        ad = jnp.where(ci < nh2, dt * -jnp.exp(alog_ref[...]),
                       jnp.where(ci < 2 * nh2, pltpu.roll(dt, nh2, axis=1), 0.0))
        adt = ad.T
        at16 = adt[0:nh2, :]
        dt16 = adt[nh2:2 * nh2, :]
        csum = _dot(jnp.concatenate(_split3(at16), axis=0), triu)
        arow = csum[0:nh2] + csum[nh2:2 * nh2] + csum[2 * nh2:3 * nh2]
        acol = jnp.concatenate([arow, jnp.zeros((ck - nh2, ck), F32)], axis=0).T
        alast = jnp.broadcast_to(arow[:, ck - 1:ck], (nh2, ck)) * _LOG2E
        arow_s = jnp.where(fwd_rows, arow, arow - at16) * _LOG2E
        ldt = jnp.log2(dt16)
        rowarg = jnp.where(fwd_rows, arow_s - ldt, arow_s + ldt)
        wrow = jnp.exp2(jnp.where(fwd_rows, alast - arow_s, arow_s))
        acol_ref[rows, :] = jnp.where(ci < nh, acol, acol - ad) * _LOG2E
        rows_ref[pl.ds(pl.multiple_of(c * (4 * nh2), 4 * nh2), 4 * nh2), :] = jnp.concatenate(
            [rowarg, wrow * dt16, jnp.exp2(alast), alast], axis=0)
        return carry

    lax.fori_loop(0, nc, prepare_chunk, 0)

    if has_h0:
        for d in range(2):
            h0 = h0_ref[d].reshape(SSD_DIM, SSD_STATE)
            h0 = jnp.concatenate([h0, jnp.zeros((SSD_DIM, LANES - SSD_STATE), F32)], axis=1)
            st_ref[d] = h0.T[:SSD_STATE, :]
    else:
        st_ref[...] = jnp.zeros(st_ref.shape, F32)

    def chunk_dir(d, c):
        cs = pl.multiple_of(c * ck, ck)
        acol = acol_ref[pl.ds(cs, ck), :]
        rws = rows_ref[pl.ds(pl.multiple_of(c * (4 * nh2), 4 * nh2), 4 * nh2), :]
        cm = cm_ref[pl.ds(cs, ck), :]
        cg = [jnp.where(lo_half, cm, 0.0), jnp.where(lo_half, 0.0, cm)]
        for q in range(nh // 2):
            g = q // 2
            gmat = gm_ref[pl.ds(cs, ck), g * LANES:(g + 1) * LANES]
            btg = bt_ref[pl.ds(cs + g * SSD_STATE, SSD_STATE), :]
            lhs_rows = []
            decl = []
            for hh in range(2):
                hi = d * nh + 2 * q + hh
                acol_b = jnp.broadcast_to(acol[:, hi:hi + 1], (ck, ck))
                rowarg = rws[hi:hi + 1, :]
                if d == 0:
                    lmat = jnp.where(ri >= ci, jnp.exp2(acol_b - rowarg), 0.0)
                    ecol = jnp.exp2(acol_b)
                else:
                    lmat = jnp.where(ci >= ri, jnp.exp2(rowarg - acol_b), 0.0)
                    ecol = jnp.exp2(rws[3 * nh2 + hi:3 * nh2 + hi + 1, :] - acol_b)
                mm = (gmat * lmat).astype(BF16)
                cs_h = (cg[g] * ecol).astype(BF16)
                bw = (btg * rws[nh2 + hi:nh2 + hi + 1, :]).astype(BF16)
                lhs_rows.append(jnp.concatenate([mm, cs_h], axis=1))
                lhs_rows.append(jnp.concatenate([bw, zeros_half], axis=1))
                decl.append(rws[2 * nh2 + hi:2 * nh2 + hi + 1, :])
            lhs = jnp.concatenate(lhs_rows, axis=0)
            cols = slice(q * LANES, (q + 1) * LANES)
            st_pair = st_ref[d, :, cols]
            stb = st_pair.astype(BF16)
            w_st = jnp.concatenate([stb, zeros_half] if g == 0 else [zeros_half, stb], axis=0)
            xs_pair = xs_ref[pl.ds(cs, ck), cols].astype(BF16)
            out = _dot(lhs, jnp.concatenate([xs_pair, w_st], axis=0))
            r2 = ck + SSD_STATE
            y_pair = jnp.where(lo_half, out[0:ck], out[r2:r2 + ck])
            st_new = jnp.where(lo64, out[ck:r2], out[r2 + ck:2 * r2])
            dec = jnp.where(lo64, decl[0], decl[1])
            st_ref[d, :, cols] = st_pair * dec + st_new
            y_ref[pl.ds(cs, ck), cols] += y_pair

    def body(c, carry):
        chunk_dir(0, c)
        chunk_dir(1, nc - 1 - c)
        return carry

    lax.fori_loop(0, nc, body, 0)

    def gate_chunk(c, carry):
        rows = pl.ds(pl.multiple_of(c * ck, ck), ck)
        y = y_ref[rows, :] * _silu(z_ref[rows, :])
        y_ref[rows, :] = _rms(y) * ng_ref[...]
        return carry

    lax.fori_loop(0, nc, gate_chunk, 0)

    if write_h:
        for d in range(2):
            st = jnp.concatenate([st_ref[d], jnp.zeros((LANES - SSD_STATE, SSD_DIM), F32)], axis=0)
            hf_ref[d] = st.T[:, :SSD_STATE].reshape(SSD_HEADS, SSD_HEADDIM, SSD_STATE)


def _ssd(proj, lp, nb, seq, h0, hist, layer):
    has_h0 = h0 is not None
    write_h = not has_h0
    seq_spec = lambda w: pl.BlockSpec((seq, w), lambda b: (b, 0))
    ins = [proj, proj, proj, lp["conv_w"], lp["conv_b"], lp["dt_bias"], lp["a_log"], lp["d_skip"],
           lp["ssd_norm_g"]]
    in_specs = [_segment_spec(seq, _IN_Z), _segment_spec(seq, _IN_XBC), _segment_spec(seq, _IN_SMALL),
                _layer_spec((SSD_CONV, SSD_CONV_DIM), layer), _layer_spec((1, SSD_CONV_DIM), layer),
                _layer_spec((1, LANES), layer), _layer_spec((1, LANES), layer),
                _layer_spec((1, SSD_DIM), layer), _layer_spec((1, SSD_DIM), layer)]
    hshape = (2, SSD_HEADS, SSD_HEADDIM, SSD_STATE)
    layer_block = pl.BlockSpec((None, None) + hshape, lambda b: (b, layer, 0, 0, 0, 0))
    outs = [jax.ShapeDtypeStruct((nb * seq, SSD_DIM), F32)]
    out_specs = [seq_spec(SSD_DIM)]
    aliases = {}
    if has_h0:
        ins.append(h0)
        in_specs.append(layer_block)
    else:
        ins.append(hist)
        in_specs.append(pl.BlockSpec(memory_space=pl.ANY))
        aliases = {len(ins) - 1: 1}
        outs.append(jax.ShapeDtypeStruct(hist.shape, F32))
        out_specs.append(layer_block)
    res = pl.pallas_call(
        functools.partial(_ssd_kernel, has_h0, write_h, seq),
        out_shape=outs, grid=(nb,), in_specs=in_specs, out_specs=out_specs,
        input_output_aliases=aliases,
        scratch_shapes=[pltpu.VMEM((seq + 2 * SUBLANES, SSD_CONV_DIM), F32),
                        pltpu.VMEM((seq, SSD_DIM), F32),
                        pltpu.VMEM((seq, LANES), F32),
                        pltpu.VMEM((seq, 2 * LANES), F32),
                        pltpu.VMEM((seq, LANES), F32),
                        pltpu.VMEM((seq, LANES), F32),
                        pltpu.VMEM((seq // SSD_CHUNK * 8 * SSD_HEADS, LANES), F32),
                        pltpu.VMEM((2, SSD_STATE, SSD_DIM), F32)],
        compiler_params=_cparams(("arbitrary",), VMEM_LIMIT),
        name="ssd",
    )(*ins)
    return res if write_h else (res[0], None)


_MLA_GROUP_ROWS = 1024


def _mla_kernel(has_ctx, seq, group, *refs):
    refs = list(refs)
    cq_ref, ckv_ref, kr_ref, qg_ref, wq_ref, kvg_ref, wkv_ref = refs[:7]
    pos = 7
    if has_ctx:
        cckv_ref, ckr_ref, cos_ref, sin_ref = refs[pos:pos + 4]
        pos += 4
    else:
        pos += 2
    y_ref = refs[pos]
    pos += 1
    if not has_ctx:
        ckvo_ref, kro_ref = refs[pos:pos + 2]

    qn = (_rms(cq_ref[...]) * qg_ref[...]).astype(BF16)
    qall = _dot(qn, wq_ref[...])
    q_nope = qall[:, :MLA_DIM]
    q_rope = qall[:, MLA_DIM:MLA_DIM + LANES]
    ckv_n = _rms(ckv_ref[...]) * kvg_ref[...]
    kr4 = kr_ref[:, :LANES]
    if has_ctx:
        q_rope = q_rope * cos_ref[...] + qall[:, MLA_DIM + LANES:] * sin_ref[...]
        kr4 = kr4 * cos_ref[...] + kr_ref[:, LANES:] * sin_ref[...]
        ckv_all = jnp.concatenate([cckv_ref[...], ckv_n], axis=0)
        kr_all = jnp.concatenate([ckr_ref[...], kr4], axis=0)
    else:
        ckvo_ref[...] = ckv_n.reshape(group, seq, KV_LORA)
        kro_ref[...] = kr4[:, :ROPE_DIM].reshape(group, seq, ROPE_DIM)
        ckv_all = ckv_n
        kr_all = kr4
    kv = _dot(ckv_all.astype(BF16), wkv_ref[...])
    krb = kr_all.astype(BF16)
    lane = lax.broadcasted_iota(I32, (Q_TILE, LANES), 1)
    lo_half = lane < QK_NOPE
    n_keys = kv.shape[0] // group
    for qt in range(group * seq // Q_TILE):
        rows = slice(qt * Q_TILE, (qt + 1) * Q_TILE)
        s0 = (qt * Q_TILE // seq) * n_keys
        keys = slice(s0, s0 + n_keys)
        qr = q_rope[rows] * MLA_SCALE
        for pair in range(MLA_HEADS // 2):
            cols = slice(pair * LANES, (pair + 1) * LANES)
            kcat = jnp.concatenate([kv[keys, cols].astype(BF16), krb[keys]], axis=1)
            vb = kv[keys, MLA_DIM + pair * LANES:MLA_DIM + (pair + 1) * LANES].astype(BF16)
            qn_pair = q_nope[rows, cols] * MLA_SCALE
            outs = []
            for hh in range(2):
                h = 2 * pair + hh
                qa = jnp.where(lo_half if hh == 0 else jnp.logical_not(lo_half), qn_pair, 0.0)
                qb = jnp.where((lane >= h * ROPE_DIM) & (lane < (h + 1) * ROPE_DIM), qr, 0.0)
                s = _dot_nt(jnp.concatenate([qa, qb], axis=1).astype(BF16), kcat)
                p = jnp.exp(s - jnp.max(s, axis=-1, keepdims=True))
                o = _dot(p.astype(BF16), vb)
                outs.append(o / jnp.sum(p, axis=-1, keepdims=True))
            y_ref[rows, cols] = jnp.where(lo_half, outs[0], outs[1])


def _mla(proj, lp, nb, seq, ctx, hist, layer):
    has_ctx = ctx is not None
    group = 1 if has_ctx else _MLA_GROUP_ROWS // seq
    seq_spec = lambda w: pl.BlockSpec((group * seq, w), lambda b: (b, 0))
    ins = [proj, proj, proj, lp["q_norm_g"], lp["wq"], lp["kv_norm_g"], lp["wkv"]]
    rows = group * seq
    in_specs = [_segment_spec(rows, _IN_CQ), _segment_spec(rows, _IN_CKV), _segment_spec(rows, _IN_KR4),
                _layer_spec((1, Q_LORA), layer), _layer_spec((Q_LORA, 2 * MLA_DIM), layer),
                _layer_spec((1, KV_LORA), layer), _layer_spec((KV_LORA, 2 * MLA_DIM), layer)]
    outs = [jax.ShapeDtypeStruct((nb * seq, MLA_DIM), F32)]
    out_specs = [seq_spec(MLA_DIM)]
    if has_ctx:
        cache_ckv, cache_kr4, cos4, sin4 = ctx
        past = cache_ckv.shape[2]
        ins += [cache_ckv, cache_kr4, cos4, sin4]
        in_specs += [pl.BlockSpec((None, None, past, KV_LORA), lambda b: (b, layer, 0, 0)),
                     pl.BlockSpec((None, None, past, LANES), lambda b: (b, layer, 0, 0)),
                     _const_spec((seq, LANES)), _const_spec((seq, LANES))]
    else:
        aliases = {len(ins): 1, len(ins) + 1: 2}
        ins += list(hist)
        in_specs += [pl.BlockSpec(memory_space=pl.ANY)] * 2
        outs += [jax.ShapeDtypeStruct(h.shape, F32) for h in hist]
        out_specs += [pl.BlockSpec((group, None, seq, KV_LORA), lambda b: (b, layer, 0, 0)),
                      pl.BlockSpec((group, None, seq, ROPE_DIM), lambda b: (b, layer, 0, 0))]
    return pl.pallas_call(
        functools.partial(_mla_kernel, has_ctx, seq, group),
        out_shape=outs, grid=(nb // group,), in_specs=in_specs, out_specs=out_specs,
        input_output_aliases={} if has_ctx else aliases,
        compiler_params=_cparams(("arbitrary",), VMEM_LIMIT),
        name="mla",
    )(*ins)


_FNET_ROWS = 2048


def _fnet_kernel(seq, g_ref, cbd_ref, sbd_ref, cs_ref, o_ref):
    gb = g_ref[...].astype(BF16)
    gc = _dot(gb, cbd_ref[...].astype(BF16)).astype(BF16)
    gs = _dot(gb, sbd_ref[...].astype(BF16)).astype(BF16)
    cs = cs_ref[...].astype(BF16)
    for i in range(_FNET_ROWS // seq):
        rows = slice(i * seq, (i + 1) * seq)
        o_ref[rows, :] = _dot(cs, jnp.concatenate([gc[rows], gs[rows]], axis=0))


def _dft_constants(seq):
    k = np.arange(FNET_GW)
    ang = 2.0 * np.pi * np.outer(k, k) / FNET_GW
    eye = np.eye(FNET_DIM // FNET_GW)
    cbd = np.kron(eye, np.cos(ang)) / np.sqrt(FNET_GW)
    sbd = np.kron(eye, np.sin(ang)) / np.sqrt(FNET_GW)
    s = np.arange(seq)
    angs = 2.0 * np.pi * np.outer(s, s) / seq
    cs = np.concatenate([np.cos(angs), -np.sin(angs)], axis=1) / np.sqrt(seq)
    return (jnp.asarray(cbd, F32), jnp.asarray(sbd, F32), jnp.asarray(cs, F32))


def _fnet(proj, nb, seq):
    cbd, sbd, cs = _dft_constants(seq)
    return pl.pallas_call(
        functools.partial(_fnet_kernel, seq),
        out_shape=jax.ShapeDtypeStruct((nb * seq, FNET_DIM), F32), grid=(nb * seq // _FNET_ROWS,),
        in_specs=[_segment_spec(_FNET_ROWS, _IN_FIN),
                  _const_spec((FNET_DIM, FNET_DIM)), _const_spec((FNET_DIM, FNET_DIM)),
                  _const_spec((seq, 2 * seq))],
        out_specs=pl.BlockSpec((_FNET_ROWS, FNET_DIM), lambda b: (b, 0)),
        compiler_params=_cparams(("arbitrary",), VMEM_LIMIT),
        name="fnet",
    )(proj, cbd, sbd, cs)


def _out_proj_kernel(x_ref, ys_ref, ym_ref, yf_ref, mod_ref, w_ref, g_ref, b_ref, wr_ref,
                     x1_ref, u2p_ref, lg_ref):
    y = jnp.concatenate([ys_ref[...].astype(BF16), ym_ref[...].astype(BF16),
                         yf_ref[...].astype(BF16)], axis=1)
    mix = _dot(y, w_ref[...])
    x1 = _ln(DN_ALPHA * x_ref[...] + mod_ref[2:3, :] * mix) * g_ref[...] + b_ref[...]
    x1_ref[...] = x1
    u2 = _ln(x1) * (1.0 + mod_ref[4:5, :]) + mod_ref[3:4, :]
    lg_ref[...] = _dot_nt(wr_ref[...], u2.astype(BF16))
    half = D_MODEL // 2
    words = pltpu.pack_elementwise([u2[:, :half], u2[:, half:]], packed_dtype=BF16)
    for j in range(PACK_SLAB):
        u2p_ref[pl.ds(j, _TM, stride=PACK_SLAB), :] = words[:, j * LANES:(j + 1) * LANES]


def _out_proj(x, yssd, ymla, yfft, mod_all, mod_rows, lp, seq, layer):
    t = x.shape[0]
    return pl.pallas_call(
        _out_proj_kernel,
        out_shape=[jax.ShapeDtypeStruct((t, D_MODEL), F32),
                   jax.ShapeDtypeStruct((t * PACK_SLAB, LANES), U32),
                   jax.ShapeDtypeStruct((N_EXPERTS, t), F32)],
        grid=(t // _TM,),
        in_specs=[_row_spec(D_MODEL), _row_spec(SSD_DIM), _row_spec(MLA_DIM), _row_spec(FNET_DIM),
                  _mod_spec(mod_rows, t, seq, layer), _layer_spec((D_MODEL, D_MODEL), layer),
                  _layer_spec((1, D_MODEL), layer), _layer_spec((1, D_MODEL), layer),
                  _layer_spec((N_EXPERTS, D_MODEL), layer)],
        out_specs=[_row_spec(D_MODEL),
                   pl.BlockSpec((_TM * PACK_SLAB, LANES), lambda i: (i, 0)),
                   pl.BlockSpec((N_EXPERTS, _TM), lambda i: (0, i))],
        compiler_params=_cparams(("arbitrary",), VMEM_LIMIT),
        name="out_proj",
    )(x, yssd, ymla, yfft, mod_all, lp["w_out"], lp["ln1_g"], lp["ln1_b"], lp["wrt"])


_RANK_SPLIT = 64.0


def _cumsum_tokens(x, triu, below):
    within = _dot(x.astype(BF16), triu)
    totals = jnp.broadcast_to(within[:, LANES - 1:LANES], within.shape)
    return within + _dot(below, totals.astype(BF16))


def _route_kernel(nblk, cap, lg_ref, idx_ref, aff_ref, p_ref, sel_ref):
    lg = lg_ref[...]
    e = jnp.exp(lg - jnp.max(lg, axis=0, keepdims=True))
    aff = e / jnp.sum(e, axis=0, keepdims=True)
    aff_ref[...] = aff
    capf = float(cap)

    def count(mask):
        s = jnp.sum(jnp.where(mask, 1.0, 0.0), axis=1, keepdims=True)
        return jnp.sum(s, axis=2, keepdims=True)

    def search(i, lo):
        cand = lo | jnp.left_shift(jnp.int32(1), 30 - i)
        return jnp.where(count(aff >= pltpu.bitcast(cand, F32)) >= capf, cand, lo)

    thr_bits = lax.fori_loop(0, 31, search, jnp.zeros((N_EXPERTS, 1, 1), I32))
    thr = pltpu.bitcast(thr_bits, F32)
    need = capf - count(aff > thr)

    rows = N_EXPERTS * nblk
    ri = lax.broadcasted_iota(I32, (LANES, LANES), 0)
    ci = lax.broadcasted_iota(I32, (LANES, LANES), 1)
    triu = jnp.where(ri <= ci, 1.0, 0.0).astype(BF16)
    rr = lax.broadcasted_iota(I32, (rows, rows), 0)
    rc = lax.broadcasted_iota(I32, (rows, rows), 1)
    below = jnp.where((rc < rr) & (rc >= (rr // nblk) * nblk), 1.0, 0.0).astype(BF16)
    flat = lambda v: v.reshape(rows, LANES)
    gt = flat(jnp.where(aff > thr, 1.0, 0.0))
    eq = flat(jnp.where(aff == thr, 1.0, 0.0))
    need_rows = flat(jnp.broadcast_to(need, aff.shape))
    eq_rank = _cumsum_tokens(eq, triu, below) - eq
    sel = gt + eq * jnp.where(eq_rank < need_rows, 1.0, 0.0)
    rank = _cumsum_tokens(sel, triu, below)
    sel_ref[...] = sel.reshape(N_EXPERTS, nblk, LANES)
    p_ref[...] = rank.reshape(N_EXPERTS, nblk, LANES)

    ones8 = jnp.ones((SUBLANES, LANES), BF16)
    bi = lax.broadcasted_iota(I32, (nblk, nblk), 0)
    bj = lax.broadcasted_iota(I32, (nblk, nblk), 1)
    triu_b = jnp.where(bi <= bj, 1.0, 0.0).astype(BF16)
    blk_ids = lax.broadcasted_iota(I32, (SUBLANES, nblk), 1).astype(F32).astype(BF16)
    r_col = lax.broadcasted_iota(I32, (cap, 1), 0).astype(F32)

    def per_expert(ex, carry):
        pe = p_ref[ex]
        tot_row = _dot_nt(ones8, sel_ref[ex].astype(BF16))[0:1, :]
        upto = _dot(jnp.broadcast_to(tot_row, (SUBLANES, nblk)).astype(BF16), triu_b)[0:1, :]
        onehot = jnp.where((upto - tot_row <= r_col) & (r_col < upto), 1.0, 0.0).astype(BF16)
        p_hi = jnp.floor(pe * (1.0 / _RANK_SPLIT))
        p_lo = pe - p_hi * _RANK_SPLIT
        ranks = _dot(onehot, p_hi.astype(BF16)) * _RANK_SPLIT + _dot(onehot, p_lo.astype(BF16))
        inside = jnp.where(ranks <= r_col, 1.0, 0.0).astype(BF16)
        cnt = _dot_nt(ones8, inside)[0:1, :] + float(LANES) * _dot_nt(blk_ids, onehot)[0:1, :]
        idx_ref[ex] = cnt.astype(I32)
        return carry

    lax.fori_loop(0, N_EXPERTS, per_expert, 0)


def _route(logits, cap):
    t = logits.shape[1]
    nblk = t // LANES
    idx, aff = pl.pallas_call(
        functools.partial(_route_kernel, nblk, cap),
        out_shape=[jax.ShapeDtypeStruct((N_EXPERTS, 1, cap), I32),
                   jax.ShapeDtypeStruct((N_EXPERTS, nblk, LANES), F32)],
        scratch_shapes=[pltpu.VMEM((N_EXPERTS, nblk, LANES), F32),
                        pltpu.VMEM((N_EXPERTS, nblk, LANES), F32)],
        compiler_params=_cparams(None, VMEM_LIMIT),
        name="route",
    )(logits.reshape(N_EXPERTS, nblk, LANES))
    return idx.reshape(N_EXPERTS, cap), aff.reshape(N_EXPERTS, t)


_UP_TF = 512
_UNROLL = 16


def _gather_rows(tile_ref, streams, ex, step, n_steps):
    row0 = 0
    for src_ref, idx_ref, cap in streams:
        cnt = cap // n_steps
        for u in range(cnt):
            r = step * cnt + u
            src = pl.multiple_of(idx_ref[ex * cap + r] * PACK_SLAB, PACK_SLAB)
            dst = pl.multiple_of((row0 + r) * PACK_SLAB, PACK_SLAB)
            tile_ref[pl.ds(dst, PACK_SLAB), :] = src_ref[pl.ds(src, PACK_SLAB), :]
        row0 += cap


def _moe_up_kernel(caps, ua_ref, ub_ref, ia_ref, ib_ref, wg_ref, wu_ref, h_ref, tile_ref, xe_ref):
    ex = pl.program_id(0)
    ff = pl.program_id(1)
    n_steps = EXPERT_FF // _UP_TF
    m = sum(caps)
    streams = ((ua_ref, ia_ref, caps[0]), (ub_ref, ib_ref, caps[1]))

    @pl.when((ex == 0) & (ff == 0))
    def _first_rows():
        def body(i, carry):
            row0 = 0
            for src_ref, idx_ref, cap in streams:
                @pl.when(i < cap // _UNROLL)
                def _(src_ref=src_ref, idx_ref=idx_ref, row0=row0):
                    for u in range(_UNROLL):
                        r = i * _UNROLL + u
                        src = pl.multiple_of(idx_ref[r] * PACK_SLAB, PACK_SLAB)
                        dst = pl.multiple_of((row0 + r) * PACK_SLAB, PACK_SLAB)
                        tile_ref[pl.ds(dst, PACK_SLAB), :] = src_ref[pl.ds(src, PACK_SLAB), :]
                row0 += cap
            return carry

        lax.fori_loop(0, max(caps) // _UNROLL, body, 0)

    @pl.when(ff == 0)
    def _unpack():
        half = D_MODEL // 2
        for j in range(PACK_SLAB):
            w = tile_ref[pl.ds(j, m, stride=PACK_SLAB), :]
            for k in range(2):
                v = pltpu.unpack_elementwise(w, index=k, packed_dtype=BF16, unpacked_dtype=F32)
                lo = k * half + j * LANES
                xe_ref[:, lo:lo + LANES] = v.astype(BF16)

    nxt = jnp.minimum(ex + 1, N_EXPERTS - 1)
    _gather_rows(tile_ref, streams, nxt, ff, n_steps)
    x = xe_ref[...]
    g = _dot(x, wg_ref[...].astype(BF16))
    u = _dot(x, wu_ref[...].astype(BF16))
    h_ref[...] = (_silu(g) * u).astype(BF16)


def _moe_up(u2p_a, u2p_b, idx_a, idx_b, w_gate, w_up, layer):
    caps = (idx_a.shape[1], idx_b.shape[1])
    m = sum(caps)
    smem = pl.BlockSpec(memory_space=pltpu.SMEM)
    w_spec = pl.BlockSpec((None, None, D_MODEL, _UP_TF), lambda e, f: (layer, e, 0, f))
    return pl.pallas_call(
        functools.partial(_moe_up_kernel, caps),
        out_shape=jax.ShapeDtypeStruct((N_EXPERTS, m, EXPERT_FF), BF16),
        grid=(N_EXPERTS, EXPERT_FF // _UP_TF),
        in_specs=[pl.BlockSpec(u2p_a.shape, lambda e, f: (0, 0), pipeline_mode=pl.Buffered(1)),
                  pl.BlockSpec(u2p_b.shape, lambda e, f: (0, 0), pipeline_mode=pl.Buffered(1)),
                  smem, smem, w_spec, w_spec],
        out_specs=pl.BlockSpec((None, m, _UP_TF), lambda e, f: (e, 0, f)),
        scratch_shapes=[pltpu.VMEM((m * PACK_SLAB, LANES), U32), pltpu.VMEM((m, D_MODEL), BF16)],
        compiler_params=_cparams(("arbitrary", "arbitrary"), VMEM_LIMIT),
        name="moe_up",
    )(u2p_a, u2p_b, idx_a.reshape(-1), idx_b.reshape(-1), w_gate, w_up)


_RMW = 8
_DOWN_ACC_LIMIT = 32 * 1024 * 1024


def _scatter_rows(acc_ref, slab_ref, idx_ref, aff_ref, idx0, src0, scale, rows):
    for b0 in range(0, rows, _RMW):
        dsts, vals = [], []
        for u in range(b0, b0 + _RMW):
            tok = idx_ref[idx0 + u]
            dst = pl.multiple_of(tok * ROW_SLAB, ROW_SLAB)
            src = pl.multiple_of(src0 + u * ROW_SLAB, ROW_SLAB)
            vals.append(acc_ref[pl.ds(dst, ROW_SLAB), :]
                        + (aff_ref[tok] * scale) * slab_ref[pl.ds(src, ROW_SLAB), :])
            dsts.append(dst)
        for dst, val in zip(dsts, vals):
            acc_ref[pl.ds(dst, ROW_SLAB), :] = val


def _moe_down_kernel(cap, tn, h_ref, wd_ref, idx_ref, affp_ref, aff_ref, out_ref, acc_ref,
                     slab_a_ref, slab_b_ref):
    ex = pl.program_id(0)
    nn = pl.program_id(1)
    n_steps = D_MODEL // tn
    n_half = tn // LANES
    part = cap // n_steps
    slabs = (slab_a_ref, slab_b_ref)

    @pl.when((ex == 0) & (nn == 0))
    def _zero():
        acc_ref[...] = jnp.zeros(acc_ref.shape, F32)
        slab_b_ref[...] = jnp.zeros(slab_b_ref.shape, F32)

    prev = jnp.maximum(ex - 1, 0)
    live = jnp.where(ex > 0, 1.0, 0.0)
    row0 = nn * part

    def stage(slab_prev_ref, slab_next_ref):
        _scatter_rows(acc_ref, slab_prev_ref, idx_ref, affp_ref, prev * cap + row0,
                      row0 * ROW_SLAB, live, part)
        ye = _dot(h_ref[...], wd_ref[...].astype(BF16))
        for j in range(n_half):
            slab_next_ref[pl.ds(nn * n_half + j, cap, stride=ROW_SLAB), :] = (
                ye[:, j * LANES:(j + 1) * LANES])

    for parity in range(2):
        @pl.when(ex % 2 == parity)
        def _(parity=parity):
            stage(slabs[1 - parity], slabs[parity])

    @pl.when((ex == N_EXPERTS - 1) & (nn == n_steps - 1))
    def _tail():
        def body(i, carry):
            _scatter_rows(acc_ref, slabs[(N_EXPERTS - 1) % 2], idx_ref, aff_ref,
                          ex * cap + i * _RMW, i * (_RMW * ROW_SLAB), 1.0, _RMW)
            return carry

        lax.fori_loop(0, cap // _RMW, body, 0)
        pltpu.sync_copy(acc_ref, out_ref)


def _moe_down(h, w_down, idx, aff, row_block, layer):
    cap = idx.shape[1]
    t = aff.shape[1]
    aff3 = aff[:, None, :]
    tn = D_MODEL if t * D_MODEL * 4 <= _DOWN_ACC_LIMIT else D_MODEL // 2
    gate_spec = lambda shift: pl.BlockSpec(
        (None, None, t), lambda e, n: (jnp.maximum(e - shift, 0), 0, 0), memory_space=pltpu.SMEM)
    return pl.pallas_call(
        functools.partial(_moe_down_kernel, cap, tn),
        out_shape=jax.ShapeDtypeStruct((t * ROW_SLAB, LANES), F32),
        grid=(N_EXPERTS, D_MODEL // tn),
        in_specs=[pl.BlockSpec((None, cap, EXPERT_FF), lambda e, n: (e, row_block, 0)),
                  pl.BlockSpec((None, None, EXPERT_FF, tn), lambda e, n: (layer, e, 0, n)),
                  pl.BlockSpec(memory_space=pltpu.SMEM), gate_spec(1), gate_spec(0)],
        out_specs=pl.BlockSpec(memory_space=pl.ANY),
        scratch_shapes=[pltpu.VMEM((t * ROW_SLAB, LANES), F32),
                        pltpu.VMEM((cap * ROW_SLAB, LANES), F32),
                        pltpu.VMEM((cap * ROW_SLAB, LANES), F32)],
        compiler_params=_cparams(("arbitrary", "arbitrary"), VMEM_LIMIT),
        name="moe_down",
    )(h, w_down, idx.reshape(-1), aff3, aff3)


def _swap_pairs(w):
    s = w.shape
    return jnp.flip(w.reshape(s[:-1] + (s[-1] // 2, 2)), axis=-1).reshape(s)


def _pack_w_in(w_in):
    o_xbc, o_dt, o_cq, o_ckv, o_kr, o_fin = 512, 1280, 1296, 1552, 1680, 1712
    kr = w_in[:, :, o_kr:o_fin]
    pad = jnp.zeros(w_in.shape[:2] + (_IN_SMALL[1] - 2 * SSD_HEADS,), w_in.dtype)
    parts = [w_in[:, :, o_xbc:o_dt], w_in[:, :, o_cq:o_ckv], w_in[:, :, :o_xbc], w_in[:, :, o_fin:],
             w_in[:, :, o_ckv:o_kr], w_in[:, :, o_dt:o_cq], pad,
             jnp.tile(kr, (1, 1, MLA_HEADS)), jnp.tile(_swap_pairs(kr), (1, 1, MLA_HEADS))]
    return jnp.concatenate(parts, axis=2).astype(BF16)


def _pack_w_uq(w_uq):
    w = w_uq.reshape(w_uq.shape[:2] + (MLA_HEADS, QK_NOPE + ROPE_DIM))
    flat = lambda v: v.reshape(v.shape[:2] + (-1,))
    rope = w[..., QK_NOPE:]
    return jnp.concatenate([flat(w[..., :QK_NOPE]), flat(rope), flat(_swap_pairs(rope))],
                           axis=2).astype(BF16)


def _pack_w_ukv(w_ukv):
    w = w_ukv.reshape(w_ukv.shape[:2] + (MLA_HEADS, QK_NOPE + V_DIM))
    flat = lambda v: v.reshape(v.shape[:2] + (-1,))
    return jnp.concatenate([flat(w[..., :QK_NOPE]), flat(w[..., QK_NOPE:])], axis=2).astype(BF16)


def _rope_tables(seq):
    rows = seq // GRID_W
    r, col = jnp.meshgrid(jnp.arange(rows, dtype=F32), jnp.arange(GRID_W, dtype=F32), indexing="ij")
    pairs = ROPE_DIM // 4
    inv = ROPE_BASE ** (-jnp.arange(pairs, dtype=F32) / pairs)
    ang = jnp.concatenate([r.reshape(-1, 1) * inv, col.reshape(-1, 1) * inv], axis=-1)
    cos = jnp.repeat(jnp.cos(ang), 2, axis=1)
    sin = jnp.repeat(jnp.sin(ang), 2, axis=1) * jnp.tile(jnp.array([-1.0, 1.0], F32), ROPE_DIM // 2)
    return jnp.tile(cos, (1, MLA_HEADS)), jnp.tile(sin, (1, MLA_HEADS))


def _pad_lanes(v):
    return jnp.pad(v, ((0, 0), (0, LANES - v.shape[1])))[:, None, :]


def kernel(x_prompt, x_sample, cache_ckv, cache_krope, state_ssm, c, c_ctx,
           w_in, conv_w, conv_b, dt_bias, a_log, d_skip, ssd_norm_g,
           q_norm_g, w_uq, kv_norm_g, w_ukv, w_out, w_mod, b_mod,
           ln1_g, ln1_b, ln2_g, ln2_b, w_router, w_gate, w_up, w_down):
    nb_c, seq_c, _ = x_prompt.shape
    nb_l, seq_l, _ = x_sample.shape
    t_c, t_l = nb_c * seq_c, nb_l * seq_l
    caps = (CAPACITY_FACTOR * t_c // N_EXPERTS, CAPACITY_FACTOR * t_l // N_EXPERTS)

    w_in_p = _pack_w_in(w_in)
    wq = _pack_w_uq(w_uq)
    wkv = _pack_w_ukv(w_ukv)
    row = lambda v: v[:, None, :]
    lp = dict(conv_w=conv_w, conv_b=row(conv_b), dt_bias=_pad_lanes(dt_bias.reshape(DEPTH, -1)),
              a_log=_pad_lanes(a_log.reshape(DEPTH, -1)),
              d_skip=row(jnp.repeat(d_skip, SSD_HEADDIM, axis=1)), ssd_norm_g=row(ssd_norm_g),
              q_norm_g=row(q_norm_g), wq=wq, kv_norm_g=row(kv_norm_g), wkv=wkv,
              w_out=w_out.astype(BF16), ln1_g=row(ln1_g), ln1_b=row(ln1_b),
              wrt=jnp.swapaxes(w_router, 1, 2).astype(BF16))
    ln2 = (row(ln2_g), row(ln2_b))
    cache_kr4 = jnp.tile(cache_krope, (1, 1, 1, MLA_HEADS))
    cos4, sin4 = _rope_tables(seq_l)

    cond = jnp.zeros((SUBLANES, D_MODEL), F32).at[0].set(c_ctx).at[1:1 + nb_l].set(c)
    mod_all = _modulation(cond, w_mod, b_mod).reshape(DEPTH, SUBLANES, 6, D_MODEL)

    streams = [
        dict(x=x_prompt.reshape(t_c, D_MODEL), nb=nb_c, seq=seq_c, cap=caps[0], mod_rows=(0, 1)),
        dict(x=x_sample.reshape(t_l, D_MODEL), nb=nb_l, seq=seq_l, cap=caps[1], mod_rows=(1, nb_l)),
    ]
    prev = [None, None]
    ssm_hist = jnp.zeros((nb_c, DEPTH) + state_ssm.shape[2:], F32)
    kv_hist = (jnp.zeros((nb_c, DEPTH, seq_c, KV_LORA), F32),
               jnp.zeros((nb_c, DEPTH, seq_c, ROPE_DIM), F32))
    for l in range(DEPTH):
        routed = []
        for si, st in enumerate(streams):
            nb, seq, mod_rows = st["nb"], st["seq"], st["mod_rows"]
            x, proj = _in_proj(st["x"], prev[si], mod_all, mod_rows, w_in_p, seq, l)
            if si == 0:
                yssd, ssm_hist = _ssd(proj, lp, nb, seq, None, ssm_hist, l)
                ymla, *kv_hist = _mla(proj, lp, nb, seq, None, kv_hist, l)
            else:
                yssd, _ = _ssd(proj, lp, nb, seq, state_ssm, None, l)
                (ymla,) = _mla(proj, lp, nb, seq, (cache_ckv, cache_kr4, cos4, sin4), None, l)
            yfft = _fnet(proj, nb, seq)
            x1, u2p, logits = _out_proj(x, yssd, ymla, yfft, mod_all, mod_rows, lp, seq, l)
            idx, aff = _route(logits, st["cap"])
            st["x"] = x1
            routed.append((u2p, idx, aff))
        h = _moe_up(routed[0][0], routed[1][0], routed[0][1], routed[1][1], w_gate, w_up, l)
        for si in range(2):
            moe = _moe_down(h, w_down, routed[si][1], routed[si][2], 0 if si == 0 else 2, l)
            prev[si] = (moe,) + ln2
    last = DEPTH - 1
    y_p, y_s = [_final_norm(st["x"], *prev[si], mod_all, st["mod_rows"], st["seq"], last)
                for si, st in enumerate(streams)]
    y_p, y_s = y_p.reshape(x_prompt.shape), y_s.reshape(x_sample.shape)
    return (y_p, y_s, kv_hist[0], kv_hist[1], ssm_hist)
```
